```python
import jax, jax.numpy as jnp
from jax import lax
import numpy as np

D_MODEL = 1024
BATCH = 4
SEQ = 4096
DEPTH = 2

GRID_W = 64
D_GROUP = 256
N_GROUPS = 4
D_MIX = N_GROUPS * D_GROUP
HEAD_DIM = 64
CONV_A_WIDTH = 31
CONV_B_WIDTH = 3
SWA_Q_HEADS = D_GROUP // HEAD_DIM
SWA_KV_HEADS = 2
SWA_WINDOW = 128
SWA_BLOCK = 128
NA_HEADS = D_GROUP // HEAD_DIM
NA_KH_MAX = 8
NA_KW = 16
ROPE_THETA = 10000.0
EPS = 1e-6
NEG_INF = -1e30

SPLIT_SIZES = (
    D_GROUP, D_GROUP, D_GROUP,
    D_GROUP, D_GROUP, D_GROUP, D_GROUP,
    SWA_Q_HEADS * HEAD_DIM, SWA_KV_HEADS * HEAD_DIM,
    SWA_KV_HEADS * HEAD_DIM, D_GROUP,
    NA_HEADS * HEAD_DIM, NA_HEADS * HEAD_DIM,
    NA_HEADS * HEAD_DIM, D_GROUP,
)
D_IN = sum(SPLIT_SIZES)

kernel_name = "hymba_style_bidir_hybrid_encoder"


def rmsnorm(x, g):
    x32 = x.astype(jnp.float32)
    y = x32 * lax.rsqrt(jnp.mean(x32 * x32, axis=-1, keepdims=True) + EPS)
    return y.astype(x.dtype) * g


def layernorm(x, g, b):
    x32 = x.astype(jnp.float32)
    mu = jnp.mean(x32, axis=-1, keepdims=True)
    xc = x32 - mu
    y = xc * lax.rsqrt(jnp.mean(xc * xc, axis=-1, keepdims=True) + EPS)
    return y.astype(x.dtype) * g + b


def depthwise_conv(x, w):
    width, ch = w.shape
    pad = (width - 1) // 2
    return lax.conv_general_dilated(
        x, w[:, None, :], window_strides=(1,), padding=[(pad, pad)],
        dimension_numbers=("NWC", "WIO", "NWC"), feature_group_count=ch)


def rope(x, pos):
    d = x.shape[-1]
    inv_freq = ROPE_THETA ** (-jnp.arange(0, d, 2, dtype=jnp.float32) / d)
    ang = pos[:, None] * inv_freq[None, :]
    cos = jnp.cos(ang)[None, :, None, :].astype(x.dtype)
    sin = jnp.sin(ang)[None, :, None, :].astype(x.dtype)
    x1, x2 = x[..., : d // 2], x[..., d // 2:]
    return jnp.concatenate([x1 * cos - x2 * sin, x2 * cos + x1 * sin], axis=-1)


def conformer_conv(u, v, conv_w, conv_b, ln_g, ln_b):
    h = u * jax.nn.sigmoid(v)
    h = depthwise_conv(h, conv_w) + conv_b
    h = layernorm(h, ln_g, ln_b)
    return jax.nn.silu(h)


def short_gated_conv(bg, cg, xv, conv_w):
    return bg * depthwise_conv(cg * xv, conv_w)


def window_gqa(q, k, v, sink):
    b, s, hq, d = q.shape
    hkv = k.shape[2]
    g = hq // hkv
    nb = s // SWA_BLOCK
    pos = jnp.arange(s, dtype=jnp.float32)
    q = rope(q, pos)
    k = rope(k, pos)
    qb = q.reshape(b, nb, SWA_BLOCK, hkv, g, d)
    pad = ((0, 0), (SWA_BLOCK, SWA_BLOCK), (0, 0), (0, 0))
    kp = jnp.pad(k, pad).reshape(b, nb + 2, SWA_BLOCK, hkv, d)
    vp = jnp.pad(v, pad).reshape(b, nb + 2, SWA_BLOCK, hkv, d)
    kw = jnp.concatenate([kp[:, :-2], kp[:, 1:-1], kp[:, 2:]], axis=2)
    vw = jnp.concatenate([vp[:, :-2], vp[:, 1:-1], vp[:, 2:]], axis=2)
    scores = jnp.einsum("bnqhgd,bnkhd->bnhgqk", qb, kw).astype(jnp.float32) * (d ** -0.5)
    blk = jnp.arange(nb)[:, None, None]
    qpos = blk * SWA_BLOCK + jnp.arange(SWA_BLOCK)[None, :, None]
    kpos = (blk - 1) * SWA_BLOCK + jnp.arange(3 * SWA_BLOCK)[None, None, :]
    valid = (jnp.abs(kpos - qpos) <= SWA_WINDOW) & (kpos >= 0) & (kpos < s)
    scores = jnp.where(valid[None, :, None, None], scores, NEG_INF)
    sink_col = jnp.broadcast_to(sink.astype(jnp.float32).reshape(1, 1, hkv, g, 1, 1),
                                scores.shape[:-1] + (1,))
    p = jax.nn.softmax(jnp.concatenate([scores, sink_col], axis=-1), axis=-1)[..., :-1]
    o = jnp.einsum("bnhgqk,bnkhd->bnqhgd", p.astype(v.dtype), vw)
    return o.reshape(b, s, hq * d)


def neighborhood_attn(q, k, v, rpb):
    b, s, h, d = q.shape
    rows = s // GRID_W
    kh = min(NA_KH_MAX, rows)
    r = jnp.arange(rows)
    r0 = jnp.clip(r - kh // 2, 0, rows - kh)
    row_idx = r0[:, None] + jnp.arange(kh)[None, :]
    qg = q.reshape(b, rows, GRID_W, h, d)
    kg = jnp.take(k.reshape(b, rows, GRID_W, h, d), row_idx, axis=1)
    vg = jnp.take(v.reshape(b, rows, GRID_W, h, d), row_idx, axis=1)
    scores = jnp.einsum("brqhd,brikhd->brhqik", qg, kg).astype(jnp.float32) * (d ** -0.5)
    c = jnp.arange(GRID_W)
    c0 = jnp.clip(c - NA_KW // 2, 0, GRID_W - NA_KW)
    col_ok = (c[None, :] >= c0[:, None]) & (c[None, :] < c0[:, None] + NA_KW)
    dr = row_idx - r[:, None] + (NA_KH_MAX - 1)
    dc = jnp.clip(c[None, :] - c[:, None], -(NA_KW - 1), NA_KW - 1) + (NA_KW - 1)
    bias = jnp.take(rpb[:, dr], dc, axis=-1)
    bias = bias.transpose(1, 0, 3, 2, 4).astype(jnp.float32)
    scores = jnp.where(col_ok[:, None, :], scores + bias[None], NEG_INF)
    p = jax.nn.softmax(scores.reshape(b, rows, h, GRID_W, kh * GRID_W), axis=-1)
    p = p.reshape(scores.shape).astype(v.dtype)
    o = jnp.einsum("brhqik,brikhd->brqhd", p, vg)
    return o.reshape(b, s, h * d)


def hybrid_layer(x, norm_g, w_in, w_out, conv_a_w, conv_a_b, ln_a_g, ln_a_b,
                 conv_b_w, swa_sink, na_rpb):
    b, s, _ = x.shape
    h = rmsnorm(x, norm_g)
    proj = jnp.einsum("bsd,de->bse", h, w_in)
    split_points = np.cumsum(SPLIT_SIZES)[:-1].tolist()
    (a_u, a_v, a_z,
     b_b, b_c, b_x, b_z,
     c_q, c_k, c_v, c_z,
     d_q, d_k, d_v, d_z) = jnp.split(proj, split_points, axis=-1)
    y_a = conformer_conv(a_u, a_v, conv_a_w, conv_a_b, ln_a_g, ln_a_b) * jax.nn.silu(a_z)
    y_b = short_gated_conv(b_b, b_c, b_x, conv_b_w) * jax.nn.silu(b_z)
    y_c = window_gqa(c_q.reshape(b, s, SWA_Q_HEADS, HEAD_DIM),
                     c_k.reshape(b, s, SWA_KV_HEADS, HEAD_DIM),
                     c_v.reshape(b, s, SWA_KV_HEADS, HEAD_DIM), swa_sink) * jax.nn.silu(c_z)
    y_d = neighborhood_attn(d_q.reshape(b, s, NA_HEADS, HEAD_DIM),
                            d_k.reshape(b, s, NA_HEADS, HEAD_DIM),
                            d_v.reshape(b, s, NA_HEADS, HEAD_DIM), na_rpb) * jax.nn.silu(d_z)
    y = jnp.concatenate([y_a, y_b, y_c, y_d], axis=-1)
    return x + jnp.einsum("bse,ed->bsd", y, w_out)


def setup_inputs(seed: int = 0) -> dict:
    key = jax.random.key(seed)
    ks = jax.random.split(key, 12)
    f32 = jnp.float32
    x = jax.random.normal(ks[0], (BATCH, SEQ, D_MODEL), f32)
    norm_g = 1.0 + 0.05 * jax.random.normal(ks[1], (DEPTH, D_MODEL), f32)
    w_in = jax.random.normal(ks[2], (DEPTH, D_MODEL, D_IN), f32) * D_MODEL ** -0.5
    w_out = jax.random.normal(ks[3], (DEPTH, D_MIX, D_MODEL), f32) * D_MIX ** -0.5
    conv_a_w = jax.random.normal(ks[4], (DEPTH, CONV_A_WIDTH, D_GROUP), f32) * CONV_A_WIDTH ** -0.5
    conv_a_b = 0.02 * jax.random.normal(ks[5], (DEPTH, D_GROUP), f32)
    ln_a_g = 1.0 + 0.05 * jax.random.normal(ks[6], (DEPTH, D_GROUP), f32)
    ln_a_b = 0.02 * jax.random.normal(ks[7], (DEPTH, D_GROUP), f32)
    conv_b_w = jax.random.normal(ks[8], (DEPTH, CONV_B_WIDTH, D_GROUP), f32) * CONV_B_WIDTH ** -0.5
    swa_sink = jax.random.normal(ks[9], (DEPTH, SWA_Q_HEADS), f32)
    na_rpb = 0.1 * jax.random.normal(ks[10], (DEPTH, NA_HEADS, 2 * NA_KH_MAX - 1, 2 * NA_KW - 1), f32)
    final_norm_g = 1.0 + 0.05 * jax.random.normal(ks[11], (D_MODEL,), f32)
    return {"x": x, "norm_g": norm_g, "w_in": w_in, "w_out": w_out,
            "conv_a_w": conv_a_w, "conv_a_b": conv_a_b, "ln_a_g": ln_a_g, "ln_a_b": ln_a_b,
            "conv_b_w": conv_b_w, "swa_sink": swa_sink, "na_rpb": na_rpb,
            "final_norm_g": final_norm_g}


def reference(x, norm_g, w_in, w_out, conv_a_w, conv_a_b, ln_a_g, ln_a_b,
              conv_b_w, swa_sink, na_rpb, final_norm_g):
    for l in range(DEPTH):
        x = hybrid_layer(x, norm_g[l], w_in[l], w_out[l], conv_a_w[l], conv_a_b[l],
                         ln_a_g[l], ln_a_b[l], conv_b_w[l], swa_sink[l], na_rpb[l])
    return rmsnorm(x, final_norm_g)
```

```python
import functools

import numpy as np
import jax
import jax.numpy as jnp
from jax import lax
from jax.experimental import pallas as pl
from jax.experimental.pallas import tpu as pltpu

D_MODEL = 1024
D_GROUP = 256
HEAD_DIM = 64
HALF = HEAD_DIM // 2
GRID_W = 64
CONV_A_WIDTH = 31
CONV_A_PAD = (CONV_A_WIDTH - 1) // 2
CONV_B_WIDTH = 3
SWA_WINDOW = 128
SWA_BLOCK = 128
NA_KH = 8
NA_KW = 16
ROPE_THETA = 10000.0
EPS = 1e-6
NEG_INF = -1e30
D_IN = 3584

COL_A_U, COL_A_V, COL_A_Z = 0, 256, 512
COL_B_B, COL_B_C, COL_B_X, COL_B_Z = 768, 1024, 1280, 1536
COL_C_Q, COL_C_K, COL_C_V, COL_C_Z = 1792, 2048, 2176, 2304
COL_D_Q, COL_D_K, COL_D_V, COL_D_Z = 2560, 2816, 3072, 3328

LANES = 128
HALO = 16
VMEM_LIMIT = 56 * 1024 * 1024

TM_PROJ = 512
T_MIX = 512
PROJ_CHUNK = 512


def _silu(x):
    return x * jax.nn.sigmoid(x)


def _in_proj_kernel(x_ref, g_ref, w_ref, cos_ref, sa_ref, sb_ref, o_ref):
    x = x_ref[0]
    ms = jnp.mean(x * x, axis=-1, keepdims=True)
    h = (x * lax.rsqrt(ms + EPS) * g_ref[...]).astype(jnp.bfloat16)
    cos, sa, sb = cos_ref[...], sa_ref[...], sb_ref[...]

    def rope(t):
        return t * cos + pltpu.roll(t, LANES - HALF, 1) * sa + pltpu.roll(t, HALF, 1) * sb

    for c0 in range(0, D_IN, PROJ_CHUNK):
        acc = jnp.dot(h, w_ref[:, c0:c0 + PROJ_CHUNK], preferred_element_type=jnp.float32)
        for s0 in range(c0, c0 + PROJ_CHUNK, LANES):
            t = acc[:, s0 - c0:s0 - c0 + LANES]
            if COL_C_Q <= s0 < COL_C_K or COL_D_Q <= s0 < COL_D_K:
                scale = HEAD_DIM ** -0.5
            else:
                scale = None
            if COL_C_Q <= s0 < COL_C_V:
                t = rope(t)
            if scale is not None:
                t = t * scale
            o_ref[0, :, s0:s0 + LANES] = t.astype(o_ref.dtype)


def _in_proj(x, g, w, cos, sa, sb):
    b, s, d = x.shape
    return pl.pallas_call(
        _in_proj_kernel,
        grid=(b, s // TM_PROJ),
        in_specs=[
            pl.BlockSpec((1, TM_PROJ, d), lambda bi, i: (bi, i, 0)),
            pl.BlockSpec((1, d), lambda bi, i: (0, 0)),
            pl.BlockSpec((d, D_IN), lambda bi, i: (0, 0)),
            pl.BlockSpec((TM_PROJ, LANES), lambda bi, i: (i, 0)),
            pl.BlockSpec((TM_PROJ, LANES), lambda bi, i: (i, 0)),
            pl.BlockSpec((TM_PROJ, LANES), lambda bi, i: (i, 0)),
        ],
        out_specs=pl.BlockSpec((1, TM_PROJ, D_IN), lambda bi, i: (bi, i, 0)),
        out_shape=jax.ShapeDtypeStruct((b, s, D_IN), jnp.bfloat16),
        compiler_params=pltpu.CompilerParams(
            dimension_semantics=("parallel", "parallel"), vmem_limit_bytes=VMEM_LIMIT),
        name="in_proj",
    )(x, g, w, cos, sa, sb)


def _conv_mix_kernel(cur_ref, prev_ref, next_ref, aw_ref, ab_ref, lg_ref, lb_ref, bw_ref,
                     o_ref, hbuf, gbuf):
    i = pl.program_id(1)
    n = pl.num_programs(1)
    t = T_MIX
    f32 = jnp.float32
    has_prev = (i > 0).astype(f32)
    has_next = (i < n - 1).astype(f32)

    def glu(ref, rows):
        u = ref[0, rows, COL_A_U:COL_A_U + D_GROUP].astype(f32)
        v = ref[0, rows, COL_A_V:COL_A_V + D_GROUP].astype(f32)
        return u * jax.nn.sigmoid(v)

    def cx(ref, rows):
        c = ref[0, rows, COL_B_C:COL_B_C + D_GROUP].astype(f32)
        xv = ref[0, rows, COL_B_X:COL_B_X + D_GROUP].astype(f32)
        return c * xv

    full = slice(None)
    hbuf[0:HALO] = glu(prev_ref, full) * has_prev
    hbuf[HALO:HALO + t] = glu(cur_ref, full)
    hbuf[HALO + t:t + 2 * HALO] = glu(next_ref, full) * has_next
    gbuf[0:HALO] = cx(prev_ref, full) * has_prev
    gbuf[HALO:HALO + t] = cx(cur_ref, full)
    gbuf[HALO + t:t + 2 * HALO] = cx(next_ref, full) * has_next

    rc = 64
    for r0 in range(0, t, rc):
        acc = jnp.broadcast_to(ab_ref[...], (rc, D_GROUP))
        for j in range(CONV_A_WIDTH):
            off = HALO - CONV_A_PAD + j + r0
            acc = acc + aw_ref[j:j + 1, :] * hbuf[off:off + rc, :]
        mu = jnp.mean(acc, axis=-1, keepdims=True)
        xc = acc - mu
        var = jnp.mean(xc * xc, axis=-1, keepdims=True)
        hn = xc * lax.rsqrt(var + EPS) * lg_ref[...] + lb_ref[...]
        za = cur_ref[0, r0:r0 + rc, COL_A_Z:COL_A_Z + D_GROUP].astype(f32)
        o_ref[0, r0:r0 + rc, 0:D_GROUP] = (_silu(hn) * _silu(za)).astype(o_ref.dtype)
        conv = (bw_ref[0:1, :] * gbuf[HALO - 1 + r0:HALO - 1 + r0 + rc, :]
                + bw_ref[1:2, :] * gbuf[HALO + r0:HALO + r0 + rc, :]
                + bw_ref[2:3, :] * gbuf[HALO + 1 + r0:HALO + 1 + r0 + rc, :])
        bb = cur_ref[0, r0:r0 + rc, COL_B_B:COL_B_B + D_GROUP].astype(f32)
        zb = cur_ref[0, r0:r0 + rc, COL_B_Z:COL_B_Z + D_GROUP].astype(f32)
        o_ref[0, r0:r0 + rc, D_GROUP:2 * D_GROUP] = (bb * conv * _silu(zb)).astype(o_ref.dtype)


def _conv_mix(proj, aw, ab, lg, lb, bw):
    b, s, _ = proj.shape
    t = T_MIX
    wab = COL_C_Q
    hb = t // HALO
    nh = s // HALO
    small = lambda shape: pl.BlockSpec(shape, lambda bi, i: (0, 0))
    return pl.pallas_call(
        _conv_mix_kernel,
        grid=(b, s // t),
        in_specs=[
            pl.BlockSpec((1, t, wab), lambda bi, i: (bi, i, 0)),
            pl.BlockSpec((1, HALO, wab), lambda bi, i: (bi, jnp.maximum(i * hb - 1, 0), 0)),
            pl.BlockSpec((1, HALO, wab), lambda bi, i: (bi, jnp.minimum((i + 1) * hb, nh - 1), 0)),
            small((CONV_A_WIDTH, D_GROUP)), small((1, D_GROUP)), small((1, D_GROUP)),
            small((1, D_GROUP)), small((CONV_B_WIDTH, D_GROUP)),
        ],
        out_specs=pl.BlockSpec((1, t, 2 * D_GROUP), lambda bi, i: (bi, i, 0)),
        out_shape=jax.ShapeDtypeStruct((b, s, 2 * D_GROUP), jnp.bfloat16),
        scratch_shapes=[pltpu.VMEM((t + 2 * HALO, D_GROUP), jnp.float32),
                        pltpu.VMEM((t + 2 * HALO, D_GROUP), jnp.float32)],
        compiler_params=pltpu.CompilerParams(
            dimension_semantics=("parallel", "parallel"), vmem_limit_bytes=VMEM_LIMIT),
        name="conv_mix",
    )(proj, proj, proj, aw, ab, lg, lb, bw)


def _swa_kernel(sink_ref, q_ref, z_ref, kv_ref, o_ref):
    i = pl.program_id(1)
    s_len = kv_ref.shape[1]
    nkeys = 3 * SWA_BLOCK
    f32 = jnp.float32
    lane = lax.broadcasted_iota(jnp.int32, (1, LANES), 1)
    lo = lane < HEAD_DIM
    zero = jnp.zeros((), jnp.bfloat16)
    sink = jnp.concatenate(
        [jnp.full((SWA_BLOCK, 1), sink_ref[h], f32) for h in range(4)], axis=0)
    for nb in range(T_MIX // SWA_BLOCK):
        q0 = i * T_MIX + nb * SWA_BLOCK
        ws = pl.multiple_of(jnp.clip(q0 - SWA_BLOCK, 0, s_len - nkeys), SWA_BLOCK)
        rows = slice(nb * SWA_BLOCK, (nb + 1) * SWA_BLOCK)
        g0 = q_ref[0, rows, 0:LANES]
        g1 = q_ref[0, rows, LANES:2 * LANES]
        lhs = jnp.concatenate([jnp.where(lo, g0, zero), jnp.where(lo, g1, zero),
                               jnp.where(lo, zero, g0), jnp.where(lo, zero, g1)], axis=0)
        kw = kv_ref[0, pl.ds(ws, nkeys), 0:LANES]
        vw = kv_ref[0, pl.ds(ws, nkeys), LANES:2 * LANES]
        sc = jnp.einsum("qd,kd->qk", lhs, kw, preferred_element_type=f32)
        qpos = q0 + lax.broadcasted_iota(jnp.int32, (SWA_BLOCK, nkeys), 0)
        kpos = ws + lax.broadcasted_iota(jnp.int32, (SWA_BLOCK, nkeys), 1)
        valid = jnp.abs(kpos - qpos) <= SWA_WINDOW
        valid = jnp.concatenate([valid] * 4, axis=0)
        sc = jnp.where(valid, sc, NEG_INF)
        m = jnp.maximum(jnp.max(sc, axis=-1, keepdims=True), sink)
        e = jnp.exp(sc - m)
        den = jnp.sum(e, axis=-1, keepdims=True) + jnp.exp(sink - m)
        pv = jnp.dot(e.astype(jnp.bfloat16), vw, preferred_element_type=f32)
        o = pv / den
        b_ = SWA_BLOCK
        og0 = jnp.where(lo, o[0:b_], o[2 * b_:3 * b_])
        og1 = jnp.where(lo, o[b_:2 * b_], o[3 * b_:4 * b_])
        z = z_ref[0, rows, :].astype(f32)
        y = jnp.concatenate([og0, og1], axis=-1) * _silu(z)
        o_ref[0, rows, :] = y.astype(o_ref.dtype)


def _swa(proj, sink):
    b, s, _ = proj.shape
    t = T_MIX
    return pl.pallas_call(
        _swa_kernel,
        grid_spec=pltpu.PrefetchScalarGridSpec(
            num_scalar_prefetch=1,
            grid=(b, s // t),
            in_specs=[
                pl.BlockSpec((1, t, D_GROUP), lambda bi, i, sk: (bi, i, COL_C_Q // D_GROUP)),
                pl.BlockSpec((1, t, D_GROUP), lambda bi, i, sk: (bi, i, COL_C_Z // D_GROUP)),
                pl.BlockSpec((1, s, D_GROUP), lambda bi, i, sk: (bi, 0, COL_C_K // D_GROUP)),
            ],
            out_specs=pl.BlockSpec((1, t, D_GROUP), lambda bi, i, sk: (bi, i, 0)),
        ),
        out_shape=jax.ShapeDtypeStruct((b, s, D_GROUP), jnp.bfloat16),
        compiler_params=pltpu.CompilerParams(
            dimension_semantics=("parallel", "parallel"), vmem_limit_bytes=VMEM_LIMIT),
        name="swa",
    )(sink, proj, proj, proj)


def _nbr_kernel(q_ref, z_ref, k_ref, v_ref, bias_ref, o_ref):
    i = pl.program_id(1)
    rows_total = k_ref.shape[1] // GRID_W
    rows_tile = T_MIX // GRID_W
    nkeys = NA_KH * GRID_W
    f32 = jnp.float32
    lane = lax.broadcasted_iota(jnp.int32, (1, LANES), 1)
    lo = lane < HEAD_DIM
    zero = jnp.zeros((), jnp.bfloat16)

    def row_body(rr, carry):
        r = i * rows_tile + rr
        r0 = jnp.clip(r - NA_KH // 2, 0, rows_total - NA_KH)
        d0 = r0 - r + (NA_KH - 1)
        ks = pl.multiple_of(r0 * GRID_W, GRID_W)
        qrow = pl.ds(pl.multiple_of(rr * GRID_W, GRID_W), GRID_W)
        outs = []
        for g in range(2):
            cols = slice(g * LANES, (g + 1) * LANES)
            qg = q_ref[0, qrow, cols]
            lhs = jnp.concatenate([jnp.where(lo, qg, zero), jnp.where(lo, zero, qg)], axis=0)
            kw = k_ref[0, pl.ds(ks, nkeys), cols]
            vw = v_ref[0, pl.ds(ks, nkeys), cols]
            sc = jnp.einsum("qd,kd->qk", lhs, kw, preferred_element_type=f32)
            bias = jnp.concatenate(
                [jnp.concatenate([bias_ref[2 * g + hh, d0 + 2 * m] for m in range(NA_KH // 2)],
                                 axis=-1) for hh in range(2)], axis=0)
            sc = sc + bias
            m_ = jnp.max(sc, axis=-1, keepdims=True)
            e = jnp.exp(sc - m_)
            den = jnp.sum(e, axis=-1, keepdims=True)
            pv = jnp.dot(e.astype(jnp.bfloat16), vw, preferred_element_type=f32)
            o = pv / den
            outs.append(jnp.where(lo, o[0:GRID_W], o[GRID_W:2 * GRID_W]))
        z = z_ref[0, qrow, :].astype(f32)
        y = jnp.concatenate(outs, axis=-1) * _silu(z)
        o_ref[0, qrow, :] = y.astype(o_ref.dtype)
        return carry

    lax.fori_loop(0, rows_tile, row_body, 0)


def _nbr(proj, bias):
    b, s, _ = proj.shape
    t = T_MIX
    return pl.pallas_call(
        _nbr_kernel,
        grid=(b, s // t),
        in_specs=[
            pl.BlockSpec((1, t, D_GROUP), lambda bi, i: (bi, i, COL_D_Q // D_GROUP)),
            pl.BlockSpec((1, t, D_GROUP), lambda bi, i: (bi, i, COL_D_Z // D_GROUP)),
            pl.BlockSpec((1, s, D_GROUP), lambda bi, i: (bi, 0, COL_D_K // D_GROUP)),
            pl.BlockSpec((1, s, D_GROUP), lambda bi, i: (bi, 0, COL_D_V // D_GROUP)),
            pl.BlockSpec(bias.shape, lambda bi, i: (0, 0, 0, 0)),
        ],
        out_specs=pl.BlockSpec((1, t, D_GROUP), lambda bi, i: (bi, i, 0)),
        out_shape=jax.ShapeDtypeStruct((b, s, D_GROUP), jnp.bfloat16),
        compiler_params=pltpu.CompilerParams(
            dimension_semantics=("parallel", "parallel"), vmem_limit_bytes=VMEM_LIMIT),
        name="nbr",
    )(proj, proj, proj, proj, bias)


def _out_proj_kernel(x_ref, yab_ref, yc_ref, yd_ref, w_ref, g_ref, o_ref, *, final_norm):
    f32 = jnp.float32
    acc = jnp.dot(yab_ref[0], w_ref[0:2 * D_GROUP, :], preferred_element_type=f32)
    acc = acc + jnp.dot(yc_ref[0], w_ref[2 * D_GROUP:3 * D_GROUP, :], preferred_element_type=f32)
    acc = acc + jnp.dot(yd_ref[0], w_ref[3 * D_GROUP:4 * D_GROUP, :], preferred_element_type=f32)
    x = x_ref[0] + acc
    if final_norm:
        ms = jnp.mean(x * x, axis=-1, keepdims=True)
        x = x * lax.rsqrt(ms + EPS) * g_ref[...]
    o_ref[0] = x


def _out_proj(x, yab, yc, yd, w, g, final_norm):
    b, s, d = x.shape
    tm = TM_PROJ
    tile = lambda width: pl.BlockSpec((1, tm, width), lambda bi, i: (bi, i, 0))
    return pl.pallas_call(
        functools.partial(_out_proj_kernel, final_norm=final_norm),
        grid=(b, s // tm),
        in_specs=[tile(d), tile(2 * D_GROUP), tile(D_GROUP), tile(D_GROUP),
                  pl.BlockSpec(w.shape, lambda bi, i: (0, 0)),
                  pl.BlockSpec((1, d), lambda bi, i: (0, 0))],
        out_specs=tile(d),
        out_shape=jax.ShapeDtypeStruct((b, s, d), jnp.float32),
        compiler_params=pltpu.CompilerParams(
            dimension_semantics=("parallel", "parallel"), vmem_limit_bytes=VMEM_LIMIT),
        name="out_proj",
    )(x, yab, yc, yd, w, g)


def _swap_middle_heads(a, axis):
    parts = jnp.split(a, 4, axis=axis)
    return jnp.concatenate([parts[0], parts[2], parts[1], parts[3]], axis=axis)


def _prep_w_in(w):
    q = _swap_middle_heads(w[:, COL_C_Q:COL_C_K], 1)
    z = _swap_middle_heads(w[:, COL_C_Z:COL_D_Q], 1)
    w = jnp.concatenate([w[:, :COL_C_Q], q, w[:, COL_C_K:COL_C_Z], z, w[:, COL_D_Q:]], axis=1)
    return w.astype(jnp.bfloat16)


def _prep_w_out(w):
    yc = _swap_middle_heads(w[2 * D_GROUP:3 * D_GROUP], 0)
    w = jnp.concatenate([w[:2 * D_GROUP], yc, w[3 * D_GROUP:]], axis=0)
    return w.astype(jnp.bfloat16)


def _rope_tables(s):
    inv_freq = ROPE_THETA ** (-jnp.arange(0, HEAD_DIM, 2, dtype=jnp.float32) / HEAD_DIM)
    ang = jnp.arange(s, dtype=jnp.float32)[:, None] * inv_freq[None, :]
    cos, sin = jnp.cos(ang), jnp.sin(ang)
    zeros = jnp.zeros_like(sin)
    reps = LANES // HEAD_DIM
    cos_t = jnp.tile(jnp.concatenate([cos, cos], axis=1), (1, reps))
    sa_t = jnp.tile(jnp.concatenate([-sin, zeros], axis=1), (1, reps))
    sb_t = jnp.tile(jnp.concatenate([zeros, sin], axis=1), (1, reps))
    return cos_t, sa_t, sb_t


def _nbr_bias_table(rpb):
    c = np.arange(GRID_W)
    c0 = np.clip(c - NA_KW // 2, 0, GRID_W - NA_KW)
    col_ok = (c[None, :] >= c0[:, None]) & (c[None, :] < c0[:, None] + NA_KW)
    dc = np.clip(c[None, :] - c[:, None], -(NA_KW - 1), NA_KW - 1) + (NA_KW - 1)
    full = jnp.take(rpb, jnp.asarray(dc), axis=-1)
    full = jnp.where(jnp.asarray(col_ok)[None, None], full, NEG_INF)
    pad = jnp.full((rpb.shape[0], 2) + full.shape[2:], NEG_INF, full.dtype)
    full = jnp.concatenate([full, pad], axis=1)
    return jnp.concatenate([full[:, :-1], full[:, 1:]], axis=-1)


def kernel(x, norm_g, w_in, w_out, conv_a_w, conv_a_b, ln_a_g, ln_a_b, conv_b_w, swa_sink,
           na_rpb, final_norm_g):
    depth = norm_g.shape[0]
    s = x.shape[1]
    cos_t, sa_t, sb_t = _rope_tables(s)
    fg = final_norm_g.reshape(1, -1)
    for l in range(depth):
        proj = _in_proj(x, norm_g[l].reshape(1, -1), _prep_w_in(w_in[l]), cos_t, sa_t, sb_t)
        yab = _conv_mix(proj, conv_a_w[l], conv_a_b[l].reshape(1, -1), ln_a_g[l].reshape(1, -1),
                        ln_a_b[l].reshape(1, -1), conv_b_w[l])
        yc = _swa(proj, swa_sink[l])
        yd = _nbr(proj, _nbr_bias_table(na_rpb[l]))
        x = _out_proj(x, yab, yc, yd, _prep_w_out(w_out[l]), fg, final_norm=(l == depth - 1))
    return x
```

```python
import functools

import numpy as np
import jax
import jax.numpy as jnp
from jax import lax
from jax.experimental import pallas as pl
from jax.experimental.pallas import tpu as pltpu

D_MODEL = 1024
D_GROUP = 256
HEAD_DIM = 64
HALF = HEAD_DIM // 2
GRID_W = 64
CONV_A_WIDTH = 31
CONV_A_PAD = (CONV_A_WIDTH - 1) // 2
CONV_B_WIDTH = 3
SWA_WINDOW = 128
SWA_BLOCK = 128
NA_KH = 8
NA_KW = 16
ROPE_THETA = 10000.0
EPS = 1e-6
NEG_INF = -1e30
D_IN = 3584

COL_A_U, COL_A_V, COL_A_Z = 0, 256, 512
COL_B_B, COL_B_C, COL_B_X, COL_B_Z = 768, 1024, 1280, 1536
COL_C_Q, COL_C_K, COL_C_V, COL_C_Z = 1792, 2048, 2176, 2304
COL_D_Q, COL_D_K, COL_D_V, COL_D_Z = 2560, 2816, 3072, 3328

LANES = 128
SUBLANES = 8
HALO = 16
VMEM_LIMIT = 56 * 1024 * 1024

TM_PROJ = 512
T_MIX = 512
PROJ_CHUNK = 512


def _silu(x):
    return x * jax.nn.sigmoid(x)


def _in_proj_kernel(x_ref, g_ref, w_ref, cos_ref, sa_ref, sb_ref, o_ref):
    x = x_ref[0]
    ms = jnp.mean(x * x, axis=-1, keepdims=True)
    h = (x * lax.rsqrt(ms + EPS) * g_ref[...]).astype(jnp.bfloat16)
    cos, sa, sb = cos_ref[...], sa_ref[...], sb_ref[...]

    def rope(t):
        return t * cos + pltpu.roll(t, LANES - HALF, 1) * sa + pltpu.roll(t, HALF, 1) * sb

    for c0 in range(0, D_IN, PROJ_CHUNK):
        acc = jnp.dot(h, w_ref[:, c0:c0 + PROJ_CHUNK], preferred_element_type=jnp.float32)
        for s0 in range(c0, c0 + PROJ_CHUNK, LANES):
            t = acc[:, s0 - c0:s0 - c0 + LANES]
            if COL_C_Q <= s0 < COL_C_K or COL_D_Q <= s0 < COL_D_K:
                scale = HEAD_DIM ** -0.5
            else:
                scale = None
            if COL_C_Q <= s0 < COL_C_V:
                t = rope(t)
            if scale is not None:
                t = t * scale
            o_ref[0, :, s0:s0 + LANES] = t.astype(o_ref.dtype)


def _in_proj(x, g, w, cos, sa, sb):
    b, s, d = x.shape
    return pl.pallas_call(
        _in_proj_kernel,
        grid=(b, s // TM_PROJ),
        in_specs=[
            pl.BlockSpec((1, TM_PROJ, d), lambda bi, i: (bi, i, 0)),
            pl.BlockSpec((1, d), lambda bi, i: (0, 0)),
            pl.BlockSpec((d, D_IN), lambda bi, i: (0, 0)),
            pl.BlockSpec((TM_PROJ, LANES), lambda bi, i: (i, 0)),
            pl.BlockSpec((TM_PROJ, LANES), lambda bi, i: (i, 0)),
            pl.BlockSpec((TM_PROJ, LANES), lambda bi, i: (i, 0)),
        ],
        out_specs=pl.BlockSpec((1, TM_PROJ, D_IN), lambda bi, i: (bi, i, 0)),
        out_shape=jax.ShapeDtypeStruct((b, s, D_IN), jnp.bfloat16),
        compiler_params=pltpu.CompilerParams(
            dimension_semantics=("parallel", "parallel"), vmem_limit_bytes=VMEM_LIMIT),
        name="in_proj",
    )(x, g, w, cos, sa, sb)


def _conv_mix_kernel(cur_ref, prev_ref, next_ref, aw_ref, ab_ref, lg_ref, lb_ref, bw_ref,
                     o_ref, hbuf, gbuf, hph, gph):
    i = pl.program_id(1)
    n = pl.num_programs(1)
    t = T_MIX
    f32 = jnp.float32
    has_prev = (i > 0).astype(f32)
    has_next = (i < n - 1).astype(f32)

    def glu(ref, rows):
        u = ref[0, rows, COL_A_U:COL_A_U + D_GROUP].astype(f32)
        v = ref[0, rows, COL_A_V:COL_A_V + D_GROUP].astype(f32)
        return u * jax.nn.sigmoid(v)

    def cx(ref, rows):
        c = ref[0, rows, COL_B_C:COL_B_C + D_GROUP].astype(f32)
        xv = ref[0, rows, COL_B_X:COL_B_X + D_GROUP].astype(f32)
        return c * xv

    full = slice(None)
    hbuf[0:HALO] = glu(prev_ref, full) * has_prev
    hbuf[HALO:HALO + t] = glu(cur_ref, full)
    hbuf[HALO + t:t + 2 * HALO] = glu(next_ref, full) * has_next
    gbuf[0:HALO] = cx(prev_ref, full) * has_prev
    gbuf[HALO:HALO + t] = cx(cur_ref, full)
    gbuf[HALO + t:t + 2 * HALO] = cx(next_ref, full) * has_next

    nph = t + 2 * HALO - SUBLANES
    for p in range(1, SUBLANES):
        hph[p - 1] = hbuf[p:p + nph]
    gph[0] = gbuf[1:1 + nph]
    gph[1] = gbuf[SUBLANES - 1:SUBLANES - 1 + nph]

    def tap(buf, phases, off, rows):
        p = off % SUBLANES
        if p == 0:
            return buf[off:off + rows, :]
        return phases[p - 1, off - p:off - p + rows, :]

    rc = 64
    for r0 in range(0, t, rc):
        acc = jnp.broadcast_to(ab_ref[...], (rc, D_GROUP))
        for j in range(CONV_A_WIDTH):
            off = HALO - CONV_A_PAD + j + r0
            acc = acc + aw_ref[j:j + 1, :] * tap(hbuf, hph, off, rc)
        mu = jnp.mean(acc, axis=-1, keepdims=True)
        xc = acc - mu
        var = jnp.mean(xc * xc, axis=-1, keepdims=True)
        hn = xc * lax.rsqrt(var + EPS) * lg_ref[...] + lb_ref[...]
        za = cur_ref[0, r0:r0 + rc, COL_A_Z:COL_A_Z + D_GROUP].astype(f32)
        o_ref[0, r0:r0 + rc, 0:D_GROUP] = (_silu(hn) * _silu(za)).astype(o_ref.dtype)
        gm = HALO + r0 - SUBLANES
        conv = (bw_ref[0:1, :] * gph[1, gm:gm + rc, :]
                + bw_ref[1:2, :] * gbuf[HALO + r0:HALO + r0 + rc, :]
                + bw_ref[2:3, :] * gph[0, HALO + r0:HALO + r0 + rc, :])
        bb = cur_ref[0, r0:r0 + rc, COL_B_B:COL_B_B + D_GROUP].astype(f32)
        zb = cur_ref[0, r0:r0 + rc, COL_B_Z:COL_B_Z + D_GROUP].astype(f32)
        o_ref[0, r0:r0 + rc, D_GROUP:2 * D_GROUP] = (bb * conv * _silu(zb)).astype(o_ref.dtype)


def _conv_mix(proj, aw, ab, lg, lb, bw):
    b, s, _ = proj.shape
    t = T_MIX
    wab = COL_C_Q
    hb = t // HALO
    nh = s // HALO
    small = lambda shape: pl.BlockSpec(shape, lambda bi, i: (0, 0))
    return pl.pallas_call(
        _conv_mix_kernel,
        grid=(b, s // t),
        in_specs=[
            pl.BlockSpec((1, t, wab), lambda bi, i: (bi, i, 0)),
            pl.BlockSpec((1, HALO, wab), lambda bi, i: (bi, jnp.maximum(i * hb - 1, 0), 0)),
            pl.BlockSpec((1, HALO, wab), lambda bi, i: (bi, jnp.minimum((i + 1) * hb, nh - 1), 0)),
            small((CONV_A_WIDTH, D_GROUP)), small((1, D_GROUP)), small((1, D_GROUP)),
            small((1, D_GROUP)), small((CONV_B_WIDTH, D_GROUP)),
        ],
        out_specs=pl.BlockSpec((1, t, 2 * D_GROUP), lambda bi, i: (bi, i, 0)),
        out_shape=jax.ShapeDtypeStruct((b, s, 2 * D_GROUP), jnp.bfloat16),
        scratch_shapes=[pltpu.VMEM((t + 2 * HALO, D_GROUP), jnp.float32),
                        pltpu.VMEM((t + 2 * HALO, D_GROUP), jnp.float32),
                        pltpu.VMEM((SUBLANES - 1, t + 2 * HALO - SUBLANES, D_GROUP), jnp.float32),
                        pltpu.VMEM((2, t + 2 * HALO - SUBLANES, D_GROUP), jnp.float32)],
        compiler_params=pltpu.CompilerParams(
            dimension_semantics=("parallel", "parallel"), vmem_limit_bytes=VMEM_LIMIT),
        name="conv_mix",
    )(proj, proj, proj, aw, ab, lg, lb, bw)


def _swa_kernel(sink_ref, q_ref, z_ref, kv_ref, o_ref):
    i = pl.program_id(1)
    s_len = kv_ref.shape[1]
    nkeys = 3 * SWA_BLOCK
    f32 = jnp.float32
    lane = lax.broadcasted_iota(jnp.int32, (1, LANES), 1)
    lo = lane < HEAD_DIM
    zero = jnp.zeros((), jnp.bfloat16)
    ones = jnp.ones((nkeys, LANES), jnp.bfloat16)
    b_ = SWA_BLOCK
    for nb in range(T_MIX // SWA_BLOCK):
        q0 = i * T_MIX + nb * SWA_BLOCK
        ws = pl.multiple_of(jnp.clip(q0 - SWA_BLOCK, 0, s_len - nkeys), SWA_BLOCK)
        rows = slice(nb * SWA_BLOCK, (nb + 1) * SWA_BLOCK)
        g0 = q_ref[0, rows, 0:LANES]
        g1 = q_ref[0, rows, LANES:2 * LANES]
        lhs = jnp.concatenate([jnp.where(lo, g0, zero), jnp.where(lo, g1, zero),
                               jnp.where(lo, zero, g0), jnp.where(lo, zero, g1)], axis=0)
        kw = kv_ref[0, pl.ds(ws, nkeys), 0:LANES]
        vw = jnp.concatenate([kv_ref[0, pl.ds(ws, nkeys), LANES:2 * LANES], ones], axis=1)
        sc = jnp.einsum("qd,kd->qk", lhs, kw, preferred_element_type=f32)
        qpos = q0 + lax.broadcasted_iota(jnp.int32, (SWA_BLOCK, nkeys), 0)
        kpos = ws + lax.broadcasted_iota(jnp.int32, (SWA_BLOCK, nkeys), 1)
        valid = jnp.abs(kpos - qpos) <= SWA_WINDOW
        o = []
        for h in range(4):
            sh = jnp.where(valid, sc[h * b_:(h + 1) * b_], NEG_INF)
            sink = sink_ref[h]
            m = jnp.maximum(jnp.max(sh, axis=-1, keepdims=True), sink)
            e = jnp.exp(sh - m)
            pv = jnp.dot(e.astype(jnp.bfloat16), vw, preferred_element_type=f32)
            den = pv[:, LANES:] + jnp.exp(sink - m)
            o.append(pv[:, :LANES] / den)
        og0 = jnp.where(lo, o[0], o[2])
        og1 = jnp.where(lo, o[1], o[3])
        z = z_ref[0, rows, :].astype(f32)
        y = jnp.concatenate([og0, og1], axis=-1) * _silu(z)
        o_ref[0, rows, :] = y.astype(o_ref.dtype)


def _swa(proj, sink):
    b, s, _ = proj.shape
    t = T_MIX
    return pl.pallas_call(
        _swa_kernel,
        grid_spec=pltpu.PrefetchScalarGridSpec(
            num_scalar_prefetch=1,
            grid=(b, s // t),
            in_specs=[
                pl.BlockSpec((1, t, D_GROUP), lambda bi, i, sk: (bi, i, COL_C_Q // D_GROUP)),
                pl.BlockSpec((1, t, D_GROUP), lambda bi, i, sk: (bi, i, COL_C_Z // D_GROUP)),
                pl.BlockSpec((1, s, D_GROUP), lambda bi, i, sk: (bi, 0, COL_C_K // D_GROUP)),
            ],
            out_specs=pl.BlockSpec((1, t, D_GROUP), lambda bi, i, sk: (bi, i, 0)),
        ),
        out_shape=jax.ShapeDtypeStruct((b, s, D_GROUP), jnp.bfloat16),
        compiler_params=pltpu.CompilerParams(
            dimension_semantics=("parallel", "parallel"), vmem_limit_bytes=VMEM_LIMIT),
        name="swa",
    )(sink, proj, proj, proj)


def _nbr_kernel(q_ref, z_ref, k_ref, v_ref, bias_ref, o_ref):
    i = pl.program_id(1)
    rows_total = k_ref.shape[1] // GRID_W
    rows_tile = T_MIX // GRID_W
    nkeys = NA_KH * GRID_W
    f32 = jnp.float32
    lane = lax.broadcasted_iota(jnp.int32, (1, LANES), 1)
    lo = lane < HEAD_DIM
    zero = jnp.zeros((), jnp.bfloat16)
    ones = jnp.ones((nkeys, LANES), jnp.bfloat16)

    for rr in range(rows_tile):
        r = i * rows_tile + rr
        r0 = jnp.clip(r - NA_KH // 2, 0, rows_total - NA_KH)
        d0 = r0 - r + (NA_KH - 1)
        ks = pl.multiple_of(r0 * GRID_W, GRID_W)
        qrow = slice(rr * GRID_W, (rr + 1) * GRID_W)
        outs = []
        for g in range(2):
            cols = slice(g * LANES, (g + 1) * LANES)
            qg = q_ref[0, qrow, cols]
            lhs = jnp.concatenate([jnp.where(lo, qg, zero), jnp.where(lo, zero, qg)], axis=0)
            kw = k_ref[0, pl.ds(ks, nkeys), cols]
            vw = jnp.concatenate([v_ref[0, pl.ds(ks, nkeys), cols], ones], axis=1)
            sc = jnp.einsum("qd,kd->qk", lhs, kw, preferred_element_type=f32)
            bias = jnp.concatenate(
                [jnp.concatenate([bias_ref[2 * g + hh, d0 + 2 * m] for m in range(NA_KH // 2)],
                                 axis=-1) for hh in range(2)], axis=0)
            sc = sc + bias
            m_ = jnp.max(sc, axis=-1, keepdims=True)
            e = jnp.exp(sc - m_)
            pv = jnp.dot(e.astype(jnp.bfloat16), vw, preferred_element_type=f32)
            o = pv[:, :LANES] / pv[:, LANES:]
            outs.append(jnp.where(lo, o[0:GRID_W], o[GRID_W:2 * GRID_W]))
        z = z_ref[0, qrow, :].astype(f32)
        y = jnp.concatenate(outs, axis=-1) * _silu(z)
        o_ref[0, qrow, :] = y.astype(o_ref.dtype)


def _nbr(proj, bias):
    b, s, _ = proj.shape
    t = T_MIX
    return pl.pallas_call(
        _nbr_kernel,
        grid=(b, s // t),
        in_specs=[
            pl.BlockSpec((1, t, D_GROUP), lambda bi, i: (bi, i, COL_D_Q // D_GROUP)),
            pl.BlockSpec((1, t, D_GROUP), lambda bi, i: (bi, i, COL_D_Z // D_GROUP)),
            pl.BlockSpec((1, s, D_GROUP), lambda bi, i: (bi, 0, COL_D_K // D_GROUP)),
            pl.BlockSpec((1, s, D_GROUP), lambda bi, i: (bi, 0, COL_D_V // D_GROUP)),
            pl.BlockSpec(bias.shape, lambda bi, i: (0, 0, 0, 0)),
        ],
        out_specs=pl.BlockSpec((1, t, D_GROUP), lambda bi, i: (bi, i, 0)),
        out_shape=jax.ShapeDtypeStruct((b, s, D_GROUP), jnp.bfloat16),
        compiler_params=pltpu.CompilerParams(
            dimension_semantics=("parallel", "parallel"), vmem_limit_bytes=VMEM_LIMIT),
        name="nbr",
    )(proj, proj, proj, proj, bias)


def _out_proj_kernel(x_ref, yab_ref, yc_ref, yd_ref, w_ref, g_ref, o_ref, *, final_norm):
    f32 = jnp.float32
    acc = jnp.dot(yab_ref[0], w_ref[0:2 * D_GROUP, :], preferred_element_type=f32)
    acc = acc + jnp.dot(yc_ref[0], w_ref[2 * D_GROUP:3 * D_GROUP, :], preferred_element_type=f32)
    acc = acc + jnp.dot(yd_ref[0], w_ref[3 * D_GROUP:4 * D_GROUP, :], preferred_element_type=f32)
    x = x_ref[0] + acc
    if final_norm:
        ms = jnp.mean(x * x, axis=-1, keepdims=True)
        x = x * lax.rsqrt(ms + EPS) * g_ref[...]
    o_ref[0] = x


def _out_proj(x, yab, yc, yd, w, g, final_norm):
    b, s, d = x.shape
    tm = TM_PROJ
    tile = lambda width: pl.BlockSpec((1, tm, width), lambda bi, i: (bi, i, 0))
    return pl.pallas_call(
        functools.partial(_out_proj_kernel, final_norm=final_norm),
        grid=(b, s // tm),
        in_specs=[tile(d), tile(2 * D_GROUP), tile(D_GROUP), tile(D_GROUP),
                  pl.BlockSpec(w.shape, lambda bi, i: (0, 0)),
                  pl.BlockSpec((1, d), lambda bi, i: (0, 0))],
        out_specs=tile(d),
        out_shape=jax.ShapeDtypeStruct((b, s, d), jnp.float32),
        compiler_params=pltpu.CompilerParams(
            dimension_semantics=("parallel", "parallel"), vmem_limit_bytes=VMEM_LIMIT),
        name="out_proj",
    )(x, yab, yc, yd, w, g)


def _swap_middle_heads(a, axis):
    parts = jnp.split(a, 4, axis=axis)
    return jnp.concatenate([parts[0], parts[2], parts[1], parts[3]], axis=axis)


def _prep_w_in(w):
    q = _swap_middle_heads(w[:, COL_C_Q:COL_C_K], 1)
    z = _swap_middle_heads(w[:, COL_C_Z:COL_D_Q], 1)
    w = jnp.concatenate([w[:, :COL_C_Q], q, w[:, COL_C_K:COL_C_Z], z, w[:, COL_D_Q:]], axis=1)
    return w.astype(jnp.bfloat16)


def _prep_w_out(w):
    yc = _swap_middle_heads(w[2 * D_GROUP:3 * D_GROUP], 0)
    w = jnp.concatenate([w[:2 * D_GROUP], yc, w[3 * D_GROUP:]], axis=0)
    return w.astype(jnp.bfloat16)


def _rope_tables(s):
    inv_freq = ROPE_THETA ** (-jnp.arange(0, HEAD_DIM, 2, dtype=jnp.float32) / HEAD_DIM)
    ang = jnp.arange(s, dtype=jnp.float32)[:, None] * inv_freq[None, :]
    cos, sin = jnp.cos(ang), jnp.sin(ang)
    zeros = jnp.zeros_like(sin)
    reps = LANES // HEAD_DIM
    cos_t = jnp.tile(jnp.concatenate([cos, cos], axis=1), (1, reps))
    sa_t = jnp.tile(jnp.concatenate([-sin, zeros], axis=1), (1, reps))
    sb_t = jnp.tile(jnp.concatenate([zeros, sin], axis=1), (1, reps))
    return cos_t, sa_t, sb_t


def _nbr_bias_table(rpb):
    c = np.arange(GRID_W)
    c0 = np.clip(c - NA_KW // 2, 0, GRID_W - NA_KW)
    col_ok = (c[None, :] >= c0[:, None]) & (c[None, :] < c0[:, None] + NA_KW)
    dc = np.clip(c[None, :] - c[:, None], -(NA_KW - 1), NA_KW - 1) + (NA_KW - 1)
    full = jnp.take(rpb, jnp.asarray(dc), axis=-1)
    full = jnp.where(jnp.asarray(col_ok)[None, None], full, NEG_INF)
    pad = jnp.full((rpb.shape[0], 2) + full.shape[2:], NEG_INF, full.dtype)
    full = jnp.concatenate([full, pad], axis=1)
    return jnp.concatenate([full[:, :-1], full[:, 1:]], axis=-1)


def kernel(x, norm_g, w_in, w_out, conv_a_w, conv_a_b, ln_a_g, ln_a_b, conv_b_w, swa_sink,
           na_rpb, final_norm_g):
    depth = norm_g.shape[0]
    s = x.shape[1]
    cos_t, sa_t, sb_t = _rope_tables(s)
    fg = final_norm_g.reshape(1, -1)
    for l in range(depth):
        proj = _in_proj(x, norm_g[l].reshape(1, -1), _prep_w_in(w_in[l]), cos_t, sa_t, sb_t)
        yab = _conv_mix(proj, conv_a_w[l], conv_a_b[l].reshape(1, -1), ln_a_g[l].reshape(1, -1),
                        ln_a_b[l].reshape(1, -1), conv_b_w[l])
        yc = _swa(proj, swa_sink[l])
        yd = _nbr(proj, _nbr_bias_table(na_rpb[l]))
        x = _out_proj(x, yab, yc, yd, _prep_w_out(w_out[l]), fg, final_norm=(l == depth - 1))
    return x
```

```python
import functools
import math

import numpy as np
import jax
import jax.numpy as jnp
from jax import lax
from jax.experimental import pallas as pl
from jax.experimental.pallas import tpu as pltpu

D_MODEL = 1024
D_GROUP = 256
HEAD_DIM = 64
HALF = HEAD_DIM // 2
GRID_W = 64
CONV_A_WIDTH = 31
CONV_A_PAD = (CONV_A_WIDTH - 1) // 2
CONV_B_WIDTH = 3
SWA_WINDOW = 128
SWA_BLOCK = 128
SWA_HEADS = 4
NA_KH = 8
NA_KW = 16
ROPE_THETA = 10000.0
EPS = 1e-6
NEG_INF = -1e30
LOG2E = math.log2(math.e)
Q_SCALE = HEAD_DIM ** -0.5 * LOG2E

COL_A_U, COL_A_V, COL_A_Z = 0, 256, 512
COL_B_B, COL_B_C, COL_B_X, COL_B_Z = 768, 1024, 1280, 1536
COL_C_Q, COL_C_K, COL_C_V, COL_C_Z = 1792, 2048, 2176, 2304
COL_D_Q, COL_D_K, COL_D_V, COL_D_Z = 2560, 2816, 3072, 3328
D_IN = 3584
COL_C_KS, COL_C_VS = 3584, 3712
D_PROJ = 3840

LANES = 128
SUBLANES = 8
HALO = 16
VMEM_LIMIT = 56 * 1024 * 1024

TM_PROJ = 512
T_MIX = 512
PROJ_CHUNK = 512


def _silu(x):
    return x * jax.nn.sigmoid(x)


def _params():
    return pltpu.CompilerParams(
        dimension_semantics=("parallel", "parallel"), vmem_limit_bytes=VMEM_LIMIT)


def _in_proj_kernel(x_ref, g_ref, w_ref, cos_ref, sin_ref, o_ref, *, layer):
    x = x_ref[0]
    ms = jnp.mean(x * x, axis=-1, keepdims=True)
    h = (x * lax.rsqrt(ms + EPS) * g_ref[layer:layer + 1, :]).astype(jnp.bfloat16)
    cos, sin = cos_ref[...], sin_ref[...]
    lane = lax.broadcasted_iota(jnp.int32, (1, LANES), 1)
    first_half = (lane % HEAD_DIM) < HALF

    def rope(t):
        partner = jnp.where(first_half, -pltpu.roll(t, LANES - HALF, 1), pltpu.roll(t, HALF, 1))
        return t * cos + partner * sin

    def store(col, t):
        o_ref[0, :, col:col + LANES] = t.astype(o_ref.dtype)

    for c0 in range(0, D_IN, PROJ_CHUNK):
        acc = jnp.dot(h, w_ref[0, :, c0:c0 + PROJ_CHUNK], preferred_element_type=jnp.float32)
        for s0 in range(c0, c0 + PROJ_CHUNK, LANES):
            t = acc[:, s0 - c0:s0 - c0 + LANES]
            if COL_C_Q <= s0 < COL_C_V:
                t = rope(t)
            if COL_C_Q <= s0 < COL_C_K or COL_D_Q <= s0 < COL_D_K:
                t = t * Q_SCALE
            store(s0, t)
            if s0 == COL_C_K:
                store(COL_C_KS, pltpu.roll(t, HEAD_DIM, 1))
            if s0 == COL_C_V:
                store(COL_C_VS, pltpu.roll(t, HEAD_DIM, 1))


def _in_proj(x, g, w, cos, sin, layer):
    b, s, d = x.shape
    return pl.pallas_call(
        functools.partial(_in_proj_kernel, layer=layer),
        grid=(b, s // TM_PROJ),
        in_specs=[
            pl.BlockSpec((1, TM_PROJ, d), lambda bi, i: (bi, i, 0)),
            pl.BlockSpec(g.shape, lambda bi, i: (0, 0)),
            pl.BlockSpec((1, d, D_IN), lambda bi, i: (layer, 0, 0)),
            pl.BlockSpec((TM_PROJ, LANES), lambda bi, i: (i, 0)),
            pl.BlockSpec((TM_PROJ, LANES), lambda bi, i: (i, 0)),
        ],
        out_specs=pl.BlockSpec((1, TM_PROJ, D_PROJ), lambda bi, i: (bi, i, 0)),
        out_shape=jax.ShapeDtypeStruct((b, s, D_PROJ), jnp.bfloat16),
        compiler_params=_params(),
        name="in_proj",
    )(x, g, w, cos, sin)


def _conv_mix_kernel(cur_ref, prev_ref, next_ref, aw_ref, ab_ref, lg_ref, lb_ref, bw_ref,
                     o_ref, hbuf, gbuf, hph, gph, *, layer):
    i = pl.program_id(1)
    n = pl.num_programs(1)
    t = T_MIX
    f32 = jnp.float32
    has_prev = (i > 0).astype(f32)
    has_next = (i < n - 1).astype(f32)
    row = lambda ref, j: ref[layer, j:j + 1, :]

    def glu(ref, rows):
        u = ref[0, rows, COL_A_U:COL_A_U + D_GROUP].astype(f32)
        v = ref[0, rows, COL_A_V:COL_A_V + D_GROUP].astype(f32)
        return u * jax.nn.sigmoid(v)

    def cx(ref, rows):
        c = ref[0, rows, COL_B_C:COL_B_C + D_GROUP].astype(f32)
        xv = ref[0, rows, COL_B_X:COL_B_X + D_GROUP].astype(f32)
        return c * xv

    full = slice(None)
    hbuf[0:HALO] = glu(prev_ref, full) * has_prev
    hbuf[HALO:HALO + t] = glu(cur_ref, full)
    hbuf[HALO + t:t + 2 * HALO] = glu(next_ref, full) * has_next
    gbuf[0:HALO] = cx(prev_ref, full) * has_prev
    gbuf[HALO:HALO + t] = cx(cur_ref, full)
    gbuf[HALO + t:t + 2 * HALO] = cx(next_ref, full) * has_next

    nph = t + 2 * HALO - SUBLANES
    for p in range(1, SUBLANES):
        hph[p - 1] = hbuf[p:p + nph]
    gph[0] = gbuf[1:1 + nph]
    gph[1] = gbuf[SUBLANES - 1:SUBLANES - 1 + nph]

    def tap(off, rows):
        p = off % SUBLANES
        if p == 0:
            return hbuf[off:off + rows, :]
        return hph[p - 1, off - p:off - p + rows, :]

    rc = 64
    for r0 in range(0, t, rc):
        acc = jnp.broadcast_to(row(ab_ref, 0), (rc, D_GROUP))
        for j in range(CONV_A_WIDTH):
            acc = acc + aw_ref[layer, j:j + 1, :] * tap(HALO - CONV_A_PAD + j + r0, rc)
        mu = jnp.mean(acc, axis=-1, keepdims=True)
        xc = acc - mu
        var = jnp.mean(xc * xc, axis=-1, keepdims=True)
        hn = xc * lax.rsqrt(var + EPS) * row(lg_ref, 0) + row(lb_ref, 0)
        za = cur_ref[0, r0:r0 + rc, COL_A_Z:COL_A_Z + D_GROUP].astype(f32)
        o_ref[0, r0:r0 + rc, 0:D_GROUP] = (_silu(hn) * _silu(za)).astype(o_ref.dtype)
        gm = HALO + r0 - SUBLANES
        conv = (row(bw_ref, 0) * gph[1, gm:gm + rc, :]
                + row(bw_ref, 1) * gbuf[HALO + r0:HALO + r0 + rc, :]
                + row(bw_ref, 2) * gph[0, HALO + r0:HALO + r0 + rc, :])
        bb = cur_ref[0, r0:r0 + rc, COL_B_B:COL_B_B + D_GROUP].astype(f32)
        zb = cur_ref[0, r0:r0 + rc, COL_B_Z:COL_B_Z + D_GROUP].astype(f32)
        o_ref[0, r0:r0 + rc, D_GROUP:2 * D_GROUP] = (bb * conv * _silu(zb)).astype(o_ref.dtype)


def _conv_mix(proj, aw, ab, lg, lb, bw, layer):
    b, s, _ = proj.shape
    t = T_MIX
    wab = COL_C_Q
    hb = t // HALO
    nh = s // HALO
    whole = lambda a: pl.BlockSpec(a.shape, lambda bi, i: (0,) * a.ndim)
    return pl.pallas_call(
        functools.partial(_conv_mix_kernel, layer=layer),
        grid=(b, s // t),
        in_specs=[
            pl.BlockSpec((1, t, wab), lambda bi, i: (bi, i, 0)),
            pl.BlockSpec((1, HALO, wab), lambda bi, i: (bi, jnp.maximum(i * hb - 1, 0), 0)),
            pl.BlockSpec((1, HALO, wab), lambda bi, i: (bi, jnp.minimum((i + 1) * hb, nh - 1), 0)),
            whole(aw), whole(ab), whole(lg), whole(lb), whole(bw),
        ],
        out_specs=pl.BlockSpec((1, t, 2 * D_GROUP), lambda bi, i: (bi, i, 0)),
        out_shape=jax.ShapeDtypeStruct((b, s, 2 * D_GROUP), jnp.bfloat16),
        scratch_shapes=[pltpu.VMEM((t + 2 * HALO, D_GROUP), jnp.float32),
                        pltpu.VMEM((t + 2 * HALO, D_GROUP), jnp.float32),
                        pltpu.VMEM((SUBLANES - 1, t + 2 * HALO - SUBLANES, D_GROUP), jnp.float32),
                        pltpu.VMEM((2, t + 2 * HALO - SUBLANES, D_GROUP), jnp.float32)],
        compiler_params=_params(),
        name="conv_mix",
    )(proj, proj, proj, aw, ab, lg, lb, bw)


def _swa_kernel(sink_ref, q_ref, z_ref, kv_ref, kvs_ref, o_ref, *, layer):
    i = pl.program_id(1)
    s_len = kv_ref.shape[1]
    nkeys = 3 * SWA_BLOCK
    b_ = SWA_BLOCK
    f32 = jnp.float32
    lane = lax.broadcasted_iota(jnp.int32, (1, LANES), 1)
    lo = lane < HEAD_DIM
    zero = jnp.zeros((), jnp.bfloat16)
    ones = jnp.ones((nkeys, LANES), jnp.bfloat16)
    for nb in range(T_MIX // SWA_BLOCK):
        q0 = i * T_MIX + nb * SWA_BLOCK
        ws = pl.multiple_of(jnp.clip(q0 - SWA_BLOCK, 0, s_len - nkeys), SWA_BLOCK)
        rows = slice(nb * SWA_BLOCK, (nb + 1) * SWA_BLOCK)
        win = pl.ds(ws, nkeys)
        g0 = q_ref[0, rows, 0:LANES]
        g1 = q_ref[0, rows, LANES:2 * LANES]
        qpos = q0 + lax.broadcasted_iota(jnp.int32, (SWA_BLOCK, nkeys), 0)
        kpos = ws + lax.broadcasted_iota(jnp.int32, (SWA_BLOCK, nkeys), 1)
        valid = jnp.abs(kpos - qpos) <= SWA_WINDOW
        o = [None] * SWA_HEADS
        for ref, heads, lhs in (
                (kv_ref, (0, 3), [jnp.where(lo, g0, zero), jnp.where(lo, zero, g1)]),
                (kvs_ref, (1, 2), [jnp.where(lo, zero, g0), jnp.where(lo, g1, zero)])):
            kw = ref[0, win, 0:LANES]
            vw = jnp.concatenate([ref[0, win, LANES:2 * LANES], ones], axis=1)
            sc = jnp.einsum("qd,kd->qk", jnp.concatenate(lhs, axis=0), kw,
                            preferred_element_type=f32)
            for j, h in enumerate(heads):
                sh = jnp.where(valid, sc[j * b_:(j + 1) * b_], NEG_INF)
                sink = sink_ref[layer, h]
                m = jnp.maximum(jnp.max(sh, axis=-1, keepdims=True), sink)
                e = jnp.exp2(sh - m)
                pv = jnp.dot(e.astype(jnp.bfloat16), vw, preferred_element_type=f32)
                den = pv[:, LANES:] + jnp.exp2(sink - m)
                o[h] = pv[:, :LANES] / den
        og0 = jnp.where(lo, o[0], o[1])
        og1 = jnp.where(lo, o[2], o[3])
        z = z_ref[0, rows, :].astype(f32)
        y = jnp.concatenate([og0, og1], axis=-1) * _silu(z)
        o_ref[0, rows, :] = y.astype(o_ref.dtype)


def _swa(proj, sink, layer):
    b, s, _ = proj.shape
    t = T_MIX
    return pl.pallas_call(
        functools.partial(_swa_kernel, layer=layer),
        grid_spec=pltpu.PrefetchScalarGridSpec(
            num_scalar_prefetch=1,
            grid=(b, s // t),
            in_specs=[
                pl.BlockSpec((1, t, D_GROUP), lambda bi, i, sk: (bi, i, COL_C_Q // D_GROUP)),
                pl.BlockSpec((1, t, D_GROUP), lambda bi, i, sk: (bi, i, COL_C_Z // D_GROUP)),
                pl.BlockSpec((1, s, D_GROUP), lambda bi, i, sk: (bi, 0, COL_C_K // D_GROUP)),
                pl.BlockSpec((1, s, D_GROUP), lambda bi, i, sk: (bi, 0, COL_C_KS // D_GROUP)),
            ],
            out_specs=pl.BlockSpec((1, t, D_GROUP), lambda bi, i, sk: (bi, i, 0)),
        ),
        out_shape=jax.ShapeDtypeStruct((b, s, D_GROUP), jnp.bfloat16),
        compiler_params=_params(),
        name="swa",
    )(sink, proj, proj, proj, proj)


def _nbr_kernel(q_ref, z_ref, k_ref, v_ref, bias_ref, o_ref, *, layer):
    i = pl.program_id(1)
    rows_total = k_ref.shape[1] // GRID_W
    rows_tile = T_MIX // GRID_W
    nkeys = NA_KH * GRID_W
    f32 = jnp.float32
    lane = lax.broadcasted_iota(jnp.int32, (1, LANES), 1)
    lo = lane < HEAD_DIM
    zero = jnp.zeros((), jnp.bfloat16)
    ones = jnp.ones((nkeys, LANES), jnp.bfloat16)

    for rr in range(rows_tile):
        r = i * rows_tile + rr
        r0 = jnp.clip(r - NA_KH // 2, 0, rows_total - NA_KH)
        d0 = r0 - r + (NA_KH - 1)
        ks = pl.multiple_of(r0 * GRID_W, GRID_W)
        qrow = slice(rr * GRID_W, (rr + 1) * GRID_W)
        outs = []
        for g in range(2):
            cols = slice(g * LANES, (g + 1) * LANES)
            qg = q_ref[0, qrow, cols]
            lhs = jnp.concatenate([jnp.where(lo, qg, zero), jnp.where(lo, zero, qg)], axis=0)
            kw = k_ref[0, pl.ds(ks, nkeys), cols]
            vw = jnp.concatenate([v_ref[0, pl.ds(ks, nkeys), cols], ones], axis=1)
            sc = jnp.einsum("qd,kd->qk", lhs, kw, preferred_element_type=f32)
            bias = jnp.concatenate(
                [jnp.concatenate([bias_ref[layer, 2 * g + hh, d0 + 2 * m]
                                  for m in range(NA_KH // 2)], axis=-1)
                 for hh in range(2)], axis=0)
            sc = sc + bias
            m_ = jnp.max(sc, axis=-1, keepdims=True)
            e = jnp.exp2(sc - m_)
            pv = jnp.dot(e.astype(jnp.bfloat16), vw, preferred_element_type=f32)
            o = pv[:, :LANES] / pv[:, LANES:]
            outs.append(jnp.where(lo, o[0:GRID_W], o[GRID_W:2 * GRID_W]))
        z = z_ref[0, qrow, :].astype(f32)
        y = jnp.concatenate(outs, axis=-1) * _silu(z)
        o_ref[0, qrow, :] = y.astype(o_ref.dtype)


def _nbr(proj, bias, layer):
    b, s, _ = proj.shape
    t = T_MIX
    return pl.pallas_call(
        functools.partial(_nbr_kernel, layer=layer),
        grid=(b, s // t),
        in_specs=[
            pl.BlockSpec((1, t, D_GROUP), lambda bi, i: (bi, i, COL_D_Q // D_GROUP)),
            pl.BlockSpec((1, t, D_GROUP), lambda bi, i: (bi, i, COL_D_Z // D_GROUP)),
            pl.BlockSpec((1, s, D_GROUP), lambda bi, i: (bi, 0, COL_D_K // D_GROUP)),
            pl.BlockSpec((1, s, D_GROUP), lambda bi, i: (bi, 0, COL_D_V // D_GROUP)),
            pl.BlockSpec(bias.shape, lambda bi, i: (0,) * bias.ndim),
        ],
        out_specs=pl.BlockSpec((1, t, D_GROUP), lambda bi, i: (bi, i, 0)),
        out_shape=jax.ShapeDtypeStruct((b, s, D_GROUP), jnp.bfloat16),
        compiler_params=_params(),
        name="nbr",
    )(proj, proj, proj, proj, bias)


def _out_proj_kernel(x_ref, yab_ref, yc_ref, yd_ref, w_ref, g_ref, o_ref, *, final_norm):
    f32 = jnp.float32
    acc = jnp.dot(yab_ref[0], w_ref[0, 0:2 * D_GROUP, :], preferred_element_type=f32)
    acc = acc + jnp.dot(yc_ref[0], w_ref[0, 2 * D_GROUP:3 * D_GROUP, :], preferred_element_type=f32)
    acc = acc + jnp.dot(yd_ref[0], w_ref[0, 3 * D_GROUP:4 * D_GROUP, :], preferred_element_type=f32)
    x = x_ref[0] + acc
    if final_norm:
        ms = jnp.mean(x * x, axis=-1, keepdims=True)
        x = x * lax.rsqrt(ms + EPS) * g_ref[...]
    o_ref[0] = x


def _out_proj(x, yab, yc, yd, w, g, layer, final_norm):
    b, s, d = x.shape
    tm = TM_PROJ
    tile = lambda width: pl.BlockSpec((1, tm, width), lambda bi, i: (bi, i, 0))
    return pl.pallas_call(
        functools.partial(_out_proj_kernel, final_norm=final_norm),
        grid=(b, s // tm),
        in_specs=[tile(d), tile(2 * D_GROUP), tile(D_GROUP), tile(D_GROUP),
                  pl.BlockSpec((1,) + w.shape[1:], lambda bi, i: (layer, 0, 0)),
                  pl.BlockSpec((1, d), lambda bi, i: (0, 0))],
        out_specs=tile(d),
        out_shape=jax.ShapeDtypeStruct((b, s, d), jnp.float32),
        compiler_params=_params(),
        name="out_proj",
    )(x, yab, yc, yd, w, g)


def _rope_tables(s):
    inv_freq = ROPE_THETA ** (-jnp.arange(0, HEAD_DIM, 2, dtype=jnp.float32) / HEAD_DIM)
    inv_freq = jnp.tile(inv_freq, LANES // HALF)
    ang = jnp.arange(s, dtype=jnp.float32)[:, None] * inv_freq[None, :]
    return jnp.cos(ang), jnp.sin(ang)


def _nbr_bias_table(rpb):
    c = np.arange(GRID_W)
    c0 = np.clip(c - NA_KW // 2, 0, GRID_W - NA_KW)
    col_ok = (c[None, :] >= c0[:, None]) & (c[None, :] < c0[:, None] + NA_KW)
    dc = np.clip(c[None, :] - c[:, None], -(NA_KW - 1), NA_KW - 1) + (NA_KW - 1)
    onehot = (dc[None] == np.arange(2 * NA_KW - 1)[:, None, None]).astype(np.float32)
    full = jnp.einsum("lhdc,cqk->lhdqk", rpb, jnp.asarray(onehot),
                      precision=lax.Precision.HIGHEST)
    full = jnp.where(jnp.asarray(col_ok), full * LOG2E, NEG_INF)
    pad = jnp.full(full.shape[:2] + (2,) + full.shape[3:], NEG_INF, full.dtype)
    full = jnp.concatenate([full, pad], axis=2)
    return jnp.concatenate([full[:, :, :-1], full[:, :, 1:]], axis=-1)


def kernel(x, norm_g, w_in, w_out, conv_a_w, conv_a_b, ln_a_g, ln_a_b, conv_b_w, swa_sink,
           na_rpb, final_norm_g):
    depth = norm_g.shape[0]
    s = x.shape[1]
    cos_t, sin_t = _rope_tables(s)
    w_in_b = w_in.astype(jnp.bfloat16)
    w_out_b = w_out.astype(jnp.bfloat16)
    bias = _nbr_bias_table(na_rpb)
    sink = swa_sink * LOG2E
    fg = final_norm_g.reshape(1, -1)
    vec = lambda a: a.reshape(depth, 1, -1)
    for l in range(depth):
        proj = _in_proj(x, norm_g, w_in_b, cos_t, sin_t, l)
        yab = _conv_mix(proj, conv_a_w, vec(conv_a_b), vec(ln_a_g), vec(ln_a_b), conv_b_w, l)
        yc = _swa(proj, sink, l)
        yd = _nbr(proj, bias, l)
        x = _out_proj(x, yab, yc, yd, w_out_b, fg, l, final_norm=(l == depth - 1))
    return x
```

```python
import functools
import math

import numpy as np
import jax
import jax.numpy as jnp
from jax import lax
from jax.experimental import pallas as pl
from jax.experimental.pallas import tpu as pltpu

D_MODEL = 1024
D_GROUP = 256
HEAD_DIM = 64
HALF = HEAD_DIM // 2
GRID_W = 64
CONV_A_WIDTH = 31
CONV_A_PAD = (CONV_A_WIDTH - 1) // 2
CONV_B_WIDTH = 3
SWA_WINDOW = 128
SWA_BLOCK = 128
SWA_HEADS = 4
NA_KH = 8
NA_KW = 16
ROPE_THETA = 10000.0
EPS = 1e-6
NEG_INF = -1e30
LOG2E = math.log2(math.e)
Q_SCALE = HEAD_DIM ** -0.5 * LOG2E

COL_A_U, COL_A_V, COL_A_Z = 0, 256, 512
COL_B_B, COL_B_C, COL_B_X, COL_B_Z = 768, 1024, 1280, 1536
COL_C_Q, COL_C_K, COL_C_V, COL_C_Z = 1792, 2048, 2176, 2304
COL_D_Q, COL_D_K, COL_D_V, COL_D_Z = 2560, 2816, 3072, 3328
D_IN = 3584

P_H, P_ZA = 0, 256
P_BB, P_G, P_ZB = 512, 768, 1024
P_CQ, P_CQS, P_CK, P_CV, P_ZC = 1280, 1536, 1792, 1920, 2048
P_DQ, P_DK, P_DV, P_ZD = 2304, 2560, 2816, 3072
D_PROJ = 3328
W_AB = P_CQ

LANES = 128
SUBLANES = 8
HALO = 16
VMEM_LIMIT = 56 * 1024 * 1024

TM_PROJ = 512
T_MIX = 512
PROJ_CHUNK = 512


def _silu(x):
    return x * jax.nn.sigmoid(x)


def _params():
    return pltpu.CompilerParams(
        dimension_semantics=("parallel", "parallel"), vmem_limit_bytes=VMEM_LIMIT)


def _in_proj_kernel(x_ref, g_ref, w_ref, cos_ref, sin_ref, o_ref, *, layer):
    x = x_ref[0]
    ms = jnp.mean(x * x, axis=-1, keepdims=True)
    h = (x * lax.rsqrt(ms + EPS) * g_ref[layer:layer + 1, :]).astype(jnp.bfloat16)
    lane = lax.broadcasted_iota(jnp.int32, (1, LANES), 1)
    first_half = (lane % HEAD_DIM) < HALF

    def rope(t):
        partner = jnp.where(first_half, -pltpu.roll(t, LANES - HALF, 1), pltpu.roll(t, HALF, 1))
        return t * cos_ref[...] + partner * sin_ref[...]

    def store(col, t):
        o_ref[0, :, col:col + t.shape[1]] = t.astype(o_ref.dtype)

    def chunk(c0):
        return jnp.dot(h, w_ref[0, :, c0:c0 + PROJ_CHUNK], preferred_element_type=jnp.float32)

    g = D_GROUP
    acc = chunk(COL_A_U)
    store(P_H, acc[:, :g] * jax.nn.sigmoid(acc[:, g:]))
    acc = chunk(COL_A_Z)
    store(P_ZA, _silu(acc[:, :g]))
    store(P_BB, acc[:, g:])
    acc = chunk(COL_B_C)
    store(P_G, acc[:, :g] * acc[:, g:])
    acc = chunk(COL_B_Z)
    store(P_ZB, _silu(acc[:, :g]))
    for s0 in range(0, g, LANES):
        t = rope(acc[:, g + s0:g + s0 + LANES]) * Q_SCALE
        store(P_CQ + s0, t)
        store(P_CQS + s0, pltpu.roll(t, HEAD_DIM, 1))
    acc = chunk(COL_C_K)
    store(P_CK, rope(acc[:, :LANES]))
    store(P_CV, acc[:, LANES:g])
    store(P_ZC, _silu(acc[:, g:]))
    acc = chunk(COL_D_Q)
    store(P_DQ, acc[:, :g] * Q_SCALE)
    store(P_DK, acc[:, g:])
    acc = chunk(COL_D_V)
    store(P_DV, acc[:, :g])
    store(P_ZD, _silu(acc[:, g:]))


def _in_proj(x, g, w, cos, sin, layer):
    b, s, d = x.shape
    return pl.pallas_call(
        functools.partial(_in_proj_kernel, layer=layer),
        grid=(b, s // TM_PROJ),
        in_specs=[
            pl.BlockSpec((1, TM_PROJ, d), lambda bi, i: (bi, i, 0)),
            pl.BlockSpec(g.shape, lambda bi, i: (0, 0)),
            pl.BlockSpec((1, d, D_IN), lambda bi, i: (layer, 0, 0)),
            pl.BlockSpec((TM_PROJ, LANES), lambda bi, i: (i, 0)),
            pl.BlockSpec((TM_PROJ, LANES), lambda bi, i: (i, 0)),
        ],
        out_specs=pl.BlockSpec((1, TM_PROJ, D_PROJ), lambda bi, i: (bi, i, 0)),
        out_shape=jax.ShapeDtypeStruct((b, s, D_PROJ), jnp.bfloat16),
        compiler_params=_params(),
        name="in_proj",
    )(x, g, w, cos, sin)


def _conv_mix_kernel(cur_ref, prev_ref, next_ref, aw_ref, ab_ref, lg_ref, lb_ref, bw_ref,
                     o_ref, hbuf, gbuf, hph, gph, *, layer):
    i = pl.program_id(1)
    n = pl.num_programs(1)
    t = T_MIX
    f32 = jnp.float32
    has_prev = (i > 0).astype(f32)
    has_next = (i < n - 1).astype(f32)
    row = lambda ref, j: ref[layer, j:j + 1, :]
    col = lambda ref, rows, c: ref[0, rows, c:c + D_GROUP].astype(f32)

    full = slice(None)
    hbuf[0:HALO] = col(prev_ref, full, P_H) * has_prev
    hbuf[HALO:HALO + t] = col(cur_ref, full, P_H)
    hbuf[HALO + t:t + 2 * HALO] = col(next_ref, full, P_H) * has_next
    gbuf[0:HALO] = col(prev_ref, full, P_G) * has_prev
    gbuf[HALO:HALO + t] = col(cur_ref, full, P_G)
    gbuf[HALO + t:t + 2 * HALO] = col(next_ref, full, P_G) * has_next

    nph = t + 2 * HALO - SUBLANES
    for p in range(1, SUBLANES):
        hph[p - 1] = hbuf[p:p + nph]
    gph[0] = gbuf[1:1 + nph]
    gph[1] = gbuf[SUBLANES - 1:SUBLANES - 1 + nph]

    def tap(off, rows):
        p = off % SUBLANES
        if p == 0:
            return hbuf[off:off + rows, :]
        return hph[p - 1, off - p:off - p + rows, :]

    rc = 64
    for r0 in range(0, t, rc):
        rows = slice(r0, r0 + rc)
        acc = jnp.broadcast_to(row(ab_ref, 0), (rc, D_GROUP))
        for j in range(CONV_A_WIDTH):
            acc = acc + aw_ref[layer, j:j + 1, :] * tap(HALO - CONV_A_PAD + j + r0, rc)
        mu = jnp.mean(acc, axis=-1, keepdims=True)
        xc = acc - mu
        var = jnp.mean(xc * xc, axis=-1, keepdims=True)
        hn = xc * lax.rsqrt(var + EPS) * row(lg_ref, 0) + row(lb_ref, 0)
        o_ref[0, rows, 0:D_GROUP] = (_silu(hn) * col(cur_ref, rows, P_ZA)).astype(o_ref.dtype)
        gm = HALO + r0 - SUBLANES
        conv = (row(bw_ref, 0) * gph[1, gm:gm + rc, :]
                + row(bw_ref, 1) * gbuf[HALO + r0:HALO + r0 + rc, :]
                + row(bw_ref, 2) * gph[0, HALO + r0:HALO + r0 + rc, :])
        yb = col(cur_ref, rows, P_BB) * conv * col(cur_ref, rows, P_ZB)
        o_ref[0, rows, D_GROUP:2 * D_GROUP] = yb.astype(o_ref.dtype)


def _conv_mix(proj, aw, ab, lg, lb, bw, layer):
    b, s, _ = proj.shape
    t = T_MIX
    hb = t // HALO
    nh = s // HALO
    whole = lambda a: pl.BlockSpec(a.shape, lambda bi, i: (0,) * a.ndim)
    return pl.pallas_call(
        functools.partial(_conv_mix_kernel, layer=layer),
        grid=(b, s // t),
        in_specs=[
            pl.BlockSpec((1, t, W_AB), lambda bi, i: (bi, i, 0)),
            pl.BlockSpec((1, HALO, W_AB), lambda bi, i: (bi, jnp.maximum(i * hb - 1, 0), 0)),
            pl.BlockSpec((1, HALO, W_AB), lambda bi, i: (bi, jnp.minimum((i + 1) * hb, nh - 1), 0)),
            whole(aw), whole(ab), whole(lg), whole(lb), whole(bw),
        ],
        out_specs=pl.BlockSpec((1, t, 2 * D_GROUP), lambda bi, i: (bi, i, 0)),
        out_shape=jax.ShapeDtypeStruct((b, s, 2 * D_GROUP), jnp.bfloat16),
        scratch_shapes=[pltpu.VMEM((t + 2 * HALO, D_GROUP), jnp.float32),
                        pltpu.VMEM((t + 2 * HALO, D_GROUP), jnp.float32),
                        pltpu.VMEM((SUBLANES - 1, t + 2 * HALO - SUBLANES, D_GROUP), jnp.float32),
                        pltpu.VMEM((2, t + 2 * HALO - SUBLANES, D_GROUP), jnp.float32)],
        compiler_params=_params(),
        name="conv_mix",
    )(proj, proj, proj, aw, ab, lg, lb, bw)


def _swa_kernel(sink_ref, q_ref, qs_ref, z_ref, kv_ref, o_ref, *, layer):
    i = pl.program_id(1)
    s_len = kv_ref.shape[1]
    nkeys = 3 * SWA_BLOCK
    b_ = SWA_BLOCK
    f32 = jnp.float32
    lane = lax.broadcasted_iota(jnp.int32, (1, LANES), 1)
    lo = lane < HEAD_DIM
    zero = jnp.zeros((), jnp.bfloat16)
    ones = jnp.ones((nkeys, LANES), jnp.bfloat16)
    for nb in range(T_MIX // SWA_BLOCK):
        q0 = i * T_MIX + nb * SWA_BLOCK
        ws = pl.multiple_of(jnp.clip(q0 - SWA_BLOCK, 0, s_len - nkeys), SWA_BLOCK)
        rows = slice(nb * SWA_BLOCK, (nb + 1) * SWA_BLOCK)
        win = pl.ds(ws, nkeys)
        lhs = jnp.concatenate([
            jnp.where(lo, q_ref[0, rows, 0:LANES], zero),
            jnp.where(lo, qs_ref[0, rows, 0:LANES], zero),
            jnp.where(lo, zero, qs_ref[0, rows, LANES:2 * LANES]),
            jnp.where(lo, zero, q_ref[0, rows, LANES:2 * LANES]),
        ], axis=0)
        kw = kv_ref[0, win, 0:LANES]
        vw = jnp.concatenate([kv_ref[0, win, LANES:2 * LANES], ones], axis=1)
        sc = jnp.einsum("qd,kd->qk", lhs, kw, preferred_element_type=f32)
        qpos = q0 + lax.broadcasted_iota(jnp.int32, (SWA_BLOCK, nkeys), 0)
        kpos = ws + lax.broadcasted_iota(jnp.int32, (SWA_BLOCK, nkeys), 1)
        valid = jnp.abs(kpos - qpos) <= SWA_WINDOW
        num, den = [], []
        for h in range(SWA_HEADS):
            sh = jnp.where(valid, sc[h * b_:(h + 1) * b_], NEG_INF)
            sink = sink_ref[layer, h]
            m = jnp.maximum(jnp.max(sh, axis=-1, keepdims=True), sink)
            e = jnp.exp2(sh - m)
            pv = jnp.dot(e.astype(jnp.bfloat16), vw, preferred_element_type=f32)
            num.append(pv[:, :LANES])
            den.append(pv[:, LANES:] + jnp.exp2(sink - m))
        og0 = jnp.where(lo, num[0], pltpu.roll(num[1], HEAD_DIM, 1)) / jnp.where(lo, den[0], den[1])
        og1 = jnp.where(lo, pltpu.roll(num[2], HEAD_DIM, 1), num[3]) / jnp.where(lo, den[2], den[3])
        y = jnp.concatenate([og0, og1], axis=-1) * z_ref[0, rows, :].astype(f32)
        o_ref[0, rows, :] = y.astype(o_ref.dtype)


def _swa(proj, sink, layer):
    b, s, _ = proj.shape
    t = T_MIX
    return pl.pallas_call(
        functools.partial(_swa_kernel, layer=layer),
        grid_spec=pltpu.PrefetchScalarGridSpec(
            num_scalar_prefetch=1,
            grid=(b, s // t),
            in_specs=[
                pl.BlockSpec((1, t, D_GROUP), lambda bi, i, sk: (bi, i, P_CQ // D_GROUP)),
                pl.BlockSpec((1, t, D_GROUP), lambda bi, i, sk: (bi, i, P_CQS // D_GROUP)),
                pl.BlockSpec((1, t, D_GROUP), lambda bi, i, sk: (bi, i, P_ZC // D_GROUP)),
                pl.BlockSpec((1, s, D_GROUP), lambda bi, i, sk: (bi, 0, P_CK // D_GROUP)),
            ],
            out_specs=pl.BlockSpec((1, t, D_GROUP), lambda bi, i, sk: (bi, i, 0)),
        ),
        out_shape=jax.ShapeDtypeStruct((b, s, D_GROUP), jnp.bfloat16),
        compiler_params=_params(),
        name="swa",
    )(sink, proj, proj, proj, proj)


def _nbr_kernel(q_ref, z_ref, k_ref, v_ref, bias_ref, o_ref, *, layer):
    i = pl.program_id(1)
    rows_total = k_ref.shape[1] // GRID_W
    rows_tile = T_MIX // GRID_W
    nkeys = NA_KH * GRID_W
    f32 = jnp.float32
    lane = lax.broadcasted_iota(jnp.int32, (1, LANES), 1)
    lo = lane < HEAD_DIM
    zero = jnp.zeros((), jnp.bfloat16)
    ones = jnp.ones((nkeys, LANES), jnp.bfloat16)

    for rr in range(rows_tile):
        r = i * rows_tile + rr
        r0 = jnp.clip(r - NA_KH // 2, 0, rows_total - NA_KH)
        d0 = r0 - r + (NA_KH - 1)
        ks = pl.multiple_of(r0 * GRID_W, GRID_W)
        qrow = slice(rr * GRID_W, (rr + 1) * GRID_W)
        outs = []
        for g in range(2):
            cols = slice(g * LANES, (g + 1) * LANES)
            qg = q_ref[0, qrow, cols]
            lhs = jnp.concatenate([jnp.where(lo, qg, zero), jnp.where(lo, zero, qg)], axis=0)
            kw = k_ref[0, pl.ds(ks, nkeys), cols]
            vw = jnp.concatenate([v_ref[0, pl.ds(ks, nkeys), cols], ones], axis=1)
            sc = jnp.einsum("qd,kd->qk", lhs, kw, preferred_element_type=f32)
            bias = jnp.concatenate(
                [jnp.concatenate([bias_ref[layer, 2 * g + hh, d0 + 2 * m]
                                  for m in range(NA_KH // 2)], axis=-1)
                 for hh in range(2)], axis=0)
            sc = sc + bias
            m_ = jnp.max(sc, axis=-1, keepdims=True)
            e = jnp.exp2(sc - m_)
            pv = jnp.dot(e.astype(jnp.bfloat16), vw, preferred_element_type=f32)
            pv = jnp.where(jnp.concatenate([lo, lo], axis=1), pv[0:GRID_W], pv[GRID_W:2 * GRID_W])
            outs.append(pv[:, :LANES] / pv[:, LANES:])
        y = jnp.concatenate(outs, axis=-1) * z_ref[0, qrow, :].astype(f32)
        o_ref[0, qrow, :] = y.astype(o_ref.dtype)


def _nbr(proj, bias, layer):
    b, s, _ = proj.shape
    t = T_MIX
    return pl.pallas_call(
        functools.partial(_nbr_kernel, layer=layer),
        grid=(b, s // t),
        in_specs=[
            pl.BlockSpec((1, t, D_GROUP), lambda bi, i: (bi, i, P_DQ // D_GROUP)),
            pl.BlockSpec((1, t, D_GROUP), lambda bi, i: (bi, i, P_ZD // D_GROUP)),
            pl.BlockSpec((1, s, D_GROUP), lambda bi, i: (bi, 0, P_DK // D_GROUP)),
            pl.BlockSpec((1, s, D_GROUP), lambda bi, i: (bi, 0, P_DV // D_GROUP)),
            pl.BlockSpec(bias.shape, lambda bi, i: (0,) * bias.ndim),
        ],
        out_specs=pl.BlockSpec((1, t, D_GROUP), lambda bi, i: (bi, i, 0)),
        out_shape=jax.ShapeDtypeStruct((b, s, D_GROUP), jnp.bfloat16),
        compiler_params=_params(),
        name="nbr",
    )(proj, proj, proj, proj, bias)


def _out_proj_kernel(x_ref, yab_ref, yc_ref, yd_ref, w_ref, g_ref, o_ref, *, final_norm):
    f32 = jnp.float32
    acc = jnp.dot(yab_ref[0], w_ref[0, 0:2 * D_GROUP, :], preferred_element_type=f32)
    acc = acc + jnp.dot(yc_ref[0], w_ref[0, 2 * D_GROUP:3 * D_GROUP, :], preferred_element_type=f32)
    acc = acc + jnp.dot(yd_ref[0], w_ref[0, 3 * D_GROUP:4 * D_GROUP, :], preferred_element_type=f32)
    x = x_ref[0] + acc
    if final_norm:
        ms = jnp.mean(x * x, axis=-1, keepdims=True)
        x = x * lax.rsqrt(ms + EPS) * g_ref[...]
    o_ref[0] = x


def _out_proj(x, yab, yc, yd, w, g, layer, final_norm):
    b, s, d = x.shape
    tm = TM_PROJ
    tile = lambda width: pl.BlockSpec((1, tm, width), lambda bi, i: (bi, i, 0))
    return pl.pallas_call(
        functools.partial(_out_proj_kernel, final_norm=final_norm),
        grid=(b, s // tm),
        in_specs=[tile(d), tile(2 * D_GROUP), tile(D_GROUP), tile(D_GROUP),
                  pl.BlockSpec((1,) + w.shape[1:], lambda bi, i: (layer, 0, 0)),
                  pl.BlockSpec((1, d), lambda bi, i: (0, 0))],
        out_specs=tile(d),
        out_shape=jax.ShapeDtypeStruct((b, s, d), jnp.float32),
        compiler_params=_params(),
        name="out_proj",
    )(x, yab, yc, yd, w, g)


def _rope_tables(s):
    inv_freq = ROPE_THETA ** (-jnp.arange(0, HEAD_DIM, 2, dtype=jnp.float32) / HEAD_DIM)
    inv_freq = jnp.tile(inv_freq, LANES // HALF)
    ang = jnp.arange(s, dtype=jnp.float32)[:, None] * inv_freq[None, :]
    return jnp.cos(ang), jnp.sin(ang)


def _nbr_bias_table(rpb):
    c = np.arange(GRID_W)
    c0 = np.clip(c - NA_KW // 2, 0, GRID_W - NA_KW)
    col_ok = (c[None, :] >= c0[:, None]) & (c[None, :] < c0[:, None] + NA_KW)
    dc = np.clip(c[None, :] - c[:, None], -(NA_KW - 1), NA_KW - 1) + (NA_KW - 1)
    onehot = (dc[None] == np.arange(2 * NA_KW - 1)[:, None, None]).astype(np.float32)
    full = jnp.einsum("lhdc,cqk->lhdqk", rpb, jnp.asarray(onehot),
                      precision=lax.Precision.HIGHEST)
    full = jnp.where(jnp.asarray(col_ok), full * LOG2E, NEG_INF)
    pad = jnp.full(full.shape[:2] + (2,) + full.shape[3:], NEG_INF, full.dtype)
    full = jnp.concatenate([full, pad], axis=2)
    return jnp.concatenate([full[:, :, :-1], full[:, :, 1:]], axis=-1)


def kernel(x, norm_g, w_in, w_out, conv_a_w, conv_a_b, ln_a_g, ln_a_b, conv_b_w, swa_sink,
           na_rpb, final_norm_g):
    depth = norm_g.shape[0]
    s = x.shape[1]
    cos_t, sin_t = _rope_tables(s)
    w_in_b = w_in.astype(jnp.bfloat16)
    w_out_b = w_out.astype(jnp.bfloat16)
    bias = _nbr_bias_table(na_rpb)
    sink = swa_sink * LOG2E
    fg = final_norm_g.reshape(1, -1)
    vec = lambda a: a.reshape(depth, 1, -1)
    for l in range(depth):
        proj = _in_proj(x, norm_g, w_in_b, cos_t, sin_t, l)
        yab = _conv_mix(proj, conv_a_w, vec(conv_a_b), vec(ln_a_g), vec(ln_a_b), conv_b_w, l)
        yc = _swa(proj, sink, l)
        yd = _nbr(proj, bias, l)
        x = _out_proj(x, yab, yc, yd, w_out_b, fg, l, final_norm=(l == depth - 1))
    return x
```

```python
import functools
import math

import numpy as np
import jax
import jax.numpy as jnp
from jax import lax
from jax.experimental import pallas as pl
from jax.experimental.pallas import tpu as pltpu

D_MODEL = 1024
D_GROUP = 256
HEAD_DIM = 64
HALF = HEAD_DIM // 2
GRID_W = 64
CONV_A_WIDTH = 31
CONV_A_PAD = (CONV_A_WIDTH - 1) // 2
CONV_B_WIDTH = 3
SWA_WINDOW = 128
SWA_BLOCK = 128
SWA_HEADS = 4
NA_KH = 8
NA_KW = 16
ROPE_THETA = 10000.0
EPS = 1e-6
NEG_INF = -1e30
LOG2E = math.log2(math.e)
Q_SCALE = HEAD_DIM ** -0.5 * LOG2E

COL_A_U, COL_A_V, COL_A_Z = 0, 256, 512
COL_B_B, COL_B_C, COL_B_X, COL_B_Z = 768, 1024, 1280, 1536
COL_C_Q, COL_C_K, COL_C_V, COL_C_Z = 1792, 2048, 2176, 2304
COL_D_Q, COL_D_K, COL_D_V, COL_D_Z = 2560, 2816, 3072, 3328
D_IN = 3584

P_H, P_ZA = 0, 256
P_BB, P_G, P_ZB = 512, 768, 1024
P_CQ, P_CQS, P_CK, P_CV, P_ZC = 1280, 1536, 1792, 1920, 2048
P_DQ, P_DK, P_DV, P_ZD = 2304, 2560, 2816, 3072
D_PROJ = 3328
W_AB = P_CQ

LANES = 128
SUBLANES = 8
HALO = 16
VMEM_LIMIT = 56 * 1024 * 1024

TM_PROJ = 512
T_MIX = 512
PROJ_CHUNK = 512


def _silu(x):
    return x * jax.nn.sigmoid(x)


def _params():
    return pltpu.CompilerParams(
        dimension_semantics=("parallel", "parallel"), vmem_limit_bytes=VMEM_LIMIT)


def _in_proj_kernel(x_ref, g_ref, w_ref, cos_ref, sin_ref, o_ref, *, layer):
    x = x_ref[0]
    ms = jnp.mean(x * x, axis=-1, keepdims=True)
    h = (x * lax.rsqrt(ms + EPS) * g_ref[layer:layer + 1, :]).astype(jnp.bfloat16)
    lane = lax.broadcasted_iota(jnp.int32, (1, LANES), 1)
    first_half = (lane % HEAD_DIM) < HALF

    def rope(t):
        partner = jnp.where(first_half, -pltpu.roll(t, LANES - HALF, 1), pltpu.roll(t, HALF, 1))
        return t * cos_ref[...] + partner * sin_ref[...]

    def store(col, t):
        o_ref[0, :, col:col + t.shape[1]] = t.astype(o_ref.dtype)

    def chunk(c0):
        return jnp.dot(h, w_ref[0, :, c0:c0 + PROJ_CHUNK], preferred_element_type=jnp.float32)

    g = D_GROUP
    acc = chunk(COL_A_U)
    store(P_H, acc[:, :g] * jax.nn.sigmoid(acc[:, g:]))
    acc = chunk(COL_A_Z)
    store(P_ZA, _silu(acc[:, :g]))
    store(P_BB, acc[:, g:])
    acc = chunk(COL_B_C)
    store(P_G, acc[:, :g] * acc[:, g:])
    acc = chunk(COL_B_Z)
    store(P_ZB, _silu(acc[:, :g]))
    for s0 in range(0, g, LANES):
        t = rope(acc[:, g + s0:g + s0 + LANES]) * Q_SCALE
        store(P_CQ + s0, t)
        store(P_CQS + s0, pltpu.roll(t, HEAD_DIM, 1))
    acc = chunk(COL_C_K)
    store(P_CK, rope(acc[:, :LANES]))
    store(P_CV, acc[:, LANES:g])
    store(P_ZC, _silu(acc[:, g:]))
    acc = chunk(COL_D_Q)
    store(P_DQ, acc[:, :g] * Q_SCALE)
    store(P_DK, acc[:, g:])
    acc = chunk(COL_D_V)
    store(P_DV, acc[:, :g])
    store(P_ZD, _silu(acc[:, g:]))


def _in_proj(x, g, w, cos, sin, layer):
    b, s, d = x.shape
    return pl.pallas_call(
        functools.partial(_in_proj_kernel, layer=layer),
        grid=(b, s // TM_PROJ),
        in_specs=[
            pl.BlockSpec((1, TM_PROJ, d), lambda bi, i: (bi, i, 0)),
            pl.BlockSpec(g.shape, lambda bi, i: (0, 0)),
            pl.BlockSpec((1, d, D_IN), lambda bi, i: (layer, 0, 0)),
            pl.BlockSpec((TM_PROJ, LANES), lambda bi, i: (i, 0)),
            pl.BlockSpec((TM_PROJ, LANES), lambda bi, i: (i, 0)),
        ],
        out_specs=pl.BlockSpec((1, TM_PROJ, D_PROJ), lambda bi, i: (bi, i, 0)),
        out_shape=jax.ShapeDtypeStruct((b, s, D_PROJ), jnp.bfloat16),
        compiler_params=_params(),
        name="in_proj",
    )(x, g, w, cos, sin)


def _conv_mix_kernel(cur_ref, prev_ref, next_ref, aw_ref, ab_ref, lg_ref, lb_ref, bw_ref,
                     o_ref, hbuf, gbuf, hph, gph, *, layer):
    i = pl.program_id(1)
    n = pl.num_programs(1)
    t = T_MIX
    f32 = jnp.float32
    has_prev = (i > 0).astype(f32)
    has_next = (i < n - 1).astype(f32)
    row = lambda ref, j: ref[layer, j:j + 1, :]
    col = lambda ref, rows, c: ref[0, rows, c:c + D_GROUP].astype(f32)

    full = slice(None)
    hbuf[0:HALO] = col(prev_ref, full, P_H) * has_prev
    hbuf[HALO:HALO + t] = col(cur_ref, full, P_H)
    hbuf[HALO + t:t + 2 * HALO] = col(next_ref, full, P_H) * has_next
    gbuf[0:HALO] = col(prev_ref, full, P_G) * has_prev
    gbuf[HALO:HALO + t] = col(cur_ref, full, P_G)
    gbuf[HALO + t:t + 2 * HALO] = col(next_ref, full, P_G) * has_next

    nph = t + 2 * HALO - SUBLANES
    for p in range(1, SUBLANES):
        hph[p - 1] = hbuf[p:p + nph]
    gph[0] = gbuf[1:1 + nph]
    gph[1] = gbuf[SUBLANES - 1:SUBLANES - 1 + nph]

    def tap(off, rows):
        p = off % SUBLANES
        if p == 0:
            return hbuf[off:off + rows, :]
        return hph[p - 1, off - p:off - p + rows, :]

    rc = 64
    for r0 in range(0, t, rc):
        rows = slice(r0, r0 + rc)
        acc = jnp.broadcast_to(row(ab_ref, 0), (rc, D_GROUP))
        for j in range(CONV_A_WIDTH):
            acc = acc + aw_ref[layer, j:j + 1, :] * tap(HALO - CONV_A_PAD + j + r0, rc)
        mu = jnp.mean(acc, axis=-1, keepdims=True)
        xc = acc - mu
        var = jnp.mean(xc * xc, axis=-1, keepdims=True)
        hn = xc * lax.rsqrt(var + EPS) * row(lg_ref, 0) + row(lb_ref, 0)
        o_ref[0, rows, 0:D_GROUP] = (_silu(hn) * col(cur_ref, rows, P_ZA)).astype(o_ref.dtype)
        gm = HALO + r0 - SUBLANES
        conv = (row(bw_ref, 0) * gph[1, gm:gm + rc, :]
                + row(bw_ref, 1) * gbuf[HALO + r0:HALO + r0 + rc, :]
                + row(bw_ref, 2) * gph[0, HALO + r0:HALO + r0 + rc, :])
        yb = col(cur_ref, rows, P_BB) * conv * col(cur_ref, rows, P_ZB)
        o_ref[0, rows, D_GROUP:2 * D_GROUP] = yb.astype(o_ref.dtype)


def _conv_mix(proj, aw, ab, lg, lb, bw, layer):
    b, s, _ = proj.shape
    t = T_MIX
    hb = t // HALO
    nh = s // HALO
    whole = lambda a: pl.BlockSpec(a.shape, lambda bi, i: (0,) * a.ndim)
    return pl.pallas_call(
        functools.partial(_conv_mix_kernel, layer=layer),
        grid=(b, s // t),
        in_specs=[
            pl.BlockSpec((1, t, W_AB), lambda bi, i: (bi, i, 0)),
            pl.BlockSpec((1, HALO, W_AB), lambda bi, i: (bi, jnp.maximum(i * hb - 1, 0), 0)),
            pl.BlockSpec((1, HALO, W_AB), lambda bi, i: (bi, jnp.minimum((i + 1) * hb, nh - 1), 0)),
            whole(aw), whole(ab), whole(lg), whole(lb), whole(bw),
        ],
        out_specs=pl.BlockSpec((1, t, 2 * D_GROUP), lambda bi, i: (bi, i, 0)),
        out_shape=jax.ShapeDtypeStruct((b, s, 2 * D_GROUP), jnp.bfloat16),
        scratch_shapes=[pltpu.VMEM((t + 2 * HALO, D_GROUP), jnp.float32),
                        pltpu.VMEM((t + 2 * HALO, D_GROUP), jnp.float32),
                        pltpu.VMEM((SUBLANES - 1, t + 2 * HALO - SUBLANES, D_GROUP), jnp.float32),
                        pltpu.VMEM((2, t + 2 * HALO - SUBLANES, D_GROUP), jnp.float32)],
        compiler_params=_params(),
        name="conv_mix",
    )(proj, proj, proj, aw, ab, lg, lb, bw)


def _swa_tile(sink_ref, q_ref, qs_ref, z_ref, kv_ref, y_ref, i, layer):
    s_len = kv_ref.shape[1]
    nkeys = 3 * SWA_BLOCK
    b_ = SWA_BLOCK
    f32 = jnp.float32
    lane = lax.broadcasted_iota(jnp.int32, (1, LANES), 1)
    lo = lane < HEAD_DIM
    zero = jnp.zeros((), jnp.bfloat16)
    ones = jnp.ones((nkeys, LANES), jnp.bfloat16)
    for nb in range(T_MIX // SWA_BLOCK):
        q0 = i * T_MIX + nb * SWA_BLOCK
        ws = pl.multiple_of(jnp.clip(q0 - SWA_BLOCK, 0, s_len - nkeys), SWA_BLOCK)
        rows = slice(nb * SWA_BLOCK, (nb + 1) * SWA_BLOCK)
        win = pl.ds(ws, nkeys)
        lhs = jnp.concatenate([
            jnp.where(lo, q_ref[0, rows, 0:LANES], zero),
            jnp.where(lo, qs_ref[0, rows, 0:LANES], zero),
            jnp.where(lo, zero, qs_ref[0, rows, LANES:2 * LANES]),
            jnp.where(lo, zero, q_ref[0, rows, LANES:2 * LANES]),
        ], axis=0)
        kw = kv_ref[0, win, 0:LANES]
        vw = jnp.concatenate([kv_ref[0, win, LANES:2 * LANES], ones], axis=1)
        sc = jnp.einsum("qd,kd->qk", lhs, kw, preferred_element_type=f32)
        qpos = q0 + lax.broadcasted_iota(jnp.int32, (SWA_BLOCK, nkeys), 0)
        kpos = ws + lax.broadcasted_iota(jnp.int32, (SWA_BLOCK, nkeys), 1)
        valid = jnp.abs(kpos - qpos) <= SWA_WINDOW
        num, den = [], []
        for h in range(SWA_HEADS):
            sh = jnp.where(valid, sc[h * b_:(h + 1) * b_], NEG_INF)
            sink = sink_ref[layer, h]
            m = jnp.maximum(jnp.max(sh, axis=-1, keepdims=True), sink)
            e = jnp.exp2(sh - m)
            pv = jnp.dot(e.astype(jnp.bfloat16), vw, preferred_element_type=f32)
            num.append(pv[:, :LANES])
            den.append(pv[:, LANES:] + jnp.exp2(sink - m))
        og0 = jnp.where(lo, num[0], pltpu.roll(num[1], HEAD_DIM, 1)) / jnp.where(lo, den[0], den[1])
        og1 = jnp.where(lo, pltpu.roll(num[2], HEAD_DIM, 1), num[3]) / jnp.where(lo, den[2], den[3])
        y = jnp.concatenate([og0, og1], axis=-1) * z_ref[0, rows, :].astype(f32)
        y_ref[rows, 0:D_GROUP] = y.astype(y_ref.dtype)


def _nbr_tile(q_ref, z_ref, k_ref, v_ref, bias_ref, y_ref, i):
    rows_total = k_ref.shape[1] // GRID_W
    rows_tile = T_MIX // GRID_W
    nkeys = NA_KH * GRID_W
    f32 = jnp.float32
    lane = lax.broadcasted_iota(jnp.int32, (1, LANES), 1)
    lo = lane < HEAD_DIM
    zero = jnp.zeros((), jnp.bfloat16)
    ones = jnp.ones((nkeys, LANES), jnp.bfloat16)

    for rr in range(rows_tile):
        r = i * rows_tile + rr
        r0 = jnp.clip(r - NA_KH // 2, 0, rows_total - NA_KH)
        d0 = r0 - r + (NA_KH - 1)
        ks = pl.multiple_of(r0 * GRID_W, GRID_W)
        qrow = slice(rr * GRID_W, (rr + 1) * GRID_W)
        outs = []
        for g in range(2):
            cols = slice(g * LANES, (g + 1) * LANES)
            qg = q_ref[0, qrow, cols]
            lhs = jnp.concatenate([jnp.where(lo, qg, zero), jnp.where(lo, zero, qg)], axis=0)
            kw = k_ref[0, pl.ds(ks, nkeys), cols]
            vw = jnp.concatenate([v_ref[0, pl.ds(ks, nkeys), cols], ones], axis=1)
            sc = jnp.einsum("qd,kd->qk", lhs, kw, preferred_element_type=f32)
            bias = jnp.concatenate(
                [jnp.concatenate([bias_ref[0, 2 * g + hh, d0 + 2 * m]
                                  for m in range(NA_KH // 2)], axis=-1)
                 for hh in range(2)], axis=0)
            sc = sc + bias
            m_ = jnp.max(sc, axis=-1, keepdims=True)
            e = jnp.exp2(sc - m_)
            pv = jnp.dot(e.astype(jnp.bfloat16), vw, preferred_element_type=f32)
            pv = jnp.where(jnp.concatenate([lo, lo], axis=1), pv[0:GRID_W], pv[GRID_W:2 * GRID_W])
            outs.append(pv[:, :LANES] / pv[:, LANES:])
        y = jnp.concatenate(outs, axis=-1) * z_ref[0, qrow, :].astype(f32)
        y_ref[qrow, D_GROUP:2 * D_GROUP] = y.astype(y_ref.dtype)


def _attn_out_kernel(sink_ref, cq_ref, cqs_ref, cz_ref, ckv_ref, dq_ref, dz_ref, dk_ref, dv_ref,
                     bias_ref, yab_ref, x_ref, w_ref, g_ref, o_ref, y_ref, *, layer, final_norm):
    i = pl.program_id(1)
    f32 = jnp.float32
    _swa_tile(sink_ref, cq_ref, cqs_ref, cz_ref, ckv_ref, y_ref, i, layer)
    _nbr_tile(dq_ref, dz_ref, dk_ref, dv_ref, bias_ref, y_ref, i)
    acc = jnp.dot(yab_ref[0], w_ref[0, 0:2 * D_GROUP, :], preferred_element_type=f32)
    acc = acc + jnp.dot(y_ref[...], w_ref[0, 2 * D_GROUP:4 * D_GROUP, :], preferred_element_type=f32)
    x = x_ref[0] + acc
    if final_norm:
        ms = jnp.mean(x * x, axis=-1, keepdims=True)
        x = x * lax.rsqrt(ms + EPS) * g_ref[...]
    o_ref[0] = x


def _attn_out(proj, yab, x, sink, bias, w, g, layer, final_norm):
    b, s, d = x.shape
    t = T_MIX
    tile = lambda col: pl.BlockSpec((1, t, D_GROUP), lambda bi, i, sk: (bi, i, col // D_GROUP))
    seq = lambda col: pl.BlockSpec((1, s, D_GROUP), lambda bi, i, sk: (bi, 0, col // D_GROUP))
    return pl.pallas_call(
        functools.partial(_attn_out_kernel, layer=layer, final_norm=final_norm),
        grid_spec=pltpu.PrefetchScalarGridSpec(
            num_scalar_prefetch=1,
            grid=(b, s // t),
            in_specs=[
                tile(P_CQ), tile(P_CQS), tile(P_ZC), seq(P_CK),
                tile(P_DQ), tile(P_ZD), seq(P_DK), seq(P_DV),
                pl.BlockSpec((1,) + bias.shape[1:], lambda bi, i, sk: (layer, 0, 0, 0, 0)),
                pl.BlockSpec((1, t, 2 * D_GROUP), lambda bi, i, sk: (bi, i, 0)),
                pl.BlockSpec((1, t, d), lambda bi, i, sk: (bi, i, 0)),
                pl.BlockSpec((1,) + w.shape[1:], lambda bi, i, sk: (layer, 0, 0)),
                pl.BlockSpec((1, d), lambda bi, i, sk: (0, 0)),
            ],
            out_specs=pl.BlockSpec((1, t, d), lambda bi, i, sk: (bi, i, 0)),
            scratch_shapes=[pltpu.VMEM((t, 2 * D_GROUP), jnp.bfloat16)],
        ),
        out_shape=jax.ShapeDtypeStruct((b, s, d), jnp.float32),
        compiler_params=_params(),
        name="attn_out",
    )(sink, proj, proj, proj, proj, proj, proj, proj, proj, bias, yab, x, w, g)


def _rope_tables(s):
    inv_freq = ROPE_THETA ** (-jnp.arange(0, HEAD_DIM, 2, dtype=jnp.float32) / HEAD_DIM)
    inv_freq = jnp.tile(inv_freq, LANES // HALF)
    ang = jnp.arange(s, dtype=jnp.float32)[:, None] * inv_freq[None, :]
    return jnp.cos(ang), jnp.sin(ang)


def _nbr_bias_table(rpb):
    c = np.arange(GRID_W)
    c0 = np.clip(c - NA_KW // 2, 0, GRID_W - NA_KW)
    col_ok = (c[None, :] >= c0[:, None]) & (c[None, :] < c0[:, None] + NA_KW)
    dc = np.clip(c[None, :] - c[:, None], -(NA_KW - 1), NA_KW - 1) + (NA_KW - 1)
    onehot = (dc[None] == np.arange(2 * NA_KW - 1)[:, None, None]).astype(np.float32)
    full = jnp.einsum("lhdc,cqk->lhdqk", rpb, jnp.asarray(onehot),
                      precision=lax.Precision.HIGHEST)
    full = jnp.where(jnp.asarray(col_ok), full * LOG2E, NEG_INF)
    pad = jnp.full(full.shape[:2] + (2,) + full.shape[3:], NEG_INF, full.dtype)
    full = jnp.concatenate([full, pad], axis=2)
    return jnp.concatenate([full[:, :, :-1], full[:, :, 1:]], axis=-1)


def kernel(x, norm_g, w_in, w_out, conv_a_w, conv_a_b, ln_a_g, ln_a_b, conv_b_w, swa_sink,
           na_rpb, final_norm_g):
    depth = norm_g.shape[0]
    s = x.shape[1]
    cos_t, sin_t = _rope_tables(s)
    w_in_b = w_in.astype(jnp.bfloat16)
    w_out_b = w_out.astype(jnp.bfloat16)
    bias = _nbr_bias_table(na_rpb)
    sink = swa_sink * LOG2E
    fg = final_norm_g.reshape(1, -1)
    vec = lambda a: a.reshape(depth, 1, -1)
    for l in range(depth):
        proj = _in_proj(x, norm_g, w_in_b, cos_t, sin_t, l)
        yab = _conv_mix(proj, conv_a_w, vec(conv_a_b), vec(ln_a_g), vec(ln_a_b), conv_b_w, l)
        x = _attn_out(proj, yab, x, sink, bias, w_out_b, fg, l, final_norm=(l == depth - 1))
    return x
```

```python
import functools
import math

import numpy as np
import jax
import jax.numpy as jnp
from jax import lax
from jax.experimental import pallas as pl
from jax.experimental.pallas import tpu as pltpu

D_MODEL = 1024
D_GROUP = 256
HEAD_DIM = 64
HALF = HEAD_DIM // 2
GRID_W = 64
CONV_A_WIDTH = 31
CONV_A_PAD = (CONV_A_WIDTH - 1) // 2
CONV_B_WIDTH = 3
SWA_WINDOW = 128
SWA_BLOCK = 128
SWA_HEADS = 4
NA_KH = 8
NA_KW = 16
ROPE_THETA = 10000.0
EPS = 1e-6
NEG_INF = -1e30
LOG2E = math.log2(math.e)
Q_SCALE = HEAD_DIM ** -0.5 * LOG2E

COL_A_U, COL_A_V, COL_A_Z = 0, 256, 512
COL_B_B, COL_B_C, COL_B_X, COL_B_Z = 768, 1024, 1280, 1536
COL_C_Q, COL_C_K, COL_C_V, COL_C_Z = 1792, 2048, 2176, 2304
COL_D_Q, COL_D_K, COL_D_V, COL_D_Z = 2560, 2816, 3072, 3328
D_IN = 3584

P_H, P_ZA = 0, 256
P_BB, P_G, P_ZB = 512, 768, 1024
P_CQ, P_CQS, P_CK, P_CV, P_ZC = 1280, 1536, 1792, 1920, 2048
P_DQ, P_DK, P_DV, P_ZD = 2304, 2560, 2816, 3072
D_PROJ = 3328
W_AB = P_CQ

LANES = 128
SUBLANES = 8
HALO = 16
VMEM_LIMIT = 56 * 1024 * 1024

TM_PROJ = 1024
T_MIX = 512
T_ATTN = 1024
PROJ_CHUNK = 512


def _silu(x):
    return x * jax.nn.sigmoid(x)


def _params():
    return pltpu.CompilerParams(
        dimension_semantics=("parallel", "parallel"), vmem_limit_bytes=VMEM_LIMIT)


def _in_proj_kernel(x_ref, g_ref, w_ref, cos_ref, sin_ref, o_ref, *, layer):
    x = x_ref[0]
    ms = jnp.mean(x * x, axis=-1, keepdims=True)
    h = (x * lax.rsqrt(ms + EPS) * g_ref[layer:layer + 1, :]).astype(jnp.bfloat16)
    lane = lax.broadcasted_iota(jnp.int32, (1, LANES), 1)
    first_half = (lane % HEAD_DIM) < HALF

    def rope(t):
        partner = jnp.where(first_half, -pltpu.roll(t, LANES - HALF, 1), pltpu.roll(t, HALF, 1))
        return t * cos_ref[...] + partner * sin_ref[...]

    def store(col, t):
        o_ref[0, :, col:col + t.shape[1]] = t.astype(o_ref.dtype)

    def chunk(c0):
        return jnp.dot(h, w_ref[0, :, c0:c0 + PROJ_CHUNK], preferred_element_type=jnp.float32)

    g = D_GROUP
    acc = chunk(COL_A_U)
    store(P_H, acc[:, :g] * jax.nn.sigmoid(acc[:, g:]))
    acc = chunk(COL_A_Z)
    store(P_ZA, _silu(acc[:, :g]))
    store(P_BB, acc[:, g:])
    acc = chunk(COL_B_C)
    store(P_G, acc[:, :g] * acc[:, g:])
    acc = chunk(COL_B_Z)
    store(P_ZB, _silu(acc[:, :g]))
    for s0 in range(0, g, LANES):
        t = rope(acc[:, g + s0:g + s0 + LANES]) * Q_SCALE
        store(P_CQ + s0, t)
        store(P_CQS + s0, pltpu.roll(t, HEAD_DIM, 1))
    acc = chunk(COL_C_K)
    store(P_CK, rope(acc[:, :LANES]))
    store(P_CV, acc[:, LANES:g])
    store(P_ZC, _silu(acc[:, g:]))
    acc = chunk(COL_D_Q)
    store(P_DQ, acc[:, :g] * Q_SCALE)
    store(P_DK, acc[:, g:])
    acc = chunk(COL_D_V)
    store(P_DV, acc[:, :g])
    store(P_ZD, _silu(acc[:, g:]))


def _in_proj(x, g, w, cos, sin, layer):
    b, s, d = x.shape
    tm = TM_PROJ
    return pl.pallas_call(
        functools.partial(_in_proj_kernel, layer=layer),
        grid=(b, s // tm),
        in_specs=[
            pl.BlockSpec((1, tm, d), lambda bi, i: (bi, i, 0)),
            pl.BlockSpec(g.shape, lambda bi, i: (0, 0)),
            pl.BlockSpec((1, d, D_IN), lambda bi, i: (layer, 0, 0)),
            pl.BlockSpec((tm, LANES), lambda bi, i: (i, 0)),
            pl.BlockSpec((tm, LANES), lambda bi, i: (i, 0)),
        ],
        out_specs=pl.BlockSpec((1, tm, D_PROJ), lambda bi, i: (bi, i, 0)),
        out_shape=jax.ShapeDtypeStruct((b, s, D_PROJ), jnp.bfloat16),
        compiler_params=_params(),
        name="in_proj",
    )(x, g, w, cos, sin)


def _conv_mix_kernel(cur_ref, prev_ref, next_ref, aw_ref, ab_ref, lg_ref, lb_ref, bw_ref,
                     o_ref, hbuf, gbuf, hph, gph, *, layer):
    i = pl.program_id(1)
    n = pl.num_programs(1)
    t = T_MIX
    f32 = jnp.float32
    has_prev = (i > 0).astype(f32)
    has_next = (i < n - 1).astype(f32)
    row = lambda ref, j: ref[layer, j:j + 1, :]
    col = lambda ref, rows, c: ref[0, rows, c:c + D_GROUP].astype(f32)

    full = slice(None)
    hwin = jnp.concatenate([col(prev_ref, full, P_H) * has_prev, col(cur_ref, full, P_H),
                            col(next_ref, full, P_H) * has_next], axis=0)
    gwin = jnp.concatenate([col(prev_ref, full, P_G) * has_prev, col(cur_ref, full, P_G),
                            col(next_ref, full, P_G) * has_next], axis=0)
    hbuf[...] = hwin
    gbuf[...] = gwin

    nwin = t + 2 * HALO
    nph = nwin - SUBLANES
    for p in range(1, SUBLANES):
        hph[p - 1] = pltpu.roll(hwin, nwin - p, 0)[0:nph]
    gph[0] = pltpu.roll(gwin, nwin - 1, 0)[0:nph]
    gph[1] = pltpu.roll(gwin, nwin - (SUBLANES - 1), 0)[0:nph]

    def tap(off, rows):
        p = off % SUBLANES
        if p == 0:
            return hbuf[off:off + rows, :]
        return hph[p - 1, off - p:off - p + rows, :]

    rc = 64
    for r0 in range(0, t, rc):
        rows = slice(r0, r0 + rc)
        acc = jnp.broadcast_to(row(ab_ref, 0), (rc, D_GROUP))
        for j in range(CONV_A_WIDTH):
            acc = acc + aw_ref[layer, j:j + 1, :] * tap(HALO - CONV_A_PAD + j + r0, rc)
        mu = jnp.mean(acc, axis=-1, keepdims=True)
        xc = acc - mu
        var = jnp.mean(xc * xc, axis=-1, keepdims=True)
        hn = xc * lax.rsqrt(var + EPS) * row(lg_ref, 0) + row(lb_ref, 0)
        o_ref[0, rows, 0:D_GROUP] = (_silu(hn) * col(cur_ref, rows, P_ZA)).astype(o_ref.dtype)
        gm = HALO + r0 - SUBLANES
        conv = (row(bw_ref, 0) * gph[1, gm:gm + rc, :]
                + row(bw_ref, 1) * gbuf[HALO + r0:HALO + r0 + rc, :]
                + row(bw_ref, 2) * gph[0, HALO + r0:HALO + r0 + rc, :])
        yb = col(cur_ref, rows, P_BB) * conv * col(cur_ref, rows, P_ZB)
        o_ref[0, rows, D_GROUP:2 * D_GROUP] = yb.astype(o_ref.dtype)


def _conv_mix(proj, aw, ab, lg, lb, bw, layer):
    b, s, _ = proj.shape
    t = T_MIX
    hb = t // HALO
    nh = s // HALO
    whole = lambda a: pl.BlockSpec(a.shape, lambda bi, i: (0,) * a.ndim)
    return pl.pallas_call(
        functools.partial(_conv_mix_kernel, layer=layer),
        grid=(b, s // t),
        in_specs=[
            pl.BlockSpec((1, t, W_AB), lambda bi, i: (bi, i, 0)),
            pl.BlockSpec((1, HALO, W_AB), lambda bi, i: (bi, jnp.maximum(i * hb - 1, 0), 0)),
            pl.BlockSpec((1, HALO, W_AB), lambda bi, i: (bi, jnp.minimum((i + 1) * hb, nh - 1), 0)),
            whole(aw), whole(ab), whole(lg), whole(lb), whole(bw),
        ],
        out_specs=pl.BlockSpec((1, t, 2 * D_GROUP), lambda bi, i: (bi, i, 0)),
        out_shape=jax.ShapeDtypeStruct((b, s, 2 * D_GROUP), jnp.bfloat16),
        scratch_shapes=[pltpu.VMEM((t + 2 * HALO, D_GROUP), jnp.float32),
                        pltpu.VMEM((t + 2 * HALO, D_GROUP), jnp.float32),
                        pltpu.VMEM((SUBLANES - 1, t + 2 * HALO - SUBLANES, D_GROUP), jnp.float32),
                        pltpu.VMEM((2, t + 2 * HALO - SUBLANES, D_GROUP), jnp.float32)],
        compiler_params=_params(),
        name="conv_mix",
    )(proj, proj, proj, aw, ab, lg, lb, bw)


def _swa_tile(sink_ref, q_ref, qs_ref, z_ref, kv_ref, y_ref, i, layer):
    s_len = kv_ref.shape[1]
    nkeys = 3 * SWA_BLOCK
    b_ = SWA_BLOCK
    f32 = jnp.float32
    lane = lax.broadcasted_iota(jnp.int32, (1, LANES), 1)
    lo = lane < HEAD_DIM
    zero = jnp.zeros((), jnp.bfloat16)
    ones = jnp.ones((nkeys, LANES), jnp.bfloat16)
    for nb in range(T_ATTN // SWA_BLOCK):
        q0 = i * T_ATTN + nb * SWA_BLOCK
        ws = pl.multiple_of(jnp.clip(q0 - SWA_BLOCK, 0, s_len - nkeys), SWA_BLOCK)
        rows = slice(nb * SWA_BLOCK, (nb + 1) * SWA_BLOCK)
        win = pl.ds(ws, nkeys)
        lhs = jnp.concatenate([
            jnp.where(lo, q_ref[0, rows, 0:LANES], zero),
            jnp.where(lo, qs_ref[0, rows, 0:LANES], zero),
            jnp.where(lo, zero, qs_ref[0, rows, LANES:2 * LANES]),
            jnp.where(lo, zero, q_ref[0, rows, LANES:2 * LANES]),
        ], axis=0)
        kw = kv_ref[0, win, 0:LANES]
        vw = jnp.concatenate([kv_ref[0, win, LANES:2 * LANES], ones], axis=1)
        sc = jnp.einsum("qd,kd->qk", lhs, kw, preferred_element_type=f32)
        qpos = q0 + lax.broadcasted_iota(jnp.int32, (SWA_BLOCK, nkeys), 0)
        kpos = ws + lax.broadcasted_iota(jnp.int32, (SWA_BLOCK, nkeys), 1)
        valid = jnp.abs(kpos - qpos) <= SWA_WINDOW
        num, den = [], []
        for h in range(SWA_HEADS):
            sh = jnp.where(valid, sc[h * b_:(h + 1) * b_], NEG_INF)
            sink = sink_ref[layer, h]
            m = jnp.maximum(jnp.max(sh, axis=-1, keepdims=True), sink)
            e = jnp.exp2(sh - m)
            pv = jnp.dot(e.astype(jnp.bfloat16), vw, preferred_element_type=f32)
            num.append(pv[:, :LANES])
            den.append(pv[:, LANES:] + jnp.exp2(sink - m))
        og0 = jnp.where(lo, num[0], pltpu.roll(num[1], HEAD_DIM, 1)) / jnp.where(lo, den[0], den[1])
        og1 = jnp.where(lo, pltpu.roll(num[2], HEAD_DIM, 1), num[3]) / jnp.where(lo, den[2], den[3])
        y = jnp.concatenate([og0, og1], axis=-1) * z_ref[0, rows, :].astype(f32)
        y_ref[rows, 0:D_GROUP] = y.astype(y_ref.dtype)


def _nbr_tile(q_ref, z_ref, k_ref, v_ref, bias_ref, y_ref, i):
    rows_total = k_ref.shape[1] // GRID_W
    rows_tile = T_ATTN // GRID_W
    nkeys = NA_KH * GRID_W
    f32 = jnp.float32
    lane = lax.broadcasted_iota(jnp.int32, (1, LANES), 1)
    lo = lane < HEAD_DIM
    zero = jnp.zeros((), jnp.bfloat16)
    ones = jnp.ones((nkeys, LANES), jnp.bfloat16)

    for rr in range(rows_tile):
        r = i * rows_tile + rr
        r0 = jnp.clip(r - NA_KH // 2, 0, rows_total - NA_KH)
        d0 = r0 - r + (NA_KH - 1)
        ks = pl.multiple_of(r0 * GRID_W, GRID_W)
        qrow = slice(rr * GRID_W, (rr + 1) * GRID_W)
        outs = []
        for g in range(2):
            cols = slice(g * LANES, (g + 1) * LANES)
            qg = q_ref[0, qrow, cols]
            lhs = jnp.concatenate([jnp.where(lo, qg, zero), jnp.where(lo, zero, qg)], axis=0)
            kw = k_ref[0, pl.ds(ks, nkeys), cols]
            vw = jnp.concatenate([v_ref[0, pl.ds(ks, nkeys), cols], ones], axis=1)
            sc = jnp.einsum("qd,kd->qk", lhs, kw, preferred_element_type=f32)
            bias = jnp.concatenate(
                [jnp.concatenate([bias_ref[0, 2 * g + hh, d0 + 2 * m]
                                  for m in range(NA_KH // 2)], axis=-1)
                 for hh in range(2)], axis=0)
            sc = sc + bias
            m_ = jnp.max(sc, axis=-1, keepdims=True)
            e = jnp.exp2(sc - m_)
            pv = jnp.dot(e.astype(jnp.bfloat16), vw, preferred_element_type=f32)
            pv = jnp.where(jnp.concatenate([lo, lo], axis=1), pv[0:GRID_W], pv[GRID_W:2 * GRID_W])
            outs.append(pv[:, :LANES] / pv[:, LANES:])
        y = jnp.concatenate(outs, axis=-1) * z_ref[0, qrow, :].astype(f32)
        y_ref[qrow, D_GROUP:2 * D_GROUP] = y.astype(y_ref.dtype)


def _attn_out_kernel(sink_ref, cq_ref, cqs_ref, cz_ref, ckv_ref, dq_ref, dz_ref, dk_ref, dv_ref,
                     bias_ref, yab_ref, x_ref, w_ref, g_ref, o_ref, y_ref, *, layer, final_norm):
    i = pl.program_id(1)
    f32 = jnp.float32
    _swa_tile(sink_ref, cq_ref, cqs_ref, cz_ref, ckv_ref, y_ref, i, layer)
    _nbr_tile(dq_ref, dz_ref, dk_ref, dv_ref, bias_ref, y_ref, i)
    acc = jnp.dot(yab_ref[0], w_ref[0, 0:2 * D_GROUP, :], preferred_element_type=f32)
    acc = acc + jnp.dot(y_ref[...], w_ref[0, 2 * D_GROUP:4 * D_GROUP, :], preferred_element_type=f32)
    x = x_ref[0] + acc
    if final_norm:
        ms = jnp.mean(x * x, axis=-1, keepdims=True)
        x = x * lax.rsqrt(ms + EPS) * g_ref[...]
    o_ref[0] = x


def _attn_out(proj, yab, x, sink, bias, w, g, layer, final_norm):
    b, s, d = x.shape
    t = T_ATTN
    tile = lambda col: pl.BlockSpec((1, t, D_GROUP), lambda bi, i, sk: (bi, i, col // D_GROUP))
    seq = lambda col: pl.BlockSpec((1, s, D_GROUP), lambda bi, i, sk: (bi, 0, col // D_GROUP))
    return pl.pallas_call(
        functools.partial(_attn_out_kernel, layer=layer, final_norm=final_norm),
        grid_spec=pltpu.PrefetchScalarGridSpec(
            num_scalar_prefetch=1,
            grid=(b, s // t),
            in_specs=[
                tile(P_CQ), tile(P_CQS), tile(P_ZC), seq(P_CK),
                tile(P_DQ), tile(P_ZD), seq(P_DK), seq(P_DV),
                pl.BlockSpec((1,) + bias.shape[1:], lambda bi, i, sk: (layer, 0, 0, 0, 0)),
                pl.BlockSpec((1, t, 2 * D_GROUP), lambda bi, i, sk: (bi, i, 0)),
                pl.BlockSpec((1, t, d), lambda bi, i, sk: (bi, i, 0)),
                pl.BlockSpec((1,) + w.shape[1:], lambda bi, i, sk: (layer, 0, 0)),
                pl.BlockSpec((1, d), lambda bi, i, sk: (0, 0)),
            ],
            out_specs=pl.BlockSpec((1, t, d), lambda bi, i, sk: (bi, i, 0)),
            scratch_shapes=[pltpu.VMEM((t, 2 * D_GROUP), jnp.bfloat16)],
        ),
        out_shape=jax.ShapeDtypeStruct((b, s, d), jnp.float32),
        compiler_params=_params(),
        name="attn_out",
    )(sink, proj, proj, proj, proj, proj, proj, proj, proj, bias, yab, x, w, g)


def _rope_tables(s):
    inv_freq = ROPE_THETA ** (-jnp.arange(0, HEAD_DIM, 2, dtype=jnp.float32) / HEAD_DIM)
    inv_freq = jnp.tile(inv_freq, LANES // HALF)
    ang = jnp.arange(s, dtype=jnp.float32)[:, None] * inv_freq[None, :]
    return jnp.cos(ang), jnp.sin(ang)


def _nbr_bias_table(rpb):
    c = np.arange(GRID_W)
    c0 = np.clip(c - NA_KW // 2, 0, GRID_W - NA_KW)
    col_ok = (c[None, :] >= c0[:, None]) & (c[None, :] < c0[:, None] + NA_KW)
    dc = np.clip(c[None, :] - c[:, None], -(NA_KW - 1), NA_KW - 1) + (NA_KW - 1)
    onehot = (dc[None] == np.arange(2 * NA_KW - 1)[:, None, None]).astype(np.float32)
    full = jnp.einsum("lhdc,cqk->lhdqk", rpb, jnp.asarray(onehot),
                      precision=lax.Precision.HIGHEST)
    full = jnp.where(jnp.asarray(col_ok), full * LOG2E, NEG_INF)
    pad_rows = lambda a, n: jnp.pad(a, ((0, 0), (0, 0), (0, n), (0, 0), (0, 0)),
                                    constant_values=NEG_INF)
    return jnp.concatenate([pad_rows(full, 1), pad_rows(full[:, :, 1:], 2)], axis=-1)


def kernel(x, norm_g, w_in, w_out, conv_a_w, conv_a_b, ln_a_g, ln_a_b, conv_b_w, swa_sink,
           na_rpb, final_norm_g):
    depth = norm_g.shape[0]
    s = x.shape[1]
    cos_t, sin_t = _rope_tables(s)
    w_in_b = w_in.astype(jnp.bfloat16)
    w_out_b = w_out.astype(jnp.bfloat16)
    bias = _nbr_bias_table(na_rpb)
    sink = swa_sink * LOG2E
    fg = final_norm_g.reshape(1, -1)
    vec = lambda a: a.reshape(depth, 1, -1)
    for l in range(depth):
        proj = _in_proj(x, norm_g, w_in_b, cos_t, sin_t, l)
        yab = _conv_mix(proj, conv_a_w, vec(conv_a_b), vec(ln_a_g), vec(ln_a_b), conv_b_w, l)
        x = _attn_out(proj, yab, x, sink, bias, w_out_b, fg, l, final_norm=(l == depth - 1))
    return x
```

```python
import functools
import math

import numpy as np
import jax
import jax.numpy as jnp
from jax import lax
from jax.experimental import pallas as pl
from jax.experimental.pallas import tpu as pltpu

D_MODEL = 1024
D_GROUP = 256
HEAD_DIM = 64
HALF = HEAD_DIM // 2
GRID_W = 64
CONV_A_WIDTH = 31
CONV_A_PAD = (CONV_A_WIDTH - 1) // 2
CONV_B_WIDTH = 3
SWA_WINDOW = 128
SWA_BLOCK = 128
SWA_HEADS = 4
NA_KH = 8
NA_KW = 16
ROPE_THETA = 10000.0
EPS = 1e-6
NEG_INF = -1e30
LOG2E = math.log2(math.e)
Q_SCALE = HEAD_DIM ** -0.5 * LOG2E

COL_A_U, COL_A_V, COL_A_Z = 0, 256, 512
COL_B_B, COL_B_C, COL_B_X, COL_B_Z = 768, 1024, 1280, 1536
COL_C_Q, COL_C_K, COL_C_V, COL_C_Z = 1792, 2048, 2176, 2304
COL_D_Q, COL_D_K, COL_D_V, COL_D_Z = 2560, 2816, 3072, 3328
D_IN = 3584

P_H, P_ZA = 0, 256
P_BB, P_G, P_ZB = 512, 768, 1024
P_CQ, P_CQS, P_CK, P_CV, P_ZC = 1280, 1536, 1792, 1920, 2048
P_DQ, P_DK, P_DV, P_ZD = 2304, 2560, 2816, 3072
D_PROJ = 3328
W_AB = P_CQ

LANES = 128
SUBLANES = 8
HALO = 16
VMEM_LIMIT = 56 * 1024 * 1024

TM_PROJ = 1024
T_MIX = 1024
T_ATTN = 1024
PROJ_CHUNK = 512
CONV_ROWS = 128


def _silu(x):
    return x * jax.nn.sigmoid(x)


def _params():
    return pltpu.CompilerParams(
        dimension_semantics=("parallel", "parallel"), vmem_limit_bytes=VMEM_LIMIT)


def _in_proj_kernel(x_ref, g_ref, w_ref, cos_ref, sin_ref, o_ref, *, layer):
    x = x_ref[0]
    ms = jnp.mean(x * x, axis=-1, keepdims=True)
    h = (x * lax.rsqrt(ms + EPS) * g_ref[layer:layer + 1, :]).astype(jnp.bfloat16)
    lane = lax.broadcasted_iota(jnp.int32, (1, LANES), 1)
    first_half = (lane % HEAD_DIM) < HALF

    def rope(t):
        partner = jnp.where(first_half, -pltpu.roll(t, LANES - HALF, 1), pltpu.roll(t, HALF, 1))
        return t * cos_ref[...] + partner * sin_ref[...]

    def store(col, t):
        o_ref[0, :, col:col + t.shape[1]] = t.astype(o_ref.dtype)

    def chunk(c0):
        return jnp.dot(h, w_ref[0, :, c0:c0 + PROJ_CHUNK], preferred_element_type=jnp.float32)

    g = D_GROUP
    acc = chunk(COL_A_U)
    store(P_H, acc[:, :g] * jax.nn.sigmoid(acc[:, g:]))
    acc = chunk(COL_A_Z)
    store(P_ZA, _silu(acc[:, :g]))
    store(P_BB, acc[:, g:])
    acc = chunk(COL_B_C)
    store(P_G, acc[:, :g] * acc[:, g:])
    acc = chunk(COL_B_Z)
    store(P_ZB, _silu(acc[:, :g]))
    for s0 in range(0, g, LANES):
        t = rope(acc[:, g + s0:g + s0 + LANES]) * Q_SCALE
        store(P_CQ + s0, t)
        store(P_CQS + s0, pltpu.roll(t, HEAD_DIM, 1))
    acc = chunk(COL_C_K)
    store(P_CK, rope(acc[:, :LANES]))
    store(P_CV, acc[:, LANES:g])
    store(P_ZC, _silu(acc[:, g:]))
    acc = chunk(COL_D_Q)
    store(P_DQ, acc[:, :g] * Q_SCALE)
    store(P_DK, acc[:, g:])
    acc = chunk(COL_D_V)
    store(P_DV, acc[:, :g])
    store(P_ZD, _silu(acc[:, g:]))


def _in_proj(x, g, w, cos, sin, layer):
    b, s, d = x.shape
    tm = TM_PROJ
    return pl.pallas_call(
        functools.partial(_in_proj_kernel, layer=layer),
        grid=(b, s // tm),
        in_specs=[
            pl.BlockSpec((1, tm, d), lambda bi, i: (bi, i, 0)),
            pl.BlockSpec(g.shape, lambda bi, i: (0, 0)),
            pl.BlockSpec((1, d, D_IN), lambda bi, i: (layer, 0, 0)),
            pl.BlockSpec((tm, LANES), lambda bi, i: (i, 0)),
            pl.BlockSpec((tm, LANES), lambda bi, i: (i, 0)),
        ],
        out_specs=pl.BlockSpec((1, tm, D_PROJ), lambda bi, i: (bi, i, 0)),
        out_shape=jax.ShapeDtypeStruct((b, s, D_PROJ), jnp.bfloat16),
        compiler_params=_params(),
        name="in_proj",
    )(x, g, w, cos, sin)


def _conv_mix_kernel(cur_ref, prev_ref, next_ref, aw_ref, ab_ref, lg_ref, lb_ref, bw_ref,
                     o_ref, hbuf, gbuf, hph, gph, *, layer):
    i = pl.program_id(1)
    n = pl.num_programs(1)
    t = T_MIX
    f32 = jnp.float32
    has_prev = (i > 0).astype(f32)
    has_next = (i < n - 1).astype(f32)
    row = lambda ref, j: ref[layer, j:j + 1, :]
    col = lambda ref, rows, c: ref[0, rows, c:c + D_GROUP].astype(f32)

    full = slice(None)
    hwin = jnp.concatenate([col(prev_ref, full, P_H) * has_prev, col(cur_ref, full, P_H),
                            col(next_ref, full, P_H) * has_next], axis=0)
    gwin = jnp.concatenate([col(prev_ref, full, P_G) * has_prev, col(cur_ref, full, P_G),
                            col(next_ref, full, P_G) * has_next], axis=0)
    hbuf[...] = hwin
    gbuf[...] = gwin

    nwin = t + 2 * HALO
    nph = nwin - SUBLANES
    for p in range(1, SUBLANES):
        hph[p - 1] = pltpu.roll(hwin, nwin - p, 0)[0:nph]
    gph[0] = pltpu.roll(gwin, nwin - 1, 0)[0:nph]
    gph[1] = pltpu.roll(gwin, nwin - (SUBLANES - 1), 0)[0:nph]

    def tap(off, rows):
        p = off % SUBLANES
        if p == 0:
            return hbuf[off:off + rows, :]
        return hph[p - 1, off - p:off - p + rows, :]

    rc = CONV_ROWS
    for r0 in range(0, t, rc):
        rows = slice(r0, r0 + rc)
        acc = jnp.broadcast_to(row(ab_ref, 0), (rc, D_GROUP))
        for j in range(CONV_A_WIDTH):
            acc = acc + aw_ref[layer, j:j + 1, :] * tap(HALO - CONV_A_PAD + j + r0, rc)
        mu = jnp.mean(acc, axis=-1, keepdims=True)
        xc = acc - mu
        var = jnp.mean(xc * xc, axis=-1, keepdims=True)
        hn = xc * lax.rsqrt(var + EPS) * row(lg_ref, 0) + row(lb_ref, 0)
        o_ref[0, rows, 0:D_GROUP] = (_silu(hn) * col(cur_ref, rows, P_ZA)).astype(o_ref.dtype)
        gm = HALO + r0 - SUBLANES
        conv = (row(bw_ref, 0) * gph[1, gm:gm + rc, :]
                + row(bw_ref, 1) * gbuf[HALO + r0:HALO + r0 + rc, :]
                + row(bw_ref, 2) * gph[0, HALO + r0:HALO + r0 + rc, :])
        yb = col(cur_ref, rows, P_BB) * conv * col(cur_ref, rows, P_ZB)
        o_ref[0, rows, D_GROUP:2 * D_GROUP] = yb.astype(o_ref.dtype)


def _conv_mix(proj, aw, ab, lg, lb, bw, layer):
    b, s, _ = proj.shape
    t = T_MIX
    hb = t // HALO
    nh = s // HALO
    whole = lambda a: pl.BlockSpec(a.shape, lambda bi, i: (0,) * a.ndim)
    return pl.pallas_call(
        functools.partial(_conv_mix_kernel, layer=layer),
        grid=(b, s // t),
        in_specs=[
            pl.BlockSpec((1, t, W_AB), lambda bi, i: (bi, i, 0)),
            pl.BlockSpec((1, HALO, W_AB), lambda bi, i: (bi, jnp.maximum(i * hb - 1, 0), 0)),
            pl.BlockSpec((1, HALO, W_AB), lambda bi, i: (bi, jnp.minimum((i + 1) * hb, nh - 1), 0)),
            whole(aw), whole(ab), whole(lg), whole(lb), whole(bw),
        ],
        out_specs=pl.BlockSpec((1, t, 2 * D_GROUP), lambda bi, i: (bi, i, 0)),
        out_shape=jax.ShapeDtypeStruct((b, s, 2 * D_GROUP), jnp.bfloat16),
        scratch_shapes=[pltpu.VMEM((t + 2 * HALO, D_GROUP), jnp.float32),
                        pltpu.VMEM((t + 2 * HALO, D_GROUP), jnp.float32),
                        pltpu.VMEM((SUBLANES - 1, t + 2 * HALO - SUBLANES, D_GROUP), jnp.float32),
                        pltpu.VMEM((2, t + 2 * HALO - SUBLANES, D_GROUP), jnp.float32)],
        compiler_params=_params(),
        name="conv_mix",
    )(proj, proj, proj, aw, ab, lg, lb, bw)


def _swa_tile(sink_ref, q_ref, qs_ref, z_ref, kv_ref, y_ref, i, layer):
    s_len = kv_ref.shape[1]
    nkeys = 3 * SWA_BLOCK
    b_ = SWA_BLOCK
    f32 = jnp.float32
    lane = lax.broadcasted_iota(jnp.int32, (1, LANES), 1)
    lo = lane < HEAD_DIM
    zero = jnp.zeros((), jnp.bfloat16)
    ones = jnp.ones((nkeys, LANES), jnp.bfloat16)
    for nb in range(T_ATTN // SWA_BLOCK):
        q0 = i * T_ATTN + nb * SWA_BLOCK
        ws = pl.multiple_of(jnp.clip(q0 - SWA_BLOCK, 0, s_len - nkeys), SWA_BLOCK)
        rows = slice(nb * SWA_BLOCK, (nb + 1) * SWA_BLOCK)
        win = pl.ds(ws, nkeys)
        lhs = jnp.concatenate([
            jnp.where(lo, q_ref[0, rows, 0:LANES], zero),
            jnp.where(lo, qs_ref[0, rows, 0:LANES], zero),
            jnp.where(lo, zero, qs_ref[0, rows, LANES:2 * LANES]),
            jnp.where(lo, zero, q_ref[0, rows, LANES:2 * LANES]),
        ], axis=0)
        kw = kv_ref[0, win, 0:LANES]
        vw = jnp.concatenate([kv_ref[0, win, LANES:2 * LANES], ones], axis=1)
        sc = jnp.einsum("qd,kd->qk", lhs, kw, preferred_element_type=f32)
        qpos = q0 + lax.broadcasted_iota(jnp.int32, (SWA_BLOCK, nkeys), 0)
        kpos = ws + lax.broadcasted_iota(jnp.int32, (SWA_BLOCK, nkeys), 1)
        valid = jnp.abs(kpos - qpos) <= SWA_WINDOW
        num, den = [], []
        for h in range(SWA_HEADS):
            sh = jnp.where(valid, sc[h * b_:(h + 1) * b_], NEG_INF)
            sink = sink_ref[layer, h]
            m = jnp.maximum(jnp.max(sh, axis=-1, keepdims=True), sink)
            e = jnp.exp2(sh - m)
            pv = jnp.dot(e.astype(jnp.bfloat16), vw, preferred_element_type=f32)
            num.append(pv[:, :LANES])
            den.append(pv[:, LANES:] + jnp.exp2(sink - m))
        og0 = jnp.where(lo, num[0], pltpu.roll(num[1], HEAD_DIM, 1)) / jnp.where(lo, den[0], den[1])
        og1 = jnp.where(lo, pltpu.roll(num[2], HEAD_DIM, 1), num[3]) / jnp.where(lo, den[2], den[3])
        y = jnp.concatenate([og0, og1], axis=-1) * z_ref[0, rows, :].astype(f32)
        y_ref[rows, 0:D_GROUP] = y.astype(y_ref.dtype)


def _nbr_tile(q_ref, z_ref, k_ref, v_ref, bias_ref, y_ref, i):
    rows_total = k_ref.shape[1] // GRID_W
    rows_tile = T_ATTN // GRID_W
    nkeys = NA_KH * GRID_W
    f32 = jnp.float32
    lane = lax.broadcasted_iota(jnp.int32, (1, LANES), 1)
    lo = lane < HEAD_DIM
    zero = jnp.zeros((), jnp.bfloat16)
    ones = jnp.ones((nkeys, LANES), jnp.bfloat16)

    for rr in range(rows_tile):
        r = i * rows_tile + rr
        r0 = jnp.clip(r - NA_KH // 2, 0, rows_total - NA_KH)
        d0 = r0 - r + (NA_KH - 1)
        ks = pl.multiple_of(r0 * GRID_W, GRID_W)
        qrow = slice(rr * GRID_W, (rr + 1) * GRID_W)
        outs = []
        for g in range(2):
            cols = slice(g * LANES, (g + 1) * LANES)
            qg = q_ref[0, qrow, cols]
            lhs = jnp.concatenate([jnp.where(lo, qg, zero), jnp.where(lo, zero, qg)], axis=0)
            kw = k_ref[0, pl.ds(ks, nkeys), cols]
            vw = jnp.concatenate([v_ref[0, pl.ds(ks, nkeys), cols], ones], axis=1)
            sc = jnp.einsum("qd,kd->qk", lhs, kw, preferred_element_type=f32)
            bias = jnp.concatenate(
                [jnp.concatenate([bias_ref[0, 2 * g + hh, d0 + 2 * m]
                                  for m in range(NA_KH // 2)], axis=-1)
                 for hh in range(2)], axis=0)
            sc = sc + bias
            m_ = jnp.max(sc, axis=-1, keepdims=True)
            e = jnp.exp2(sc - m_)
            pv = jnp.dot(e.astype(jnp.bfloat16), vw, preferred_element_type=f32)
            pv = jnp.where(jnp.concatenate([lo, lo], axis=1), pv[0:GRID_W], pv[GRID_W:2 * GRID_W])
            outs.append(pv[:, :LANES] / pv[:, LANES:])
        y = jnp.concatenate(outs, axis=-1) * z_ref[0, qrow, :].astype(f32)
        y_ref[qrow, D_GROUP:2 * D_GROUP] = y.astype(y_ref.dtype)


def _attn_out_kernel(sink_ref, cq_ref, cqs_ref, cz_ref, ckv_ref, dq_ref, dz_ref, dk_ref, dv_ref,
                     bias_ref, yab_ref, x_ref, w_ref, g_ref, o_ref, y_ref, *, layer, final_norm):
    i = pl.program_id(1)
    f32 = jnp.float32
    _swa_tile(sink_ref, cq_ref, cqs_ref, cz_ref, ckv_ref, y_ref, i, layer)
    _nbr_tile(dq_ref, dz_ref, dk_ref, dv_ref, bias_ref, y_ref, i)
    acc = jnp.dot(yab_ref[0], w_ref[0, 0:2 * D_GROUP, :], preferred_element_type=f32)
    acc = acc + jnp.dot(y_ref[...], w_ref[0, 2 * D_GROUP:4 * D_GROUP, :], preferred_element_type=f32)
    x = x_ref[0] + acc
    if final_norm:
        ms = jnp.mean(x * x, axis=-1, keepdims=True)
        x = x * lax.rsqrt(ms + EPS) * g_ref[...]
    o_ref[0] = x


def _attn_out(proj, yab, x, sink, bias, w, g, layer, final_norm):
    b, s, d = x.shape
    t = T_ATTN
    tile = lambda col: pl.BlockSpec((1, t, D_GROUP), lambda bi, i, sk: (bi, i, col // D_GROUP))
    seq = lambda col: pl.BlockSpec((1, s, D_GROUP), lambda bi, i, sk: (bi, 0, col // D_GROUP))
    return pl.pallas_call(
        functools.partial(_attn_out_kernel, layer=layer, final_norm=final_norm),
        grid_spec=pltpu.PrefetchScalarGridSpec(
            num_scalar_prefetch=1,
            grid=(b, s // t),
            in_specs=[
                tile(P_CQ), tile(P_CQS), tile(P_ZC), seq(P_CK),
                tile(P_DQ), tile(P_ZD), seq(P_DK), seq(P_DV),
                pl.BlockSpec((1,) + bias.shape[1:], lambda bi, i, sk: (layer, 0, 0, 0, 0)),
                pl.BlockSpec((1, t, 2 * D_GROUP), lambda bi, i, sk: (bi, i, 0)),
                pl.BlockSpec((1, t, d), lambda bi, i, sk: (bi, i, 0)),
                pl.BlockSpec((1,) + w.shape[1:], lambda bi, i, sk: (layer, 0, 0)),
                pl.BlockSpec((1, d), lambda bi, i, sk: (0, 0)),
            ],
            out_specs=pl.BlockSpec((1, t, d), lambda bi, i, sk: (bi, i, 0)),
            scratch_shapes=[pltpu.VMEM((t, 2 * D_GROUP), jnp.bfloat16)],
        ),
        out_shape=jax.ShapeDtypeStruct((b, s, d), jnp.float32),
        compiler_params=_params(),
        name="attn_out",
    )(sink, proj, proj, proj, proj, proj, proj, proj, proj, bias, yab, x, w, g)


def _rope_tables(s):
    inv_freq = ROPE_THETA ** (-jnp.arange(0, HEAD_DIM, 2, dtype=jnp.float32) / HEAD_DIM)
    inv_freq = jnp.tile(inv_freq, LANES // HALF)
    ang = jnp.arange(s, dtype=jnp.float32)[:, None] * inv_freq[None, :]
    return jnp.cos(ang), jnp.sin(ang)


def _nbr_bias_table(rpb):
    c = np.arange(GRID_W)
    c0 = np.clip(c - NA_KW // 2, 0, GRID_W - NA_KW)
    col_ok = (c[None, :] >= c0[:, None]) & (c[None, :] < c0[:, None] + NA_KW)
    dc = np.clip(c[None, :] - c[:, None], -(NA_KW - 1), NA_KW - 1) + (NA_KW - 1)
    onehot = (dc[None] == np.arange(2 * NA_KW - 1)[:, None, None]).astype(np.float32)
    full = jnp.einsum("lhdc,cqk->lhdqk", rpb, jnp.asarray(onehot),
                      precision=lax.Precision.HIGHEST)
    full = jnp.where(jnp.asarray(col_ok), full * LOG2E, NEG_INF)
    pad_rows = lambda a, n: jnp.pad(a, ((0, 0), (0, 0), (0, n), (0, 0), (0, 0)),
                                    constant_values=NEG_INF)
    return jnp.concatenate([pad_rows(full, 1), pad_rows(full[:, :, 1:], 2)], axis=-1)


def kernel(x, norm_g, w_in, w_out, conv_a_w, conv_a_b, ln_a_g, ln_a_b, conv_b_w, swa_sink,
           na_rpb, final_norm_g):
    depth = norm_g.shape[0]
    s = x.shape[1]
    cos_t, sin_t = _rope_tables(s)
    w_in_b = w_in.astype(jnp.bfloat16)
    w_out_b = w_out.astype(jnp.bfloat16)
    bias = _nbr_bias_table(na_rpb)
    sink = swa_sink * LOG2E
    fg = final_norm_g.reshape(1, -1)
    vec = lambda a: a.reshape(depth, 1, -1)
    for l in range(depth):
        proj = _in_proj(x, norm_g, w_in_b, cos_t, sin_t, l)
        yab = _conv_mix(proj, conv_a_w, vec(conv_a_b), vec(ln_a_g), vec(ln_a_b), conv_b_w, l)
        x = _attn_out(proj, yab, x, sink, bias, w_out_b, fg, l, final_norm=(l == depth - 1))
    return x
```

```python
import functools
import math

import numpy as np
import jax
import jax.numpy as jnp
from jax import lax
from jax.experimental import pallas as pl
from jax.experimental.pallas import tpu as pltpu

D_MODEL = 1024
D_GROUP = 256
HEAD_DIM = 64
HALF = HEAD_DIM // 2
GRID_W = 64
CONV_A_WIDTH = 31
CONV_A_PAD = (CONV_A_WIDTH - 1) // 2
CONV_B_WIDTH = 3
SWA_WINDOW = 128
SWA_BLOCK = 128
SWA_HEADS = 4
NA_KH = 8
NA_KW = 16
ROPE_THETA = 10000.0
EPS = 1e-6
NEG_INF = -1e30
LOG2E = math.log2(math.e)
Q_SCALE = HEAD_DIM ** -0.5 * LOG2E

COL_A_U, COL_A_V, COL_A_Z = 0, 256, 512
COL_B_B, COL_B_C, COL_B_X, COL_B_Z = 768, 1024, 1280, 1536
COL_C_Q, COL_C_K, COL_C_V, COL_C_Z = 1792, 2048, 2176, 2304
COL_D_Q, COL_D_K, COL_D_V, COL_D_Z = 2560, 2816, 3072, 3328
D_IN = 3584

P_H, P_ZA = 0, 256
P_BB, P_G, P_ZB = 512, 768, 1024
P_CQ, P_CQS, P_CK, P_CV, P_ZC = 1280, 1536, 1792, 1920, 2048
P_DQ, P_DK, P_DV, P_ZD = 2304, 2560, 2816, 3072
D_PROJ = 3328
W_AB = P_CQ

LANES = 128
SUBLANES = 8
HALO = 16
VMEM_LIMIT = 56 * 1024 * 1024

TM_PROJ = 1024
T_MIX = 1024
T_ATTN = 1024
PROJ_CHUNK = 512
CONV_ROWS = 256


def _silu(x):
    return x * jax.nn.sigmoid(x)


def _params():
    return pltpu.CompilerParams(
        dimension_semantics=("parallel", "parallel"), vmem_limit_bytes=VMEM_LIMIT)


def _in_proj_kernel(x_ref, g_ref, w_ref, cos_ref, sin_ref, o_ref, *, layer):
    x = x_ref[0]
    ms = jnp.mean(x * x, axis=-1, keepdims=True)
    h = (x * lax.rsqrt(ms + EPS) * g_ref[layer:layer + 1, :]).astype(jnp.bfloat16)
    lane = lax.broadcasted_iota(jnp.int32, (1, LANES), 1)
    first_half = (lane % HEAD_DIM) < HALF

    def rope(t):
        partner = jnp.where(first_half, -pltpu.roll(t, LANES - HALF, 1), pltpu.roll(t, HALF, 1))
        return t * cos_ref[...] + partner * sin_ref[...]

    def store(col, t):
        o_ref[0, :, col:col + t.shape[1]] = t.astype(o_ref.dtype)

    def chunk(c0):
        return jnp.dot(h, w_ref[0, :, c0:c0 + PROJ_CHUNK], preferred_element_type=jnp.float32)

    g = D_GROUP
    acc = chunk(COL_A_U)
    store(P_H, acc[:, :g] * jax.nn.sigmoid(acc[:, g:]))
    acc = chunk(COL_A_Z)
    store(P_ZA, _silu(acc[:, :g]))
    store(P_BB, acc[:, g:])
    acc = chunk(COL_B_C)
    store(P_G, acc[:, :g] * acc[:, g:])
    acc = chunk(COL_B_Z)
    store(P_ZB, _silu(acc[:, :g]))
    for s0 in range(0, g, LANES):
        t = rope(acc[:, g + s0:g + s0 + LANES]) * Q_SCALE
        store(P_CQ + s0, t)
        store(P_CQS + s0, pltpu.roll(t, HEAD_DIM, 1))
    acc = chunk(COL_C_K)
    store(P_CK, rope(acc[:, :LANES]))
    store(P_CV, acc[:, LANES:g])
    store(P_ZC, _silu(acc[:, g:]))
    acc = chunk(COL_D_Q)
    store(P_DQ, acc[:, :g] * Q_SCALE)
    store(P_DK, acc[:, g:])
    acc = chunk(COL_D_V)
    store(P_DV, acc[:, :g])
    store(P_ZD, _silu(acc[:, g:]))


def _in_proj(x, g, w, cos, sin, layer):
    b, s, d = x.shape
    tm = TM_PROJ
    return pl.pallas_call(
        functools.partial(_in_proj_kernel, layer=layer),
        grid=(b, s // tm),
        in_specs=[
            pl.BlockSpec((1, tm, d), lambda bi, i: (bi, i, 0)),
            pl.BlockSpec(g.shape, lambda bi, i: (0, 0)),
            pl.BlockSpec((1, d, D_IN), lambda bi, i: (layer, 0, 0)),
            pl.BlockSpec((tm, LANES), lambda bi, i: (i, 0)),
            pl.BlockSpec((tm, LANES), lambda bi, i: (i, 0)),
        ],
        out_specs=pl.BlockSpec((1, tm, D_PROJ), lambda bi, i: (bi, i, 0)),
        out_shape=jax.ShapeDtypeStruct((b, s, D_PROJ), jnp.bfloat16),
        compiler_params=_params(),
        name="in_proj",
    )(x, g, w, cos, sin)


def _conv_out_kernel(cur_ref, prev_ref, next_ref, aw_ref, ab_ref, lg_ref, lb_ref, bw_ref,
                     ycd_ref, x_ref, w_ref, g_ref, o_ref, hbuf, gbuf, hph, gph,
                     *, layer, final_norm):
    i = pl.program_id(1)
    n = pl.num_programs(1)
    t = T_MIX
    f32 = jnp.float32
    has_prev = (i > 0).astype(f32)
    has_next = (i < n - 1).astype(f32)
    row = lambda ref, j: ref[layer, j:j + 1, :]
    col = lambda ref, rows, c: ref[0, rows, c:c + D_GROUP].astype(f32)

    full = slice(None)
    hwin = jnp.concatenate([col(prev_ref, full, P_H) * has_prev, col(cur_ref, full, P_H),
                            col(next_ref, full, P_H) * has_next], axis=0)
    gwin = jnp.concatenate([col(prev_ref, full, P_G) * has_prev, col(cur_ref, full, P_G),
                            col(next_ref, full, P_G) * has_next], axis=0)
    hbuf[...] = hwin
    gbuf[...] = gwin

    nwin = t + 2 * HALO
    nph = nwin - SUBLANES
    for p in range(1, SUBLANES):
        hph[p - 1] = pltpu.roll(hwin, nwin - p, 0)[0:nph]
    gph[0] = pltpu.roll(gwin, nwin - 1, 0)[0:nph]
    gph[1] = pltpu.roll(gwin, nwin - (SUBLANES - 1), 0)[0:nph]

    def tap(off, rows):
        p = off % SUBLANES
        if p == 0:
            return hbuf[off:off + rows, :]
        return hph[p - 1, off - p:off - p + rows, :]

    rc = CONV_ROWS
    for r0 in range(0, t, rc):
        rows = slice(r0, r0 + rc)
        acc = jnp.broadcast_to(row(ab_ref, 0), (rc, D_GROUP))
        for j in range(CONV_A_WIDTH):
            acc = acc + aw_ref[layer, j:j + 1, :] * tap(HALO - CONV_A_PAD + j + r0, rc)
        mu = jnp.mean(acc, axis=-1, keepdims=True)
        xc = acc - mu
        var = jnp.mean(xc * xc, axis=-1, keepdims=True)
        hn = xc * lax.rsqrt(var + EPS) * row(lg_ref, 0) + row(lb_ref, 0)
        ya = _silu(hn) * col(cur_ref, rows, P_ZA)
        gm = HALO + r0 - SUBLANES
        conv = (row(bw_ref, 0) * gph[1, gm:gm + rc, :]
                + row(bw_ref, 1) * gbuf[HALO + r0:HALO + r0 + rc, :]
                + row(bw_ref, 2) * gph[0, HALO + r0:HALO + r0 + rc, :])
        yb = col(cur_ref, rows, P_BB) * conv * col(cur_ref, rows, P_ZB)
        yab = jnp.concatenate([ya.astype(jnp.bfloat16), yb.astype(jnp.bfloat16)], axis=1)
        xo = x_ref[0, rows, :] + jnp.dot(ycd_ref[0, rows, :], w_ref[0, 2 * D_GROUP:4 * D_GROUP, :],
                                         preferred_element_type=f32)
        xo = xo + jnp.dot(yab, w_ref[0, 0:2 * D_GROUP, :], preferred_element_type=f32)
        if final_norm:
            ms = jnp.mean(xo * xo, axis=-1, keepdims=True)
            xo = xo * lax.rsqrt(ms + EPS) * g_ref[...]
        o_ref[0, rows, :] = xo


def _conv_out(proj, ycd, x, aw, ab, lg, lb, bw, w, g, layer, final_norm):
    b, s, d = x.shape
    t = T_MIX
    hb = t // HALO
    nh = s // HALO
    f32 = jnp.float32
    whole = lambda a: pl.BlockSpec(a.shape, lambda bi, i: (0,) * a.ndim)
    return pl.pallas_call(
        functools.partial(_conv_out_kernel, layer=layer, final_norm=final_norm),
        grid=(b, s // t),
        in_specs=[
            pl.BlockSpec((1, t, W_AB), lambda bi, i: (bi, i, 0)),
            pl.BlockSpec((1, HALO, W_AB), lambda bi, i: (bi, jnp.maximum(i * hb - 1, 0), 0)),
            pl.BlockSpec((1, HALO, W_AB), lambda bi, i: (bi, jnp.minimum((i + 1) * hb, nh - 1), 0)),
            whole(aw), whole(ab), whole(lg), whole(lb), whole(bw),
            pl.BlockSpec((1, t, 2 * D_GROUP), lambda bi, i: (bi, i, 0)),
            pl.BlockSpec((1, t, d), lambda bi, i: (bi, i, 0)),
            pl.BlockSpec((1,) + w.shape[1:], lambda bi, i: (layer, 0, 0)),
            pl.BlockSpec((1, d), lambda bi, i: (0, 0)),
        ],
        out_specs=pl.BlockSpec((1, t, d), lambda bi, i: (bi, i, 0)),
        out_shape=jax.ShapeDtypeStruct((b, s, d), f32),
        scratch_shapes=[pltpu.VMEM((t + 2 * HALO, D_GROUP), f32),
                        pltpu.VMEM((t + 2 * HALO, D_GROUP), f32),
                        pltpu.VMEM((SUBLANES - 1, t + 2 * HALO - SUBLANES, D_GROUP), f32),
                        pltpu.VMEM((2, t + 2 * HALO - SUBLANES, D_GROUP), f32)],
        compiler_params=_params(),
        name="conv_out",
    )(proj, proj, proj, aw, ab, lg, lb, bw, ycd, x, w, g)


def _swa_tile(sink_ref, q_ref, qs_ref, z_ref, kv_ref, y_ref, i, layer):
    s_len = kv_ref.shape[1]
    nkeys = 3 * SWA_BLOCK
    b_ = SWA_BLOCK
    f32 = jnp.float32
    lane = lax.broadcasted_iota(jnp.int32, (1, LANES), 1)
    lo = lane < HEAD_DIM
    zero = jnp.zeros((), jnp.bfloat16)
    ones = jnp.ones((nkeys, LANES), jnp.bfloat16)
    for nb in range(T_ATTN // SWA_BLOCK):
        q0 = i * T_ATTN + nb * SWA_BLOCK
        ws = pl.multiple_of(jnp.clip(q0 - SWA_BLOCK, 0, s_len - nkeys), SWA_BLOCK)
        rows = slice(nb * SWA_BLOCK, (nb + 1) * SWA_BLOCK)
        win = pl.ds(ws, nkeys)
        lhs = jnp.concatenate([
            jnp.where(lo, q_ref[0, rows, 0:LANES], zero),
            jnp.where(lo, qs_ref[0, rows, 0:LANES], zero),
            jnp.where(lo, zero, qs_ref[0, rows, LANES:2 * LANES]),
            jnp.where(lo, zero, q_ref[0, rows, LANES:2 * LANES]),
        ], axis=0)
        kw = kv_ref[0, win, 0:LANES]
        vw = jnp.concatenate([kv_ref[0, win, LANES:2 * LANES], ones], axis=1)
        sc = jnp.einsum("qd,kd->qk", lhs, kw, preferred_element_type=f32)
        qpos = q0 + lax.broadcasted_iota(jnp.int32, (SWA_BLOCK, nkeys), 0)
        kpos = ws + lax.broadcasted_iota(jnp.int32, (SWA_BLOCK, nkeys), 1)
        valid = jnp.abs(kpos - qpos) <= SWA_WINDOW
        num, den = [], []
        for h in range(SWA_HEADS):
            sh = jnp.where(valid, sc[h * b_:(h + 1) * b_], NEG_INF)
            sink = sink_ref[layer, h]
            m = jnp.maximum(jnp.max(sh, axis=-1, keepdims=True), sink)
            e = jnp.exp2(sh - m)
            pv = jnp.dot(e.astype(jnp.bfloat16), vw, preferred_element_type=f32)
            num.append(pv[:, :LANES])
            den.append(pv[:, LANES:] + jnp.exp2(sink - m))
        og0 = jnp.where(lo, num[0], pltpu.roll(num[1], HEAD_DIM, 1)) / jnp.where(lo, den[0], den[1])
        og1 = jnp.where(lo, pltpu.roll(num[2], HEAD_DIM, 1), num[3]) / jnp.where(lo, den[2], den[3])
        y = jnp.concatenate([og0, og1], axis=-1) * z_ref[0, rows, :].astype(f32)
        y_ref[rows, 0:D_GROUP] = y.astype(y_ref.dtype)


def _nbr_tile(q_ref, z_ref, k_ref, v_ref, bias_ref, y_ref, i):
    rows_total = k_ref.shape[1] // GRID_W
    rows_tile = T_ATTN // GRID_W
    nkeys = NA_KH * GRID_W
    f32 = jnp.float32
    lane = lax.broadcasted_iota(jnp.int32, (1, LANES), 1)
    lo = lane < HEAD_DIM
    zero = jnp.zeros((), jnp.bfloat16)
    ones = jnp.ones((nkeys, LANES), jnp.bfloat16)

    for rr in range(rows_tile):
        r = i * rows_tile + rr
        r0 = jnp.clip(r - NA_KH // 2, 0, rows_total - NA_KH)
        d0 = r0 - r + (NA_KH - 1)
        ks = pl.multiple_of(r0 * GRID_W, GRID_W)
        qrow = slice(rr * GRID_W, (rr + 1) * GRID_W)
        outs = []
        for g in range(2):
            cols = slice(g * LANES, (g + 1) * LANES)
            qg = q_ref[0, qrow, cols]
            lhs = jnp.concatenate([jnp.where(lo, qg, zero), jnp.where(lo, zero, qg)], axis=0)
            kw = k_ref[0, pl.ds(ks, nkeys), cols]
            vw = jnp.concatenate([v_ref[0, pl.ds(ks, nkeys), cols], ones], axis=1)
            sc = jnp.einsum("qd,kd->qk", lhs, kw, preferred_element_type=f32)
            bias = jnp.concatenate(
                [jnp.concatenate([bias_ref[0, 2 * g + hh, d0 + 2 * m]
                                  for m in range(NA_KH // 2)], axis=-1)
                 for hh in range(2)], axis=0)
            sc = sc + bias
            m_ = jnp.max(sc, axis=-1, keepdims=True)
            e = jnp.exp2(sc - m_)
            pv = jnp.dot(e.astype(jnp.bfloat16), vw, preferred_element_type=f32)
            pv = jnp.where(jnp.concatenate([lo, lo], axis=1), pv[0:GRID_W], pv[GRID_W:2 * GRID_W])
            outs.append(pv[:, :LANES] / pv[:, LANES:])
        y = jnp.concatenate(outs, axis=-1) * z_ref[0, qrow, :].astype(f32)
        y_ref[qrow, D_GROUP:2 * D_GROUP] = y.astype(y_ref.dtype)


def _attn_kernel(sink_ref, cq_ref, cqs_ref, cz_ref, ckv_ref, dq_ref, dz_ref, dk_ref, dv_ref,
                 bias_ref, o_ref, *, layer):
    i = pl.program_id(1)
    y_ref = o_ref.at[0]
    _swa_tile(sink_ref, cq_ref, cqs_ref, cz_ref, ckv_ref, y_ref, i, layer)
    _nbr_tile(dq_ref, dz_ref, dk_ref, dv_ref, bias_ref, y_ref, i)


def _attn(proj, sink, bias, layer):
    b, s, _ = proj.shape
    t = T_ATTN
    tile = lambda col: pl.BlockSpec((1, t, D_GROUP), lambda bi, i, sk: (bi, i, col // D_GROUP))
    seq = lambda col: pl.BlockSpec((1, s, D_GROUP), lambda bi, i, sk: (bi, 0, col // D_GROUP))
    return pl.pallas_call(
        functools.partial(_attn_kernel, layer=layer),
        grid_spec=pltpu.PrefetchScalarGridSpec(
            num_scalar_prefetch=1,
            grid=(b, s // t),
            in_specs=[
                tile(P_CQ), tile(P_CQS), tile(P_ZC), seq(P_CK),
                tile(P_DQ), tile(P_ZD), seq(P_DK), seq(P_DV),
                pl.BlockSpec((1,) + bias.shape[1:], lambda bi, i, sk: (layer, 0, 0, 0, 0)),
            ],
            out_specs=pl.BlockSpec((1, t, 2 * D_GROUP), lambda bi, i, sk: (bi, i, 0)),
        ),
        out_shape=jax.ShapeDtypeStruct((b, s, 2 * D_GROUP), jnp.bfloat16),
        compiler_params=_params(),
        name="attn",
    )(sink, proj, proj, proj, proj, proj, proj, proj, proj, bias)


def _rope_tables(s):
    inv_freq = ROPE_THETA ** (-jnp.arange(0, HEAD_DIM, 2, dtype=jnp.float32) / HEAD_DIM)
    inv_freq = jnp.tile(inv_freq, LANES // HALF)
    ang = jnp.arange(s, dtype=jnp.float32)[:, None] * inv_freq[None, :]
    return jnp.cos(ang), jnp.sin(ang)


def _nbr_bias_table(rpb):
    c = np.arange(GRID_W)
    c0 = np.clip(c - NA_KW // 2, 0, GRID_W - NA_KW)
    col_ok = (c[None, :] >= c0[:, None]) & (c[None, :] < c0[:, None] + NA_KW)
    dc = np.clip(c[None, :] - c[:, None], -(NA_KW - 1), NA_KW - 1) + (NA_KW - 1)
    onehot = (dc[None] == np.arange(2 * NA_KW - 1)[:, None, None]).astype(np.float32)
    full = jnp.einsum("lhdc,cqk->lhdqk", rpb, jnp.asarray(onehot),
                      precision=lax.Precision.HIGHEST)
    full = jnp.where(jnp.asarray(col_ok), full * LOG2E, NEG_INF)
    pad_rows = lambda a, n: jnp.pad(a, ((0, 0), (0, 0), (0, n), (0, 0), (0, 0)),
                                    constant_values=NEG_INF)
    return jnp.concatenate([pad_rows(full, 1), pad_rows(full[:, :, 1:], 2)], axis=-1)


def kernel(x, norm_g, w_in, w_out, conv_a_w, conv_a_b, ln_a_g, ln_a_b, conv_b_w, swa_sink,
           na_rpb, final_norm_g):
    depth = norm_g.shape[0]
    s = x.shape[1]
    cos_t, sin_t = _rope_tables(s)
    w_in_b = w_in.astype(jnp.bfloat16)
    w_out_b = w_out.astype(jnp.bfloat16)
    bias = _nbr_bias_table(na_rpb)
    sink = swa_sink * LOG2E
    fg = final_norm_g.reshape(1, -1)
    vec = lambda a: a.reshape(depth, 1, -1)
    for l in range(depth):
        proj = _in_proj(x, norm_g, w_in_b, cos_t, sin_t, l)
        ycd = _attn(proj, sink, bias, l)
        x = _conv_out(proj, ycd, x, conv_a_w, vec(conv_a_b), vec(ln_a_g), vec(ln_a_b), conv_b_w,
                      w_out_b, fg, l, final_norm=(l == depth - 1))
    return x
```

```python
import functools
import math

import numpy as np
import jax
import jax.numpy as jnp
from jax import lax
from jax.experimental import pallas as pl
from jax.experimental.pallas import tpu as pltpu

D_MODEL = 1024
D_GROUP = 256
HEAD_DIM = 64
HALF = HEAD_DIM // 2
GRID_W = 64
CONV_A_WIDTH = 31
CONV_A_PAD = (CONV_A_WIDTH - 1) // 2
CONV_B_WIDTH = 3
SWA_WINDOW = 128
SWA_BLOCK = 128
SWA_HEADS = 4
NA_KH = 8
NA_KW = 16
ROPE_THETA = 10000.0
EPS = 1e-6
NEG_INF = -1e30
LOG2E = math.log2(math.e)
Q_SCALE = HEAD_DIM ** -0.5 * LOG2E

COL_A_U, COL_A_V, COL_A_Z = 0, 256, 512
COL_B_B, COL_B_C, COL_B_X, COL_B_Z = 768, 1024, 1280, 1536
COL_C_Q, COL_C_K, COL_C_V, COL_C_Z = 1792, 2048, 2176, 2304
COL_D_Q, COL_D_K, COL_D_V, COL_D_Z = 2560, 2816, 3072, 3328
D_IN = 3584

P_H, P_ZA = 0, 256
P_BB, P_G, P_ZB = 512, 768, 1024
P_CQ, P_CQS, P_CK, P_CV, P_ZC = 1280, 1536, 1792, 1920, 2048
P_DQ, P_DK, P_DV, P_ZD = 2304, 2560, 2816, 3072
D_PROJ = 3328
W_AB = P_CQ

LANES = 128
SUBLANES = 8
HALO = 16
VMEM_LIMIT = 56 * 1024 * 1024

TM_PROJ = 1024
T_MIX = 1024
T_ATTN = 1024
PROJ_CHUNK = 512
CONV_ROWS = 256


def _silu(x):
    return x * jax.nn.sigmoid(x)


def _params():
    return pltpu.CompilerParams(
        dimension_semantics=("parallel", "parallel"), vmem_limit_bytes=VMEM_LIMIT)


def _in_proj_kernel(x_ref, g_ref, w_ref, cos_ref, sin_ref, o_ref, *, layer):
    x = x_ref[0]
    ms = jnp.mean(x * x, axis=-1, keepdims=True)
    h = (x * lax.rsqrt(ms + EPS) * g_ref[layer:layer + 1, :]).astype(jnp.bfloat16)
    lane = lax.broadcasted_iota(jnp.int32, (1, LANES), 1)
    first_half = (lane % HEAD_DIM) < HALF

    def rope(t):
        partner = jnp.where(first_half, -pltpu.roll(t, LANES - HALF, 1), pltpu.roll(t, HALF, 1))
        return t * cos_ref[...] + partner * sin_ref[...]

    def store(col, t):
        o_ref[0, :, col:col + t.shape[1]] = t.astype(o_ref.dtype)

    def chunk(c0):
        return jnp.dot(h, w_ref[0, :, c0:c0 + PROJ_CHUNK], preferred_element_type=jnp.float32)

    g = D_GROUP
    acc = chunk(COL_A_U)
    store(P_H, acc[:, :g] * jax.nn.sigmoid(acc[:, g:]))
    acc = chunk(COL_A_Z)
    store(P_ZA, _silu(acc[:, :g]))
    store(P_BB, acc[:, g:])
    acc = chunk(COL_B_C)
    store(P_G, acc[:, :g] * acc[:, g:])
    acc = chunk(COL_B_Z)
    store(P_ZB, _silu(acc[:, :g]))
    for s0 in range(0, g, LANES):
        t = rope(acc[:, g + s0:g + s0 + LANES]) * Q_SCALE
        store(P_CQ + s0, t)
        store(P_CQS + s0, pltpu.roll(t, HEAD_DIM, 1))
    acc = chunk(COL_C_K)
    store(P_CK, rope(acc[:, :LANES]))
    store(P_CV, acc[:, LANES:g])
    store(P_ZC, _silu(acc[:, g:]))
    acc = chunk(COL_D_Q)
    store(P_DQ, acc[:, :g] * Q_SCALE)
    store(P_DK, acc[:, g:])
    acc = chunk(COL_D_V)
    store(P_DV, acc[:, :g])
    store(P_ZD, _silu(acc[:, g:]))


def _in_proj(x, g, w, cos, sin, layer):
    b, s, d = x.shape
    tm = TM_PROJ
    return pl.pallas_call(
        functools.partial(_in_proj_kernel, layer=layer),
        grid=(b, s // tm),
        in_specs=[
            pl.BlockSpec((1, tm, d), lambda bi, i: (bi, i, 0)),
            pl.BlockSpec(g.shape, lambda bi, i: (0, 0)),
            pl.BlockSpec((1, d, D_IN), lambda bi, i: (layer, 0, 0)),
            pl.BlockSpec((tm, LANES), lambda bi, i: (i, 0)),
            pl.BlockSpec((tm, LANES), lambda bi, i: (i, 0)),
        ],
        out_specs=pl.BlockSpec((1, tm, D_PROJ), lambda bi, i: (bi, i, 0)),
        out_shape=jax.ShapeDtypeStruct((b, s, D_PROJ), jnp.bfloat16),
        compiler_params=_params(),
        name="in_proj",
    )(x, g, w, cos, sin)


def _conv_out_kernel(cur_ref, prev_ref, next_ref, aw_ref, ab_ref, lg_ref, lb_ref, bw_ref,
                     ycd_ref, x_ref, w_ref, g_ref, o_ref, hbuf, gbuf, hph, gph,
                     *, layer, final_norm):
    i = pl.program_id(1)
    n = pl.num_programs(1)
    t = T_MIX
    f32 = jnp.float32
    has_prev = (i > 0).astype(f32)
    has_next = (i < n - 1).astype(f32)
    row = lambda ref, j: ref[layer, j:j + 1, :]
    col = lambda ref, rows, c: ref[0, rows, c:c + D_GROUP].astype(f32)

    full = slice(None)
    hwin = jnp.concatenate([col(prev_ref, full, P_H) * has_prev, col(cur_ref, full, P_H),
                            col(next_ref, full, P_H) * has_next], axis=0)
    gwin = jnp.concatenate([col(prev_ref, full, P_G) * has_prev, col(cur_ref, full, P_G),
                            col(next_ref, full, P_G) * has_next], axis=0)
    hbuf[...] = hwin
    gbuf[...] = gwin

    nwin = t + 2 * HALO
    nph = nwin - SUBLANES
    for p in range(1, SUBLANES):
        hph[p - 1] = pltpu.roll(hwin, nwin - p, 0)[0:nph]
    gph[0] = pltpu.roll(gwin, nwin - 1, 0)[0:nph]
    gph[1] = pltpu.roll(gwin, nwin - (SUBLANES - 1), 0)[0:nph]

    def tap(off, rows):
        p = off % SUBLANES
        if p == 0:
            return hbuf[off:off + rows, :]
        return hph[p - 1, off - p:off - p + rows, :]

    rc = CONV_ROWS
    for r0 in range(0, t, rc):
        rows = slice(r0, r0 + rc)
        acc = jnp.broadcast_to(row(ab_ref, 0), (rc, D_GROUP))
        for j in range(CONV_A_WIDTH):
            acc = acc + aw_ref[layer, j:j + 1, :] * tap(HALO - CONV_A_PAD + j + r0, rc)
        mu = jnp.mean(acc, axis=-1, keepdims=True)
        xc = acc - mu
        var = jnp.mean(xc * xc, axis=-1, keepdims=True)
        hn = xc * lax.rsqrt(var + EPS) * row(lg_ref, 0) + row(lb_ref, 0)
        ya = _silu(hn) * col(cur_ref, rows, P_ZA)
        gm = HALO + r0 - SUBLANES
        conv = (row(bw_ref, 0) * gph[1, gm:gm + rc, :]
                + row(bw_ref, 1) * gbuf[HALO + r0:HALO + r0 + rc, :]
                + row(bw_ref, 2) * gph[0, HALO + r0:HALO + r0 + rc, :])
        yb = col(cur_ref, rows, P_BB) * conv * col(cur_ref, rows, P_ZB)
        yab = jnp.concatenate([ya.astype(jnp.bfloat16), yb.astype(jnp.bfloat16)], axis=1)
        xo = x_ref[0, rows, :] + jnp.dot(ycd_ref[0, rows, :], w_ref[0, 2 * D_GROUP:4 * D_GROUP, :],
                                         preferred_element_type=f32)
        xo = xo + jnp.dot(yab, w_ref[0, 0:2 * D_GROUP, :], preferred_element_type=f32)
        if final_norm:
            ms = jnp.mean(xo * xo, axis=-1, keepdims=True)
            xo = xo * lax.rsqrt(ms + EPS) * g_ref[...]
        o_ref[0, rows, :] = xo


def _conv_out(proj, ycd, x, aw, ab, lg, lb, bw, w, g, layer, final_norm):
    b, s, d = x.shape
    t = T_MIX
    hb = t // HALO
    nh = s // HALO
    f32 = jnp.float32
    whole = lambda a: pl.BlockSpec(a.shape, lambda bi, i: (0,) * a.ndim)
    return pl.pallas_call(
        functools.partial(_conv_out_kernel, layer=layer, final_norm=final_norm),
        grid=(b, s // t),
        in_specs=[
            pl.BlockSpec((1, t, W_AB), lambda bi, i: (bi, i, 0)),
            pl.BlockSpec((1, HALO, W_AB), lambda bi, i: (bi, jnp.maximum(i * hb - 1, 0), 0)),
            pl.BlockSpec((1, HALO, W_AB), lambda bi, i: (bi, jnp.minimum((i + 1) * hb, nh - 1), 0)),
            whole(aw), whole(ab), whole(lg), whole(lb), whole(bw),
            pl.BlockSpec((1, t, 2 * D_GROUP), lambda bi, i: (bi, i, 0)),
            pl.BlockSpec((1, t, d), lambda bi, i: (bi, i, 0)),
            pl.BlockSpec((1,) + w.shape[1:], lambda bi, i: (layer, 0, 0)),
            pl.BlockSpec((1, d), lambda bi, i: (0, 0)),
        ],
        out_specs=pl.BlockSpec((1, t, d), lambda bi, i: (bi, i, 0)),
        out_shape=jax.ShapeDtypeStruct((b, s, d), f32),
        scratch_shapes=[pltpu.VMEM((t + 2 * HALO, D_GROUP), f32),
                        pltpu.VMEM((t + 2 * HALO, D_GROUP), f32),
                        pltpu.VMEM((SUBLANES - 1, t + 2 * HALO - SUBLANES, D_GROUP), f32),
                        pltpu.VMEM((2, t + 2 * HALO - SUBLANES, D_GROUP), f32)],
        compiler_params=_params(),
        name="conv_out",
    )(proj, proj, proj, aw, ab, lg, lb, bw, ycd, x, w, g)


def _swa_tile(sink_ref, q_ref, qs_ref, z_ref, kv_ref, y_ref, i, layer):
    s_len = kv_ref.shape[1]
    nkeys = 3 * SWA_BLOCK
    b_ = SWA_BLOCK
    f32 = jnp.float32
    lane = lax.broadcasted_iota(jnp.int32, (1, LANES), 1)
    lo = lane < HEAD_DIM
    zero = jnp.zeros((), jnp.bfloat16)
    ones = jnp.ones((nkeys, LANES), jnp.bfloat16)
    for nb in range(T_ATTN // SWA_BLOCK):
        q0 = i * T_ATTN + nb * SWA_BLOCK
        ws = pl.multiple_of(jnp.clip(q0 - SWA_BLOCK, 0, s_len - nkeys), SWA_BLOCK)
        rows = slice(nb * SWA_BLOCK, (nb + 1) * SWA_BLOCK)
        win = pl.ds(ws, nkeys)
        lhs = [
            jnp.where(lo, q_ref[0, rows, 0:LANES], zero),
            jnp.where(lo, qs_ref[0, rows, 0:LANES], zero),
            jnp.where(lo, zero, qs_ref[0, rows, LANES:2 * LANES]),
            jnp.where(lo, zero, q_ref[0, rows, LANES:2 * LANES]),
        ]
        kw = kv_ref[0, win, 0:LANES]
        vw = jnp.concatenate([kv_ref[0, win, LANES:2 * LANES], ones], axis=1)
        qpos = q0 + lax.broadcasted_iota(jnp.int32, (SWA_BLOCK, nkeys), 0)
        kpos = ws + lax.broadcasted_iota(jnp.int32, (SWA_BLOCK, nkeys), 1)
        valid = jnp.abs(kpos - qpos) <= SWA_WINDOW
        num, den = [], []
        for h in range(SWA_HEADS):
            sc = jnp.einsum("qd,kd->qk", lhs[h], kw, preferred_element_type=f32)
            sh = jnp.where(valid, sc, NEG_INF)
            sink = sink_ref[layer, h]
            m = jnp.maximum(jnp.max(sh, axis=-1, keepdims=True), sink)
            e = jnp.exp2(sh - m)
            pv = jnp.dot(e.astype(jnp.bfloat16), vw, preferred_element_type=f32)
            num.append(pv[:, :LANES])
            den.append(pv[:, LANES:] + jnp.exp2(sink - m))
        og0 = jnp.where(lo, num[0], pltpu.roll(num[1], HEAD_DIM, 1)) / jnp.where(lo, den[0], den[1])
        og1 = jnp.where(lo, pltpu.roll(num[2], HEAD_DIM, 1), num[3]) / jnp.where(lo, den[2], den[3])
        y = jnp.concatenate([og0, og1], axis=-1) * z_ref[0, rows, :].astype(f32)
        y_ref[rows, 0:D_GROUP] = y.astype(y_ref.dtype)


def _nbr_tile(q_ref, z_ref, k_ref, v_ref, bias_ref, y_ref, i):
    rows_total = k_ref.shape[1] // GRID_W
    rows_tile = T_ATTN // GRID_W
    nkeys = NA_KH * GRID_W
    f32 = jnp.float32
    lane = lax.broadcasted_iota(jnp.int32, (1, LANES), 1)
    lo = lane < HEAD_DIM
    zero = jnp.zeros((), jnp.bfloat16)
    ones = jnp.ones((nkeys, LANES), jnp.bfloat16)

    for rr in range(rows_tile):
        r = i * rows_tile + rr
        r0 = jnp.clip(r - NA_KH // 2, 0, rows_total - NA_KH)
        d0 = r0 - r + (NA_KH - 1)
        ks = pl.multiple_of(r0 * GRID_W, GRID_W)
        qrow = slice(rr * GRID_W, (rr + 1) * GRID_W)
        outs = []
        for g in range(2):
            cols = slice(g * LANES, (g + 1) * LANES)
            qg = q_ref[0, qrow, cols]
            lhs = jnp.concatenate([jnp.where(lo, qg, zero), jnp.where(lo, zero, qg)], axis=0)
            kw = k_ref[0, pl.ds(ks, nkeys), cols]
            vw = jnp.concatenate([v_ref[0, pl.ds(ks, nkeys), cols], ones], axis=1)
            sc = jnp.einsum("qd,kd->qk", lhs, kw, preferred_element_type=f32)
            bias = jnp.concatenate(
                [jnp.concatenate([bias_ref[0, 2 * g + hh, d0 + 2 * m]
                                  for m in range(NA_KH // 2)], axis=-1)
                 for hh in range(2)], axis=0)
            sc = sc + bias
            m_ = jnp.max(sc, axis=-1, keepdims=True)
            e = jnp.exp2(sc - m_)
            pv = jnp.dot(e.astype(jnp.bfloat16), vw, preferred_element_type=f32)
            pv = jnp.where(jnp.concatenate([lo, lo], axis=1), pv[0:GRID_W], pv[GRID_W:2 * GRID_W])
            outs.append(pv[:, :LANES] / pv[:, LANES:])
        y = jnp.concatenate(outs, axis=-1) * z_ref[0, qrow, :].astype(f32)
        y_ref[qrow, D_GROUP:2 * D_GROUP] = y.astype(y_ref.dtype)


def _attn_kernel(sink_ref, cq_ref, cqs_ref, cz_ref, ckv_ref, dq_ref, dz_ref, dk_ref, dv_ref,
                 bias_ref, o_ref, *, layer):
    i = pl.program_id(1)
    y_ref = o_ref.at[0]
    _swa_tile(sink_ref, cq_ref, cqs_ref, cz_ref, ckv_ref, y_ref, i, layer)
    _nbr_tile(dq_ref, dz_ref, dk_ref, dv_ref, bias_ref, y_ref, i)


def _attn(proj, sink, bias, layer):
    b, s, _ = proj.shape
    t = T_ATTN
    tile = lambda col: pl.BlockSpec((1, t, D_GROUP), lambda bi, i, sk: (bi, i, col // D_GROUP))
    seq = lambda col: pl.BlockSpec((1, s, D_GROUP), lambda bi, i, sk: (bi, 0, col // D_GROUP))
    return pl.pallas_call(
        functools.partial(_attn_kernel, layer=layer),
        grid_spec=pltpu.PrefetchScalarGridSpec(
            num_scalar_prefetch=1,
            grid=(b, s // t),
            in_specs=[
                tile(P_CQ), tile(P_CQS), tile(P_ZC), seq(P_CK),
                tile(P_DQ), tile(P_ZD), seq(P_DK), seq(P_DV),
                pl.BlockSpec((1,) + bias.shape[1:], lambda bi, i, sk: (layer, 0, 0, 0, 0)),
            ],
            out_specs=pl.BlockSpec((1, t, 2 * D_GROUP), lambda bi, i, sk: (bi, i, 0)),
        ),
        out_shape=jax.ShapeDtypeStruct((b, s, 2 * D_GROUP), jnp.bfloat16),
        compiler_params=_params(),
        name="attn",
    )(sink, proj, proj, proj, proj, proj, proj, proj, proj, bias)


def _rope_tables(s):
    inv_freq = ROPE_THETA ** (-jnp.arange(0, HEAD_DIM, 2, dtype=jnp.float32) / HEAD_DIM)
    inv_freq = jnp.tile(inv_freq, LANES // HALF)
    ang = jnp.arange(s, dtype=jnp.float32)[:, None] * inv_freq[None, :]
    return jnp.cos(ang), jnp.sin(ang)


def _nbr_bias_table(rpb):
    c = np.arange(GRID_W)
    c0 = np.clip(c - NA_KW // 2, 0, GRID_W - NA_KW)
    col_ok = (c[None, :] >= c0[:, None]) & (c[None, :] < c0[:, None] + NA_KW)
    dc = np.clip(c[None, :] - c[:, None], -(NA_KW - 1), NA_KW - 1) + (NA_KW - 1)
    onehot = (dc[None] == np.arange(2 * NA_KW - 1)[:, None, None]).astype(np.float32)
    full = jnp.einsum("lhdc,cqk->lhdqk", rpb, jnp.asarray(onehot),
                      precision=lax.Precision.HIGHEST)
    full = jnp.where(jnp.asarray(col_ok), full * LOG2E, NEG_INF)
    pad_rows = lambda a, n: jnp.pad(a, ((0, 0), (0, 0), (0, n), (0, 0), (0, 0)),
                                    constant_values=NEG_INF)
    return jnp.concatenate([pad_rows(full, 1), pad_rows(full[:, :, 1:], 2)], axis=-1)


def kernel(x, norm_g, w_in, w_out, conv_a_w, conv_a_b, ln_a_g, ln_a_b, conv_b_w, swa_sink,
           na_rpb, final_norm_g):
    depth = norm_g.shape[0]
    s = x.shape[1]
    cos_t, sin_t = _rope_tables(s)
    w_in_b = w_in.astype(jnp.bfloat16)
    w_out_b = w_out.astype(jnp.bfloat16)
    bias = _nbr_bias_table(na_rpb)
    sink = swa_sink * LOG2E
    fg = final_norm_g.reshape(1, -1)
    vec = lambda a: a.reshape(depth, 1, -1)
    for l in range(depth):
        proj = _in_proj(x, norm_g, w_in_b, cos_t, sin_t, l)
        ycd = _attn(proj, sink, bias, l)
        x = _conv_out(proj, ycd, x, conv_a_w, vec(conv_a_b), vec(ln_a_g), vec(ln_a_b), conv_b_w,
                      w_out_b, fg, l, final_norm=(l == depth - 1))
    return x
```

```python
import functools
import math

import numpy as np
import jax
import jax.numpy as jnp
from jax import lax
from jax.experimental import pallas as pl
from jax.experimental.pallas import tpu as pltpu

D_MODEL = 1024
D_GROUP = 256
HEAD_DIM = 64
HALF = HEAD_DIM // 2
GRID_W = 64
CONV_A_WIDTH = 31
CONV_A_PAD = (CONV_A_WIDTH - 1) // 2
CONV_B_WIDTH = 3
SWA_WINDOW = 128
SWA_BLOCK = 128
SWA_HEADS = 4
NA_KH = 8
NA_KW = 16
ROPE_THETA = 10000.0
EPS = 1e-6
NEG_INF = -1e30
LOG2E = math.log2(math.e)
Q_SCALE = HEAD_DIM ** -0.5 * LOG2E

COL_A_U, COL_A_V, COL_A_Z = 0, 256, 512
COL_B_B, COL_B_C, COL_B_X, COL_B_Z = 768, 1024, 1280, 1536
COL_C_Q, COL_C_K, COL_C_V, COL_C_Z = 1792, 2048, 2176, 2304
COL_D_Q, COL_D_K, COL_D_V, COL_D_Z = 2560, 2816, 3072, 3328
D_IN = 3584

P_H, P_ZA = 0, 256
P_BB, P_G, P_ZB = 512, 768, 1024
P_CQ, P_CQS, P_CK, P_CV, P_ZC = 1280, 1536, 1792, 1920, 2048
P_DQ, P_DK, P_DV, P_ZD = 2304, 2560, 2816, 3072
D_PROJ = 3328
W_AB = P_CQ

LANES = 128
SUBLANES = 8
HALO = 16
VMEM_LIMIT = 56 * 1024 * 1024

TM_PROJ = 1024
T_MIX = 1024
T_ATTN = 2048
PROJ_CHUNK = 512
CONV_ROWS = 256


def _silu(x):
    return x * jax.nn.sigmoid(x)


def _params():
    return pltpu.CompilerParams(
        dimension_semantics=("parallel", "parallel"), vmem_limit_bytes=VMEM_LIMIT)


def _in_proj_kernel(x_ref, g_ref, w_ref, cos_ref, sin_ref, o_ref, *, layer):
    x = x_ref[0]
    ms = jnp.mean(x * x, axis=-1, keepdims=True)
    h = (x * lax.rsqrt(ms + EPS) * g_ref[layer:layer + 1, :]).astype(jnp.bfloat16)
    lane = lax.broadcasted_iota(jnp.int32, (1, LANES), 1)
    first_half = (lane % HEAD_DIM) < HALF

    def rope(t):
        partner = jnp.where(first_half, -pltpu.roll(t, LANES - HALF, 1), pltpu.roll(t, HALF, 1))
        return t * cos_ref[...] + partner * sin_ref[...]

    def store(col, t):
        o_ref[0, :, col:col + t.shape[1]] = t.astype(o_ref.dtype)

    def chunk(c0):
        return jnp.dot(h, w_ref[0, :, c0:c0 + PROJ_CHUNK], preferred_element_type=jnp.float32)

    g = D_GROUP
    acc = chunk(COL_A_U)
    store(P_H, acc[:, :g] * jax.nn.sigmoid(acc[:, g:]))
    acc = chunk(COL_A_Z)
    store(P_ZA, _silu(acc[:, :g]))
    store(P_BB, acc[:, g:])
    acc = chunk(COL_B_C)
    store(P_G, acc[:, :g] * acc[:, g:])
    acc = chunk(COL_B_Z)
    store(P_ZB, _silu(acc[:, :g]))
    for s0 in range(0, g, LANES):
        t = rope(acc[:, g + s0:g + s0 + LANES]) * Q_SCALE
        store(P_CQ + s0, t)
        store(P_CQS + s0, pltpu.roll(t, HEAD_DIM, 1))
    acc = chunk(COL_C_K)
    store(P_CK, rope(acc[:, :LANES]))
    store(P_CV, acc[:, LANES:g])
    store(P_ZC, _silu(acc[:, g:]))
    acc = chunk(COL_D_Q)
    store(P_DQ, acc[:, :g] * Q_SCALE)
    store(P_DK, acc[:, g:])
    acc = chunk(COL_D_V)
    store(P_DV, acc[:, :g])
    store(P_ZD, _silu(acc[:, g:]))


def _in_proj(x, g, w, cos, sin, layer):
    b, s, d = x.shape
    tm = TM_PROJ
    return pl.pallas_call(
        functools.partial(_in_proj_kernel, layer=layer),
        grid=(b, s // tm),
        in_specs=[
            pl.BlockSpec((1, tm, d), lambda bi, i: (bi, i, 0)),
            pl.BlockSpec(g.shape, lambda bi, i: (0, 0)),
            pl.BlockSpec((1, d, D_IN), lambda bi, i: (layer, 0, 0)),
            pl.BlockSpec((tm, LANES), lambda bi, i: (i, 0)),
            pl.BlockSpec((tm, LANES), lambda bi, i: (i, 0)),
        ],
        out_specs=pl.BlockSpec((1, tm, D_PROJ), lambda bi, i: (bi, i, 0)),
        out_shape=jax.ShapeDtypeStruct((b, s, D_PROJ), jnp.bfloat16),
        compiler_params=_params(),
        name="in_proj",
    )(x, g, w, cos, sin)


def _conv_out_kernel(cur_ref, prev_ref, next_ref, aw_ref, ab_ref, lg_ref, lb_ref, bw_ref,
                     ycd_ref, x_ref, w_ref, g_ref, o_ref, hbuf, gbuf, hph, gph,
                     *, layer, final_norm):
    i = pl.program_id(1)
    n = pl.num_programs(1)
    t = T_MIX
    f32 = jnp.float32
    has_prev = (i > 0).astype(f32)
    has_next = (i < n - 1).astype(f32)
    row = lambda ref, j: ref[layer, j:j + 1, :]
    col = lambda ref, rows, c: ref[0, rows, c:c + D_GROUP].astype(f32)

    full = slice(None)
    hwin = jnp.concatenate([col(prev_ref, full, P_H) * has_prev, col(cur_ref, full, P_H),
                            col(next_ref, full, P_H) * has_next], axis=0)
    gwin = jnp.concatenate([col(prev_ref, full, P_G) * has_prev, col(cur_ref, full, P_G),
                            col(next_ref, full, P_G) * has_next], axis=0)
    hbuf[...] = hwin
    gbuf[...] = gwin

    nwin = t + 2 * HALO
    nph = nwin - SUBLANES
    for p in range(1, SUBLANES):
        hph[p - 1] = pltpu.roll(hwin, nwin - p, 0)[0:nph]
    gph[0] = pltpu.roll(gwin, nwin - 1, 0)[0:nph]
    gph[1] = pltpu.roll(gwin, nwin - (SUBLANES - 1), 0)[0:nph]

    def tap(off, rows):
        p = off % SUBLANES
        if p == 0:
            return hbuf[off:off + rows, :]
        return hph[p - 1, off - p:off - p + rows, :]

    rc = CONV_ROWS
    for r0 in range(0, t, rc):
        rows = slice(r0, r0 + rc)
        acc = jnp.broadcast_to(row(ab_ref, 0), (rc, D_GROUP))
        for j in range(CONV_A_WIDTH):
            acc = acc + aw_ref[layer, j:j + 1, :] * tap(HALO - CONV_A_PAD + j + r0, rc)
        mu = jnp.mean(acc, axis=-1, keepdims=True)
        xc = acc - mu
        var = jnp.mean(xc * xc, axis=-1, keepdims=True)
        hn = xc * lax.rsqrt(var + EPS) * row(lg_ref, 0) + row(lb_ref, 0)
        ya = _silu(hn) * col(cur_ref, rows, P_ZA)
        gm = HALO + r0 - SUBLANES
        conv = (row(bw_ref, 0) * gph[1, gm:gm + rc, :]
                + row(bw_ref, 1) * gbuf[HALO + r0:HALO + r0 + rc, :]
                + row(bw_ref, 2) * gph[0, HALO + r0:HALO + r0 + rc, :])
        yb = col(cur_ref, rows, P_BB) * conv * col(cur_ref, rows, P_ZB)
        yab = jnp.concatenate([ya.astype(jnp.bfloat16), yb.astype(jnp.bfloat16)], axis=1)
        xo = x_ref[0, rows, :] + jnp.dot(ycd_ref[0, rows, :], w_ref[0, 2 * D_GROUP:4 * D_GROUP, :],
                                         preferred_element_type=f32)
        xo = xo + jnp.dot(yab, w_ref[0, 0:2 * D_GROUP, :], preferred_element_type=f32)
        if final_norm:
            ms = jnp.mean(xo * xo, axis=-1, keepdims=True)
            xo = xo * lax.rsqrt(ms + EPS) * g_ref[...]
        o_ref[0, rows, :] = xo


def _conv_out(proj, ycd, x, aw, ab, lg, lb, bw, w, g, layer, final_norm):
    b, s, d = x.shape
    t = T_MIX
    hb = t // HALO
    nh = s // HALO
    f32 = jnp.float32
    whole = lambda a: pl.BlockSpec(a.shape, lambda bi, i: (0,) * a.ndim)
    return pl.pallas_call(
        functools.partial(_conv_out_kernel, layer=layer, final_norm=final_norm),
        grid=(b, s // t),
        in_specs=[
            pl.BlockSpec((1, t, W_AB), lambda bi, i: (bi, i, 0)),
            pl.BlockSpec((1, HALO, W_AB), lambda bi, i: (bi, jnp.maximum(i * hb - 1, 0), 0)),
            pl.BlockSpec((1, HALO, W_AB), lambda bi, i: (bi, jnp.minimum((i + 1) * hb, nh - 1), 0)),
            whole(aw), whole(ab), whole(lg), whole(lb), whole(bw),
            pl.BlockSpec((1, t, 2 * D_GROUP), lambda bi, i: (bi, i, 0)),
            pl.BlockSpec((1, t, d), lambda bi, i: (bi, i, 0)),
            pl.BlockSpec((1,) + w.shape[1:], lambda bi, i: (layer, 0, 0)),
            pl.BlockSpec((1, d), lambda bi, i: (0, 0)),
        ],
        out_specs=pl.BlockSpec((1, t, d), lambda bi, i: (bi, i, 0)),
        out_shape=jax.ShapeDtypeStruct((b, s, d), f32),
        scratch_shapes=[pltpu.VMEM((t + 2 * HALO, D_GROUP), f32),
                        pltpu.VMEM((t + 2 * HALO, D_GROUP), f32),
                        pltpu.VMEM((SUBLANES - 1, t + 2 * HALO - SUBLANES, D_GROUP), f32),
                        pltpu.VMEM((2, t + 2 * HALO - SUBLANES, D_GROUP), f32)],
        compiler_params=_params(),
        name="conv_out",
    )(proj, proj, proj, aw, ab, lg, lb, bw, ycd, x, w, g)


def _swa_tile(sink_ref, q_ref, qs_ref, z_ref, kv_ref, y_ref, i, layer):
    s_len = kv_ref.shape[1]
    nkeys = 3 * SWA_BLOCK
    b_ = SWA_BLOCK
    f32 = jnp.float32
    lane = lax.broadcasted_iota(jnp.int32, (1, LANES), 1)
    lo = lane < HEAD_DIM
    zero = jnp.zeros((), jnp.bfloat16)
    ones = jnp.ones((nkeys, LANES), jnp.bfloat16)
    for nb in range(T_ATTN // SWA_BLOCK):
        q0 = i * T_ATTN + nb * SWA_BLOCK
        ws = pl.multiple_of(jnp.clip(q0 - SWA_BLOCK, 0, s_len - nkeys), SWA_BLOCK)
        rows = slice(nb * SWA_BLOCK, (nb + 1) * SWA_BLOCK)
        win = pl.ds(ws, nkeys)
        lhs = [
            jnp.where(lo, q_ref[0, rows, 0:LANES], zero),
            jnp.where(lo, qs_ref[0, rows, 0:LANES], zero),
            jnp.where(lo, zero, qs_ref[0, rows, LANES:2 * LANES]),
            jnp.where(lo, zero, q_ref[0, rows, LANES:2 * LANES]),
        ]
        kw = kv_ref[0, win, 0:LANES]
        vw = jnp.concatenate([kv_ref[0, win, LANES:2 * LANES], ones], axis=1)
        qpos = q0 + lax.broadcasted_iota(jnp.int32, (SWA_BLOCK, nkeys), 0)
        kpos = ws + lax.broadcasted_iota(jnp.int32, (SWA_BLOCK, nkeys), 1)
        valid = jnp.abs(kpos - qpos) <= SWA_WINDOW
        num, den = [], []
        for h in range(SWA_HEADS):
            sc = jnp.einsum("qd,kd->qk", lhs[h], kw, preferred_element_type=f32)
            sh = jnp.where(valid, sc, NEG_INF)
            sink = sink_ref[layer, h]
            m = jnp.maximum(jnp.max(sh, axis=-1, keepdims=True), sink)
            e = jnp.exp2(sh - m)
            pv = jnp.dot(e.astype(jnp.bfloat16), vw, preferred_element_type=f32)
            num.append(pv[:, :LANES])
            den.append(pv[:, LANES:] + jnp.exp2(sink - m))
        og0 = jnp.where(lo, num[0], pltpu.roll(num[1], HEAD_DIM, 1)) / jnp.where(lo, den[0], den[1])
        og1 = jnp.where(lo, pltpu.roll(num[2], HEAD_DIM, 1), num[3]) / jnp.where(lo, den[2], den[3])
        y = jnp.concatenate([og0, og1], axis=-1) * z_ref[0, rows, :].astype(f32)
        y_ref[rows, 0:D_GROUP] = y.astype(y_ref.dtype)


def _nbr_tile(q_ref, z_ref, k_ref, v_ref, bias_ref, y_ref, i):
    rows_total = k_ref.shape[1] // GRID_W
    rows_tile = T_ATTN // GRID_W
    nkeys = NA_KH * GRID_W
    f32 = jnp.float32
    lane = lax.broadcasted_iota(jnp.int32, (1, LANES), 1)
    lo = lane < HEAD_DIM
    zero = jnp.zeros((), jnp.bfloat16)
    ones = jnp.ones((nkeys, LANES), jnp.bfloat16)

    for rr in range(rows_tile):
        r = i * rows_tile + rr
        r0 = jnp.clip(r - NA_KH // 2, 0, rows_total - NA_KH)
        d0 = r0 - r + (NA_KH - 1)
        ks = pl.multiple_of(r0 * GRID_W, GRID_W)
        qrow = slice(rr * GRID_W, (rr + 1) * GRID_W)
        outs = []
        for g in range(2):
            cols = slice(g * LANES, (g + 1) * LANES)
            qg = q_ref[0, qrow, cols]
            lhs = jnp.concatenate([jnp.where(lo, qg, zero), jnp.where(lo, zero, qg)], axis=0)
            kw = k_ref[0, pl.ds(ks, nkeys), cols]
            vw = jnp.concatenate([v_ref[0, pl.ds(ks, nkeys), cols], ones], axis=1)
            sc = jnp.einsum("qd,kd->qk", lhs, kw, preferred_element_type=f32)
            bias = jnp.concatenate(
                [jnp.concatenate([bias_ref[0, 2 * g + hh, d0 + 2 * m]
                                  for m in range(NA_KH // 2)], axis=-1)
                 for hh in range(2)], axis=0)
            sc = sc + bias
            m_ = jnp.max(sc, axis=-1, keepdims=True)
            e = jnp.exp2(sc - m_)
            pv = jnp.dot(e.astype(jnp.bfloat16), vw, preferred_element_type=f32)
            pv = jnp.where(jnp.concatenate([lo, lo], axis=1), pv[0:GRID_W], pv[GRID_W:2 * GRID_W])
            outs.append(pv[:, :LANES] / pv[:, LANES:])
        y = jnp.concatenate(outs, axis=-1) * z_ref[0, qrow, :].astype(f32)
        y_ref[qrow, D_GROUP:2 * D_GROUP] = y.astype(y_ref.dtype)


def _attn_kernel(sink_ref, cq_ref, cqs_ref, cz_ref, ckv_ref, dq_ref, dz_ref, dk_ref, dv_ref,
                 bias_ref, o_ref, *, layer):
    i = pl.program_id(1)
    y_ref = o_ref.at[0]
    _swa_tile(sink_ref, cq_ref, cqs_ref, cz_ref, ckv_ref, y_ref, i, layer)
    _nbr_tile(dq_ref, dz_ref, dk_ref, dv_ref, bias_ref, y_ref, i)


def _attn(proj, sink, bias, layer):
    b, s, _ = proj.shape
    t = T_ATTN
    tile = lambda col: pl.BlockSpec((1, t, D_GROUP), lambda bi, i, sk: (bi, i, col // D_GROUP))
    seq = lambda col: pl.BlockSpec((1, s, D_GROUP), lambda bi, i, sk: (bi, 0, col // D_GROUP))
    return pl.pallas_call(
        functools.partial(_attn_kernel, layer=layer),
        grid_spec=pltpu.PrefetchScalarGridSpec(
            num_scalar_prefetch=1,
            grid=(b, s // t),
            in_specs=[
                tile(P_CQ), tile(P_CQS), tile(P_ZC), seq(P_CK),
                tile(P_DQ), tile(P_ZD), seq(P_DK), seq(P_DV),
                pl.BlockSpec((1,) + bias.shape[1:], lambda bi, i, sk: (layer, 0, 0, 0, 0)),
            ],
            out_specs=pl.BlockSpec((1, t, 2 * D_GROUP), lambda bi, i, sk: (bi, i, 0)),
        ),
        out_shape=jax.ShapeDtypeStruct((b, s, 2 * D_GROUP), jnp.bfloat16),
        compiler_params=_params(),
        name="attn",
    )(sink, proj, proj, proj, proj, proj, proj, proj, proj, bias)


def _rope_tables(s):
    inv_freq = ROPE_THETA ** (-jnp.arange(0, HEAD_DIM, 2, dtype=jnp.float32) / HEAD_DIM)
    inv_freq = jnp.tile(inv_freq, LANES // HALF)
    ang = jnp.arange(s, dtype=jnp.float32)[:, None] * inv_freq[None, :]
    return jnp.cos(ang), jnp.sin(ang)


def _nbr_bias_table(rpb):
    c = np.arange(GRID_W)
    c0 = np.clip(c - NA_KW // 2, 0, GRID_W - NA_KW)
    col_ok = (c[None, :] >= c0[:, None]) & (c[None, :] < c0[:, None] + NA_KW)
    dc = np.clip(c[None, :] - c[:, None], -(NA_KW - 1), NA_KW - 1) + (NA_KW - 1)
    onehot = (dc[None] == np.arange(2 * NA_KW - 1)[:, None, None]).astype(np.float32)
    full = jnp.einsum("lhdc,cqk->lhdqk", rpb, jnp.asarray(onehot),
                      precision=lax.Precision.HIGHEST)
    full = jnp.where(jnp.asarray(col_ok), full * LOG2E, NEG_INF)
    pad_rows = lambda a, n: jnp.pad(a, ((0, 0), (0, 0), (0, n), (0, 0), (0, 0)),
                                    constant_values=NEG_INF)
    return jnp.concatenate([pad_rows(full, 1), pad_rows(full[:, :, 1:], 2)], axis=-1)


def kernel(x, norm_g, w_in, w_out, conv_a_w, conv_a_b, ln_a_g, ln_a_b, conv_b_w, swa_sink,
           na_rpb, final_norm_g):
    depth = norm_g.shape[0]
    s = x.shape[1]
    cos_t, sin_t = _rope_tables(s)
    w_in_b = w_in.astype(jnp.bfloat16)
    w_out_b = w_out.astype(jnp.bfloat16)
    bias = _nbr_bias_table(na_rpb)
    sink = swa_sink * LOG2E
    fg = final_norm_g.reshape(1, -1)
    vec = lambda a: a.reshape(depth, 1, -1)
    for l in range(depth):
        proj = _in_proj(x, norm_g, w_in_b, cos_t, sin_t, l)
        ycd = _attn(proj, sink, bias, l)
        x = _conv_out(proj, ycd, x, conv_a_w, vec(conv_a_b), vec(ln_a_g), vec(ln_a_b), conv_b_w,
                      w_out_b, fg, l, final_norm=(l == depth - 1))
    return x
```

```python
import functools
import math

import numpy as np
import jax
import jax.numpy as jnp
from jax import lax
from jax.experimental import pallas as pl
from jax.experimental.pallas import tpu as pltpu

D_MODEL = 1024
D_GROUP = 256
HEAD_DIM = 64
HALF = HEAD_DIM // 2
GRID_W = 64
CONV_A_WIDTH = 31
CONV_A_PAD = (CONV_A_WIDTH - 1) // 2
CONV_B_WIDTH = 3
SWA_WINDOW = 128
SWA_BLOCK = 128
SWA_HEADS = 4
NA_KH = 8
NA_KW = 16
ROPE_THETA = 10000.0
EPS = 1e-6
NEG_INF = -1e30
LOG2E = math.log2(math.e)
Q_SCALE = HEAD_DIM ** -0.5 * LOG2E

COL_A_U, COL_A_V, COL_A_Z = 0, 256, 512
COL_B_B, COL_B_C, COL_B_X, COL_B_Z = 768, 1024, 1280, 1536
COL_C_Q, COL_C_K, COL_C_V, COL_C_Z = 1792, 2048, 2176, 2304
COL_D_Q, COL_D_K, COL_D_V, COL_D_Z = 2560, 2816, 3072, 3328
D_IN = 3584

P_H, P_ZA = 0, 256
P_BB, P_G, P_ZB = 512, 768, 1024
P_CQ, P_CQS, P_CK, P_CV, P_ZC = 1280, 1536, 1792, 1920, 2048
P_DQ, P_DK, P_DV, P_ZD = 2304, 2560, 2816, 3072
D_PROJ = 3328
W_AB = P_CQ

LANES = 128
SUBLANES = 8
HALO = 16
VMEM_LIMIT = 56 * 1024 * 1024

TM_PROJ = 1024
T_MIX = 1024
T_ATTN = 1024
PROJ_CHUNK = 512
CONV_ROWS = 256


def _silu(x):
    return x * jax.nn.sigmoid(x)


def _params():
    return pltpu.CompilerParams(
        dimension_semantics=("parallel", "parallel"), vmem_limit_bytes=VMEM_LIMIT)


def _in_proj_kernel(x_ref, g_ref, w_ref, cos_ref, sin_ref, o_ref, *, layer):
    x = x_ref[0]
    ms = jnp.mean(x * x, axis=-1, keepdims=True)
    h = (x * lax.rsqrt(ms + EPS) * g_ref[layer:layer + 1, :]).astype(jnp.bfloat16)
    lane = lax.broadcasted_iota(jnp.int32, (1, LANES), 1)
    first_half = (lane % HEAD_DIM) < HALF

    def rope(t):
        partner = jnp.where(first_half, -pltpu.roll(t, LANES - HALF, 1), pltpu.roll(t, HALF, 1))
        return t * cos_ref[...] + partner * sin_ref[...]

    def store(col, t):
        o_ref[0, :, col:col + t.shape[1]] = t.astype(o_ref.dtype)

    def chunk(c0):
        return jnp.dot(h, w_ref[0, :, c0:c0 + PROJ_CHUNK], preferred_element_type=jnp.float32)

    g = D_GROUP
    acc = chunk(COL_A_U)
    store(P_H, acc[:, :g] * jax.nn.sigmoid(acc[:, g:]))
    acc = chunk(COL_A_Z)
    store(P_ZA, _silu(acc[:, :g]))
    store(P_BB, acc[:, g:])
    acc = chunk(COL_B_C)
    store(P_G, acc[:, :g] * acc[:, g:])
    acc = chunk(COL_B_Z)
    store(P_ZB, _silu(acc[:, :g]))
    for s0 in range(0, g, LANES):
        t = rope(acc[:, g + s0:g + s0 + LANES]) * Q_SCALE
        store(P_CQ + s0, t)
        store(P_CQS + s0, pltpu.roll(t, HEAD_DIM, 1))
    acc = chunk(COL_C_K)
    store(P_CK, rope(acc[:, :LANES]))
    store(P_CV, acc[:, LANES:g])
    store(P_ZC, _silu(acc[:, g:]))
    acc = chunk(COL_D_Q)
    store(P_DQ, acc[:, :g] * Q_SCALE)
    store(P_DK, acc[:, g:])
    acc = chunk(COL_D_V)
    store(P_DV, acc[:, :g])
    store(P_ZD, _silu(acc[:, g:]))


def _in_proj(x, g, w, cos, sin, layer):
    b, s, d = x.shape
    tm = TM_PROJ
    return pl.pallas_call(
        functools.partial(_in_proj_kernel, layer=layer),
        grid=(b, s // tm),
        in_specs=[
            pl.BlockSpec((1, tm, d), lambda bi, i: (bi, i, 0)),
            pl.BlockSpec(g.shape, lambda bi, i: (0, 0)),
            pl.BlockSpec((1, d, D_IN), lambda bi, i: (layer, 0, 0)),
            pl.BlockSpec((tm, LANES), lambda bi, i: (i, 0)),
            pl.BlockSpec((tm, LANES), lambda bi, i: (i, 0)),
        ],
        out_specs=pl.BlockSpec((1, tm, D_PROJ), lambda bi, i: (bi, i, 0)),
        out_shape=jax.ShapeDtypeStruct((b, s, D_PROJ), jnp.bfloat16),
        compiler_params=_params(),
        name="in_proj",
    )(x, g, w, cos, sin)


def _conv_out_kernel(cur_ref, prev_ref, next_ref, aw_ref, ab_ref, lg_ref, lb_ref, bw_ref,
                     ycd_ref, x_ref, w_ref, g_ref, o_ref, hbuf, gbuf, hph, gph,
                     *, layer, final_norm):
    i = pl.program_id(1)
    n = pl.num_programs(1)
    t = T_MIX
    f32 = jnp.float32
    has_prev = (i > 0).astype(f32)
    has_next = (i < n - 1).astype(f32)
    row = lambda ref, j: ref[layer, j:j + 1, :]
    vec = lambda ref: ref[layer:layer + 1, :]
    col = lambda ref, rows, c: ref[0, rows, c:c + D_GROUP].astype(f32)

    full = slice(None)
    hwin = jnp.concatenate([col(prev_ref, full, P_H) * has_prev, col(cur_ref, full, P_H),
                            col(next_ref, full, P_H) * has_next], axis=0)
    gwin = jnp.concatenate([col(prev_ref, full, P_G) * has_prev, col(cur_ref, full, P_G),
                            col(next_ref, full, P_G) * has_next], axis=0)
    hbuf[...] = hwin
    gbuf[...] = gwin

    nwin = t + 2 * HALO
    nph = nwin - SUBLANES
    for p in range(1, SUBLANES):
        hph[p - 1] = pltpu.roll(hwin, nwin - p, 0)[0:nph]
    gph[0] = pltpu.roll(gwin, nwin - 1, 0)[0:nph]
    gph[1] = pltpu.roll(gwin, nwin - (SUBLANES - 1), 0)[0:nph]

    def tap(off, rows):
        p = off % SUBLANES
        if p == 0:
            return hbuf[off:off + rows, :]
        return hph[p - 1, off - p:off - p + rows, :]

    rc = CONV_ROWS
    for r0 in range(0, t, rc):
        rows = slice(r0, r0 + rc)
        acc = jnp.broadcast_to(vec(ab_ref), (rc, D_GROUP))
        for j in range(CONV_A_WIDTH):
            acc = acc + aw_ref[layer, j:j + 1, :] * tap(HALO - CONV_A_PAD + j + r0, rc)
        mu = jnp.mean(acc, axis=-1, keepdims=True)
        xc = acc - mu
        var = jnp.mean(xc * xc, axis=-1, keepdims=True)
        hn = xc * lax.rsqrt(var + EPS) * vec(lg_ref) + vec(lb_ref)
        ya = _silu(hn) * col(cur_ref, rows, P_ZA)
        gm = HALO + r0 - SUBLANES
        conv = (row(bw_ref, 0) * gph[1, gm:gm + rc, :]
                + row(bw_ref, 1) * gbuf[HALO + r0:HALO + r0 + rc, :]
                + row(bw_ref, 2) * gph[0, HALO + r0:HALO + r0 + rc, :])
        yb = col(cur_ref, rows, P_BB) * conv * col(cur_ref, rows, P_ZB)
        yab = jnp.concatenate([ya.astype(jnp.bfloat16), yb.astype(jnp.bfloat16)], axis=1)
        xo = x_ref[0, rows, :] + jnp.dot(ycd_ref[0, rows, :], w_ref[0, 2 * D_GROUP:4 * D_GROUP, :],
                                         preferred_element_type=f32)
        xo = xo + jnp.dot(yab, w_ref[0, 0:2 * D_GROUP, :], preferred_element_type=f32)
        if final_norm:
            ms = jnp.mean(xo * xo, axis=-1, keepdims=True)
            xo = xo * lax.rsqrt(ms + EPS) * g_ref[...]
        o_ref[0, rows, :] = xo


def _conv_out(proj, ycd, x, aw, ab, lg, lb, bw, w, g, layer, final_norm):
    b, s, d = x.shape
    t = T_MIX
    hb = t // HALO
    nh = s // HALO
    f32 = jnp.float32
    whole = lambda a: pl.BlockSpec(a.shape, lambda bi, i: (0,) * a.ndim)
    return pl.pallas_call(
        functools.partial(_conv_out_kernel, layer=layer, final_norm=final_norm),
        grid=(b, s // t),
        in_specs=[
            pl.BlockSpec((1, t, W_AB), lambda bi, i: (bi, i, 0)),
            pl.BlockSpec((1, HALO, W_AB), lambda bi, i: (bi, jnp.maximum(i * hb - 1, 0), 0)),
            pl.BlockSpec((1, HALO, W_AB), lambda bi, i: (bi, jnp.minimum((i + 1) * hb, nh - 1), 0)),
            whole(aw), whole(ab), whole(lg), whole(lb), whole(bw),
            pl.BlockSpec((1, t, 2 * D_GROUP), lambda bi, i: (bi, i, 0)),
            pl.BlockSpec((1, t, d), lambda bi, i: (bi, i, 0)),
            pl.BlockSpec((1,) + w.shape[1:], lambda bi, i: (layer, 0, 0)),
            pl.BlockSpec((1, d), lambda bi, i: (0, 0)),
        ],
        out_specs=pl.BlockSpec((1, t, d), lambda bi, i: (bi, i, 0)),
        out_shape=jax.ShapeDtypeStruct((b, s, d), f32),
        scratch_shapes=[pltpu.VMEM((t + 2 * HALO, D_GROUP), f32),
                        pltpu.VMEM((t + 2 * HALO, D_GROUP), f32),
                        pltpu.VMEM((SUBLANES - 1, t + 2 * HALO - SUBLANES, D_GROUP), f32),
                        pltpu.VMEM((2, t + 2 * HALO - SUBLANES, D_GROUP), f32)],
        compiler_params=_params(),
        name="conv_out",
    )(proj, proj, proj, aw, ab, lg, lb, bw, ycd, x, w, g)


def _swa_tile(sink_ref, q_ref, qs_ref, z_ref, kv_ref, y_ref, i, layer):
    s_len = kv_ref.shape[1]
    nkeys = 3 * SWA_BLOCK
    b_ = SWA_BLOCK
    f32 = jnp.float32
    lane = lax.broadcasted_iota(jnp.int32, (1, LANES), 1)
    lo = lane < HEAD_DIM
    zero = jnp.zeros((), jnp.bfloat16)
    ones = jnp.ones((nkeys, LANES), jnp.bfloat16)
    for nb in range(T_ATTN // SWA_BLOCK):
        q0 = i * T_ATTN + nb * SWA_BLOCK
        ws = pl.multiple_of(jnp.clip(q0 - SWA_BLOCK, 0, s_len - nkeys), SWA_BLOCK)
        rows = slice(nb * SWA_BLOCK, (nb + 1) * SWA_BLOCK)
        win = pl.ds(ws, nkeys)
        lhs = [
            jnp.where(lo, q_ref[0, rows, 0:LANES], zero),
            jnp.where(lo, qs_ref[0, rows, 0:LANES], zero),
            jnp.where(lo, zero, qs_ref[0, rows, LANES:2 * LANES]),
            jnp.where(lo, zero, q_ref[0, rows, LANES:2 * LANES]),
        ]
        kw = kv_ref[0, win, 0:LANES]
        vw = jnp.concatenate([kv_ref[0, win, LANES:2 * LANES], ones], axis=1)
        qpos = q0 + lax.broadcasted_iota(jnp.int32, (SWA_BLOCK, nkeys), 0)
        kpos = ws + lax.broadcasted_iota(jnp.int32, (SWA_BLOCK, nkeys), 1)
        valid = jnp.abs(kpos - qpos) <= SWA_WINDOW
        num, den = [], []
        for h in range(SWA_HEADS):
            sc = jnp.einsum("qd,kd->qk", lhs[h], kw, preferred_element_type=f32)
            sh = jnp.where(valid, sc, NEG_INF)
            sink = sink_ref[layer, h]
            m = jnp.maximum(jnp.max(sh, axis=-1, keepdims=True), sink)
            e = jnp.exp2(sh - m)
            pv = jnp.dot(e.astype(jnp.bfloat16), vw, preferred_element_type=f32)
            num.append(pv[:, :LANES])
            den.append(pv[:, LANES:] + jnp.exp2(sink - m))
        og0 = jnp.where(lo, num[0], pltpu.roll(num[1], HEAD_DIM, 1)) / jnp.where(lo, den[0], den[1])
        og1 = jnp.where(lo, pltpu.roll(num[2], HEAD_DIM, 1), num[3]) / jnp.where(lo, den[2], den[3])
        y = jnp.concatenate([og0, og1], axis=-1) * z_ref[0, rows, :].astype(f32)
        y_ref[rows, 0:D_GROUP] = y.astype(y_ref.dtype)


def _nbr_tile(q_ref, z_ref, k_ref, v_ref, bias_ref, y_ref, i):
    rows_total = k_ref.shape[1] // GRID_W
    rows_tile = T_ATTN // GRID_W
    nkeys = NA_KH * GRID_W
    f32 = jnp.float32
    lane = lax.broadcasted_iota(jnp.int32, (1, LANES), 1)
    lo = lane < HEAD_DIM
    zero = jnp.zeros((), jnp.bfloat16)
    ones = jnp.ones((nkeys, LANES), jnp.bfloat16)

    for rr in range(rows_tile):
        r = i * rows_tile + rr
        r0 = jnp.clip(r - NA_KH // 2, 0, rows_total - NA_KH)
        d0 = r0 - r + (NA_KH - 1)
        ks = pl.multiple_of(r0 * GRID_W, GRID_W)
        qrow = slice(rr * GRID_W, (rr + 1) * GRID_W)
        outs = []
        for g in range(2):
            cols = slice(g * LANES, (g + 1) * LANES)
            qg = q_ref[0, qrow, cols]
            lhs = jnp.concatenate([jnp.where(lo, qg, zero), jnp.where(lo, zero, qg)], axis=0)
            kw = k_ref[0, pl.ds(ks, nkeys), cols]
            vw = jnp.concatenate([v_ref[0, pl.ds(ks, nkeys), cols], ones], axis=1)
            sc = jnp.einsum("qd,kd->qk", lhs, kw, preferred_element_type=f32)
            bias = jnp.concatenate(
                [jnp.concatenate([bias_ref[0, 2 * g + hh, d0 + 2 * m]
                                  for m in range(NA_KH // 2)], axis=-1)
                 for hh in range(2)], axis=0)
            sc = sc + bias
            m_ = jnp.max(sc, axis=-1, keepdims=True)
            e = jnp.exp2(sc - m_)
            pv = jnp.dot(e.astype(jnp.bfloat16), vw, preferred_element_type=f32)
            pv = jnp.where(jnp.concatenate([lo, lo], axis=1), pv[0:GRID_W], pv[GRID_W:2 * GRID_W])
            outs.append(pv[:, :LANES] / pv[:, LANES:])
        y = jnp.concatenate(outs, axis=-1) * z_ref[0, qrow, :].astype(f32)
        y_ref[qrow, D_GROUP:2 * D_GROUP] = y.astype(y_ref.dtype)


def _attn_kernel(sink_ref, cq_ref, cqs_ref, cz_ref, ckv_ref, dq_ref, dz_ref, dk_ref, dv_ref,
                 bias_ref, o_ref, *, layer):
    i = pl.program_id(1)
    y_ref = o_ref.at[0]
    _swa_tile(sink_ref, cq_ref, cqs_ref, cz_ref, ckv_ref, y_ref, i, layer)
    _nbr_tile(dq_ref, dz_ref, dk_ref, dv_ref, bias_ref, y_ref, i)


def _attn(proj, sink, bias, layer):
    b, s, _ = proj.shape
    t = T_ATTN
    tile = lambda col: pl.BlockSpec((1, t, D_GROUP), lambda bi, i, sk: (bi, i, col // D_GROUP))
    seq = lambda col: pl.BlockSpec((1, s, D_GROUP), lambda bi, i, sk: (bi, 0, col // D_GROUP))
    return pl.pallas_call(
        functools.partial(_attn_kernel, layer=layer),
        grid_spec=pltpu.PrefetchScalarGridSpec(
            num_scalar_prefetch=1,
            grid=(b, s // t),
            in_specs=[
                tile(P_CQ), tile(P_CQS), tile(P_ZC), seq(P_CK),
                tile(P_DQ), tile(P_ZD), seq(P_DK), seq(P_DV),
                pl.BlockSpec((1,) + bias.shape[1:], lambda bi, i, sk: (layer, 0, 0, 0, 0)),
            ],
            out_specs=pl.BlockSpec((1, t, 2 * D_GROUP), lambda bi, i, sk: (bi, i, 0)),
        ),
        out_shape=jax.ShapeDtypeStruct((b, s, 2 * D_GROUP), jnp.bfloat16),
        compiler_params=_params(),
        name="attn",
    )(sink, proj, proj, proj, proj, proj, proj, proj, proj, bias)


def _rope_tables(s):
    inv_freq = ROPE_THETA ** (-jnp.arange(0, HEAD_DIM, 2, dtype=jnp.float32) / HEAD_DIM)
    inv_freq = jnp.tile(inv_freq, LANES // HALF)
    ang = jnp.arange(s, dtype=jnp.float32)[:, None] * inv_freq[None, :]
    return jnp.cos(ang), jnp.sin(ang)


def _nbr_bias_table(rpb):
    c = np.arange(GRID_W)
    c0 = np.clip(c - NA_KW // 2, 0, GRID_W - NA_KW)
    col_ok = (c[None, :] >= c0[:, None]) & (c[None, :] < c0[:, None] + NA_KW)
    dc = np.clip(c[None, :] - c[:, None], -(NA_KW - 1), NA_KW - 1) + (NA_KW - 1)
    onehot = (dc[None] == np.arange(2 * NA_KW - 1)[:, None, None]).astype(np.float32)
    full = jnp.einsum("lhdc,cqk->lhdqk", rpb, jnp.asarray(onehot),
                      precision=lax.Precision.HIGHEST)
    full = jnp.where(jnp.asarray(col_ok), full * LOG2E, NEG_INF)
    pad_rows = lambda a, n: jnp.pad(a, ((0, 0), (0, 0), (0, n), (0, 0), (0, 0)),
                                    constant_values=NEG_INF)
    return jnp.concatenate([pad_rows(full, 1), pad_rows(full[:, :, 1:], 2)], axis=-1)


def kernel(x, norm_g, w_in, w_out, conv_a_w, conv_a_b, ln_a_g, ln_a_b, conv_b_w, swa_sink,
           na_rpb, final_norm_g):
    depth = norm_g.shape[0]
    s = x.shape[1]
    cos_t, sin_t = _rope_tables(s)
    w_in_b = w_in.astype(jnp.bfloat16)
    w_out_b = w_out.astype(jnp.bfloat16)
    bias = _nbr_bias_table(na_rpb)
    sink = swa_sink * LOG2E
    fg = final_norm_g.reshape(1, -1)
    for l in range(depth):
        proj = _in_proj(x, norm_g, w_in_b, cos_t, sin_t, l)
        ycd = _attn(proj, sink, bias, l)
        x = _conv_out(proj, ycd, x, conv_a_w, conv_a_b, ln_a_g, ln_a_b, conv_b_w,
                      w_out_b, fg, l, final_norm=(l == depth - 1))
    return x
```

```python
import functools
import math

import numpy as np
import jax
import jax.numpy as jnp
from jax import lax
from jax.experimental import pallas as pl
from jax.experimental.pallas import tpu as pltpu

D_GROUP = 256
HEAD_DIM = 64
HALF = HEAD_DIM // 2
GRID_W = 64
CONV_A_WIDTH = 31
CONV_A_PAD = (CONV_A_WIDTH - 1) // 2
SWA_WINDOW = 128
SWA_BLOCK = 128
SWA_HEADS = 4
NA_KH = 8
NA_KW = 16
ROPE_THETA = 10000.0
EPS = 1e-6
NEG_INF = -1e30
LOG2E = math.log2(math.e)
Q_SCALE = HEAD_DIM ** -0.5 * LOG2E

COL_A_U, COL_A_V, COL_A_Z = 0, 256, 512
COL_B_B, COL_B_C, COL_B_X, COL_B_Z = 768, 1024, 1280, 1536
COL_C_Q, COL_C_K, COL_C_V, COL_C_Z = 1792, 2048, 2176, 2304
COL_D_Q, COL_D_K, COL_D_V, COL_D_Z = 2560, 2816, 3072, 3328
D_IN = 3584

P_H, P_ZA = 0, 256
P_BB, P_G, P_ZB = 512, 768, 1024
P_CQ, P_CQS, P_CK, P_CV, P_ZC = 1280, 1536, 1792, 1920, 2048
P_DQ, P_DK, P_DV, P_ZD = 2304, 2560, 2816, 3072
D_PROJ = 3328
W_AB = P_CQ

LANES = 128
SUBLANES = 8
HALO = 16
VMEM_LIMIT = 56 * 1024 * 1024

TM_PROJ = 1024
T_MIX = 1024
T_ATTN = 2048
PROJ_CHUNK = 512
CONV_ROWS = 256


def _silu(x):
    return x * jax.nn.sigmoid(x)


def _params():
    return pltpu.CompilerParams(
        dimension_semantics=("parallel", "parallel"), vmem_limit_bytes=VMEM_LIMIT)


def _in_proj_kernel(x_ref, g_ref, w_ref, cos_ref, sin_ref, o_ref, *, layer):
    x = x_ref[0]
    ms = jnp.mean(x * x, axis=-1, keepdims=True)
    h = (x * lax.rsqrt(ms + EPS) * g_ref[layer:layer + 1, :]).astype(jnp.bfloat16)
    lane = lax.broadcasted_iota(jnp.int32, (1, LANES), 1)
    first_half = (lane % HEAD_DIM) < HALF

    def rope(t):
        partner = jnp.where(first_half, -pltpu.roll(t, LANES - HALF, 1), pltpu.roll(t, HALF, 1))
        return t * cos_ref[...] + partner * sin_ref[...]

    def store(col, t):
        o_ref[0, :, col:col + t.shape[1]] = t.astype(o_ref.dtype)

    def chunk(c0):
        return jnp.dot(h, w_ref[0, :, c0:c0 + PROJ_CHUNK], preferred_element_type=jnp.float32)

    g = D_GROUP
    acc = chunk(COL_A_U)
    store(P_H, acc[:, :g] * jax.nn.sigmoid(acc[:, g:]))
    acc = chunk(COL_A_Z)
    store(P_ZA, _silu(acc[:, :g]))
    store(P_BB, acc[:, g:])
    acc = chunk(COL_B_C)
    store(P_G, acc[:, :g] * acc[:, g:])
    acc = chunk(COL_B_Z)
    store(P_ZB, _silu(acc[:, :g]))
    for s0 in range(0, g, LANES):
        t = rope(acc[:, g + s0:g + s0 + LANES]) * Q_SCALE
        store(P_CQ + s0, t)
        store(P_CQS + s0, pltpu.roll(t, HEAD_DIM, 1))
    acc = chunk(COL_C_K)
    store(P_CK, rope(acc[:, :LANES]))
    store(P_CV, acc[:, LANES:g])
    store(P_ZC, _silu(acc[:, g:]))
    acc = chunk(COL_D_Q)
    store(P_DQ, acc[:, :g] * Q_SCALE)
    store(P_DK, acc[:, g:])
    acc = chunk(COL_D_V)
    store(P_DV, acc[:, :g])
    store(P_ZD, _silu(acc[:, g:]))


def _in_proj(x, g, w, cos, sin, layer):
    b, s, d = x.shape
    tm = TM_PROJ
    return pl.pallas_call(
        functools.partial(_in_proj_kernel, layer=layer),
        grid=(b, s // tm),
        in_specs=[
            pl.BlockSpec((1, tm, d), lambda bi, i: (bi, i, 0)),
            pl.BlockSpec(g.shape, lambda bi, i: (0, 0)),
            pl.BlockSpec((1, d, D_IN), lambda bi, i: (layer, 0, 0)),
            pl.BlockSpec((tm, LANES), lambda bi, i: (i, 0)),
            pl.BlockSpec((tm, LANES), lambda bi, i: (i, 0)),
        ],
        out_specs=pl.BlockSpec((1, tm, D_PROJ), lambda bi, i: (bi, i, 0)),
        out_shape=jax.ShapeDtypeStruct((b, s, D_PROJ), jnp.bfloat16),
        compiler_params=_params(),
        name="in_proj",
    )(x, g, w, cos, sin)


def _conv_out_kernel(cur_ref, prev_ref, next_ref, aw_ref, ab_ref, lg_ref, lb_ref, bw_ref,
                     ycd_ref, x_ref, w_ref, g_ref, o_ref, hbuf, gbuf, hph, gph,
                     *, layer, final_norm):
    i = pl.program_id(1)
    n = pl.num_programs(1)
    t = T_MIX
    f32 = jnp.float32
    has_prev = (i > 0).astype(f32)
    has_next = (i < n - 1).astype(f32)
    row = lambda ref, j: ref[layer, j:j + 1, :]
    vec = lambda ref: ref[layer:layer + 1, :]
    col = lambda ref, rows, c: ref[0, rows, c:c + D_GROUP].astype(f32)

    full = slice(None)
    hwin = jnp.concatenate([col(prev_ref, full, P_H) * has_prev, col(cur_ref, full, P_H),
                            col(next_ref, full, P_H) * has_next], axis=0)
    gwin = jnp.concatenate([col(prev_ref, full, P_G) * has_prev, col(cur_ref, full, P_G),
                            col(next_ref, full, P_G) * has_next], axis=0)
    hbuf[...] = hwin
    gbuf[...] = gwin

    nwin = t + 2 * HALO
    nph = nwin - SUBLANES
    for p in range(1, SUBLANES):
        hph[p - 1] = pltpu.roll(hwin, nwin - p, 0)[0:nph]
    gph[0] = pltpu.roll(gwin, nwin - 1, 0)[0:nph]
    gph[1] = pltpu.roll(gwin, nwin - (SUBLANES - 1), 0)[0:nph]

    def tap(off, rows):
        p = off % SUBLANES
        if p == 0:
            return hbuf[off:off + rows, :]
        return hph[p - 1, off - p:off - p + rows, :]

    rc = CONV_ROWS
    for r0 in range(0, t, rc):
        rows = slice(r0, r0 + rc)
        acc = jnp.broadcast_to(vec(ab_ref), (rc, D_GROUP))
        for j in range(CONV_A_WIDTH):
            acc = acc + aw_ref[layer, j:j + 1, :] * tap(HALO - CONV_A_PAD + j + r0, rc)
        mu = jnp.mean(acc, axis=-1, keepdims=True)
        xc = acc - mu
        var = jnp.mean(xc * xc, axis=-1, keepdims=True)
        hn = xc * lax.rsqrt(var + EPS) * vec(lg_ref) + vec(lb_ref)
        ya = _silu(hn) * col(cur_ref, rows, P_ZA)
        gm = HALO + r0 - SUBLANES
        conv = (row(bw_ref, 0) * gph[1, gm:gm + rc, :]
                + row(bw_ref, 1) * gbuf[HALO + r0:HALO + r0 + rc, :]
                + row(bw_ref, 2) * gph[0, HALO + r0:HALO + r0 + rc, :])
        yb = col(cur_ref, rows, P_BB) * conv * col(cur_ref, rows, P_ZB)
        yab = jnp.concatenate([ya.astype(jnp.bfloat16), yb.astype(jnp.bfloat16)], axis=1)
        xo = x_ref[0, rows, :] + jnp.dot(ycd_ref[0, rows, :], w_ref[0, 2 * D_GROUP:4 * D_GROUP, :],
                                         preferred_element_type=f32)
        xo = xo + jnp.dot(yab, w_ref[0, 0:2 * D_GROUP, :], preferred_element_type=f32)
        if final_norm:
            ms = jnp.mean(xo * xo, axis=-1, keepdims=True)
            xo = xo * lax.rsqrt(ms + EPS) * g_ref[...]
        o_ref[0, rows, :] = xo


def _conv_out(proj, ycd, x, aw, ab, lg, lb, bw, w, g, layer, final_norm):
    b, s, d = x.shape
    t = T_MIX
    hb = t // HALO
    nh = s // HALO
    f32 = jnp.float32
    whole = lambda a: pl.BlockSpec(a.shape, lambda bi, i: (0,) * a.ndim)
    return pl.pallas_call(
        functools.partial(_conv_out_kernel, layer=layer, final_norm=final_norm),
        grid=(b, s // t),
        in_specs=[
            pl.BlockSpec((1, t, W_AB), lambda bi, i: (bi, i, 0)),
            pl.BlockSpec((1, HALO, W_AB), lambda bi, i: (bi, jnp.maximum(i * hb - 1, 0), 0)),
            pl.BlockSpec((1, HALO, W_AB), lambda bi, i: (bi, jnp.minimum((i + 1) * hb, nh - 1), 0)),
            whole(aw), whole(ab), whole(lg), whole(lb), whole(bw),
            pl.BlockSpec((1, t, 2 * D_GROUP), lambda bi, i: (bi, i, 0)),
            pl.BlockSpec((1, t, d), lambda bi, i: (bi, i, 0)),
            pl.BlockSpec((1,) + w.shape[1:], lambda bi, i: (layer, 0, 0)),
            pl.BlockSpec((1, d), lambda bi, i: (0, 0)),
        ],
        out_specs=pl.BlockSpec((1, t, d), lambda bi, i: (bi, i, 0)),
        out_shape=jax.ShapeDtypeStruct((b, s, d), f32),
        scratch_shapes=[pltpu.VMEM((t + 2 * HALO, D_GROUP), f32),
                        pltpu.VMEM((t + 2 * HALO, D_GROUP), f32),
                        pltpu.VMEM((SUBLANES - 1, t + 2 * HALO - SUBLANES, D_GROUP), f32),
                        pltpu.VMEM((2, t + 2 * HALO - SUBLANES, D_GROUP), f32)],
        compiler_params=_params(),
        name="conv_out",
    )(proj, proj, proj, aw, ab, lg, lb, bw, ycd, x, w, g)


def _swa_tile(sink_ref, q_ref, qs_ref, z_ref, kv_ref, y_ref, i, layer):
    s_len = kv_ref.shape[1]
    nkeys = 3 * SWA_BLOCK
    f32 = jnp.float32
    lane = lax.broadcasted_iota(jnp.int32, (1, LANES), 1)
    lo = lane < HEAD_DIM
    zero = jnp.zeros((), jnp.bfloat16)
    ones = jnp.ones((nkeys, LANES), jnp.bfloat16)
    for nb in range(T_ATTN // SWA_BLOCK):
        q0 = i * T_ATTN + nb * SWA_BLOCK
        ws = pl.multiple_of(jnp.clip(q0 - SWA_BLOCK, 0, s_len - nkeys), SWA_BLOCK)
        rows = slice(nb * SWA_BLOCK, (nb + 1) * SWA_BLOCK)
        win = pl.ds(ws, nkeys)
        lhs = [
            jnp.where(lo, q_ref[0, rows, 0:LANES], zero),
            jnp.where(lo, qs_ref[0, rows, 0:LANES], zero),
            jnp.where(lo, zero, qs_ref[0, rows, LANES:2 * LANES]),
            jnp.where(lo, zero, q_ref[0, rows, LANES:2 * LANES]),
        ]
        kw = kv_ref[0, win, 0:LANES]
        vw = jnp.concatenate([kv_ref[0, win, LANES:2 * LANES], ones], axis=1)
        qpos = q0 + lax.broadcasted_iota(jnp.int32, (SWA_BLOCK, nkeys), 0)
        kpos = ws + lax.broadcasted_iota(jnp.int32, (SWA_BLOCK, nkeys), 1)
        valid = jnp.abs(kpos - qpos) <= SWA_WINDOW
        num, den = [], []
        for h in range(SWA_HEADS):
            sc = jnp.einsum("qd,kd->qk", lhs[h], kw, preferred_element_type=f32)
            sh = jnp.where(valid, sc, NEG_INF)
            sink = sink_ref[layer, h]
            m = jnp.maximum(jnp.max(sh, axis=-1, keepdims=True), sink)
            e = jnp.exp2(sh - m)
            pv = jnp.dot(e.astype(jnp.bfloat16), vw, preferred_element_type=f32)
            num.append(pv[:, :LANES])
            den.append(pv[:, LANES:] + jnp.exp2(sink - m))
        og0 = jnp.where(lo, num[0], pltpu.roll(num[1], HEAD_DIM, 1)) / jnp.where(lo, den[0], den[1])
        og1 = jnp.where(lo, pltpu.roll(num[2], HEAD_DIM, 1), num[3]) / jnp.where(lo, den[2], den[3])
        y = jnp.concatenate([og0, og1], axis=-1) * z_ref[0, rows, :].astype(f32)
        y_ref[rows, 0:D_GROUP] = y.astype(y_ref.dtype)


def _nbr_tile(q_ref, z_ref, k_ref, v_ref, bias_ref, y_ref, i):
    rows_total = k_ref.shape[1] // GRID_W
    rows_tile = T_ATTN // GRID_W
    nkeys = NA_KH * GRID_W
    f32 = jnp.float32
    lane = lax.broadcasted_iota(jnp.int32, (1, LANES), 1)
    lo = lane < HEAD_DIM
    zero = jnp.zeros((), jnp.bfloat16)
    ones = jnp.ones((nkeys, LANES), jnp.bfloat16)

    for rr in range(rows_tile):
        r = i * rows_tile + rr
        r0 = jnp.clip(r - NA_KH // 2, 0, rows_total - NA_KH)
        d0 = r0 - r + (NA_KH - 1)
        ks = pl.multiple_of(r0 * GRID_W, GRID_W)
        qrow = slice(rr * GRID_W, (rr + 1) * GRID_W)
        outs = []
        for g in range(2):
            cols = slice(g * LANES, (g + 1) * LANES)
            qg = q_ref[0, qrow, cols]
            lhs = jnp.concatenate([jnp.where(lo, qg, zero), jnp.where(lo, zero, qg)], axis=0)
            kw = k_ref[0, pl.ds(ks, nkeys), cols]
            vw = jnp.concatenate([v_ref[0, pl.ds(ks, nkeys), cols], ones], axis=1)
            sc = jnp.einsum("qd,kd->qk", lhs, kw, preferred_element_type=f32)
            bias = jnp.concatenate(
                [jnp.concatenate([bias_ref[0, 2 * g + hh, d0 + 2 * m]
                                  for m in range(NA_KH // 2)], axis=-1)
                 for hh in range(2)], axis=0)
            sc = sc + bias
            m_ = jnp.max(sc, axis=-1, keepdims=True)
            e = jnp.exp2(sc - m_)
            pv = jnp.dot(e.astype(jnp.bfloat16), vw, preferred_element_type=f32)
            pv = jnp.where(jnp.concatenate([lo, lo], axis=1), pv[0:GRID_W], pv[GRID_W:2 * GRID_W])
            outs.append(pv[:, :LANES] / pv[:, LANES:])
        y = jnp.concatenate(outs, axis=-1) * z_ref[0, qrow, :].astype(f32)
        y_ref[qrow, D_GROUP:2 * D_GROUP] = y.astype(y_ref.dtype)


def _attn_kernel(sink_ref, cq_ref, cqs_ref, cz_ref, ckv_ref, dq_ref, dz_ref, dk_ref, dv_ref,
                 bias_ref, o_ref, *, layer):
    i = pl.program_id(1)
    y_ref = o_ref.at[0]
    _swa_tile(sink_ref, cq_ref, cqs_ref, cz_ref, ckv_ref, y_ref, i, layer)
    _nbr_tile(dq_ref, dz_ref, dk_ref, dv_ref, bias_ref, y_ref, i)


def _attn(proj, sink, bias, layer):
    b, s, _ = proj.shape
    t = T_ATTN
    tile = lambda col: pl.BlockSpec((1, t, D_GROUP), lambda bi, i, sk: (bi, i, col // D_GROUP))
    seq = lambda col: pl.BlockSpec((1, s, D_GROUP), lambda bi, i, sk: (bi, 0, col // D_GROUP))
    return pl.pallas_call(
        functools.partial(_attn_kernel, layer=layer),
        grid_spec=pltpu.PrefetchScalarGridSpec(
            num_scalar_prefetch=1,
            grid=(b, s // t),
            in_specs=[
                tile(P_CQ), tile(P_CQS), tile(P_ZC), seq(P_CK),
                tile(P_DQ), tile(P_ZD), seq(P_DK), seq(P_DV),
                pl.BlockSpec((1,) + bias.shape[1:], lambda bi, i, sk: (layer, 0, 0, 0, 0)),
            ],
            out_specs=pl.BlockSpec((1, t, 2 * D_GROUP), lambda bi, i, sk: (bi, i, 0)),
        ),
        out_shape=jax.ShapeDtypeStruct((b, s, 2 * D_GROUP), jnp.bfloat16),
        compiler_params=_params(),
        name="attn",
    )(sink, proj, proj, proj, proj, proj, proj, proj, proj, bias)


def _rope_tables(s):
    inv_freq = ROPE_THETA ** (-jnp.arange(0, HEAD_DIM, 2, dtype=jnp.float32) / HEAD_DIM)
    inv_freq = jnp.tile(inv_freq, LANES // HALF)
    ang = jnp.arange(s, dtype=jnp.float32)[:, None] * inv_freq[None, :]
    return jnp.cos(ang), jnp.sin(ang)


def _nbr_bias_table(rpb):
    c = np.arange(GRID_W)
    c0 = np.clip(c - NA_KW // 2, 0, GRID_W - NA_KW)
    col_ok = (c[None, :] >= c0[:, None]) & (c[None, :] < c0[:, None] + NA_KW)
    dc = np.clip(c[None, :] - c[:, None], -(NA_KW - 1), NA_KW - 1) + (NA_KW - 1)
    onehot = (dc[None] == np.arange(2 * NA_KW - 1)[:, None, None]).astype(np.float32)
    full = jnp.einsum("lhdc,cqk->lhdqk", rpb, jnp.asarray(onehot),
                      precision=lax.Precision.HIGHEST)
    full = jnp.where(jnp.asarray(col_ok), full * LOG2E, NEG_INF)
    pad_rows = lambda a, n: jnp.pad(a, ((0, 0), (0, 0), (0, n), (0, 0), (0, 0)),
                                    constant_values=NEG_INF)
    return jnp.concatenate([pad_rows(full, 1), pad_rows(full[:, :, 1:], 2)], axis=-1)


def kernel(x, norm_g, w_in, w_out, conv_a_w, conv_a_b, ln_a_g, ln_a_b, conv_b_w, swa_sink,
           na_rpb, final_norm_g):
    depth = norm_g.shape[0]
    s = x.shape[1]
    assert s % max(TM_PROJ, T_MIX, T_ATTN) == 0 and s % GRID_W == 0, x.shape
    assert s // GRID_W >= NA_KH and s >= 3 * SWA_BLOCK, x.shape
    assert w_in.shape[1:] == (x.shape[2], D_IN) and w_out.shape[1:] == (4 * D_GROUP, x.shape[2])
    cos_t, sin_t = _rope_tables(s)
    w_in_b = w_in.astype(jnp.bfloat16)
    w_out_b = w_out.astype(jnp.bfloat16)
    bias = _nbr_bias_table(na_rpb)
    sink = swa_sink * LOG2E
    fg = final_norm_g.reshape(1, -1)
    for l in range(depth):
        proj = _in_proj(x, norm_g, w_in_b, cos_t, sin_t, l)
        ycd = _attn(proj, sink, bias, l)
        x = _conv_out(proj, ycd, x, conv_a_w, conv_a_b, ln_a_g, ln_a_b, conv_b_w,
                      w_out_b, fg, l, final_norm=(l == depth - 1))
    return x
```

```python
import functools
import math

import numpy as np
import jax
import jax.numpy as jnp
from jax import lax
from jax.experimental import pallas as pl
from jax.experimental.pallas import tpu as pltpu

D_GROUP = 256
HEAD_DIM = 64
HALF = HEAD_DIM // 2
GRID_W = 64
CONV_A_WIDTH = 31
CONV_A_PAD = (CONV_A_WIDTH - 1) // 2
SWA_WINDOW = 128
SWA_BLOCK = 128
SWA_HEADS = 4
NA_KH = 8
NA_KW = 16
ROPE_THETA = 10000.0
EPS = 1e-6
NEG_INF = -1e30
LOG2E = math.log2(math.e)
Q_SCALE = HEAD_DIM ** -0.5 * LOG2E

COL_A_U, COL_A_V, COL_A_Z = 0, 256, 512
COL_B_B, COL_B_C, COL_B_X, COL_B_Z = 768, 1024, 1280, 1536
COL_C_Q, COL_C_K, COL_C_V, COL_C_Z = 1792, 2048, 2176, 2304
COL_D_Q, COL_D_K, COL_D_V, COL_D_Z = 2560, 2816, 3072, 3328
D_IN = 3584

P_H, P_ZA = 0, 256
P_BB, P_G, P_ZB = 512, 768, 1024
P_CQ, P_CQS, P_CK, P_CV, P_ZC = 1280, 1536, 1792, 1920, 2048
P_DQ, P_DK, P_DV, P_ZD = 2304, 2560, 2816, 3072
D_PROJ = 3328
W_AB = P_CQ

LANES = 128
SUBLANES = 8
HALO = 16
VMEM_LIMIT = 56 * 1024 * 1024

TM_PROJ = 1024
T_MIX = 1024
T_ATTN = 1024
PROJ_CHUNK = 512
CONV_ROWS = 256


def _silu(x):
    return x * jax.nn.sigmoid(x)


def _params():
    return pltpu.CompilerParams(
        dimension_semantics=("parallel", "parallel"), vmem_limit_bytes=VMEM_LIMIT)


def _in_proj_kernel(x_ref, g_ref, w_ref, cos_ref, sin_ref, o_ref, *, layer):
    x = x_ref[0]
    ms = jnp.mean(x * x, axis=-1, keepdims=True)
    h = (x * lax.rsqrt(ms + EPS) * g_ref[layer:layer + 1, :]).astype(jnp.bfloat16)
    lane = lax.broadcasted_iota(jnp.int32, (1, LANES), 1)
    first_half = (lane % HEAD_DIM) < HALF

    def rope(t):
        partner = jnp.where(first_half, -pltpu.roll(t, LANES - HALF, 1), pltpu.roll(t, HALF, 1))
        return t * cos_ref[...] + partner * sin_ref[...]

    def store(col, t):
        o_ref[0, :, col:col + t.shape[1]] = t.astype(o_ref.dtype)

    def chunk(c0):
        return jnp.dot(h, w_ref[0, :, c0:c0 + PROJ_CHUNK], preferred_element_type=jnp.float32)

    g = D_GROUP
    acc = chunk(COL_A_U)
    store(P_H, acc[:, :g] * jax.nn.sigmoid(acc[:, g:]))
    acc = chunk(COL_A_Z)
    store(P_ZA, _silu(acc[:, :g]))
    store(P_BB, acc[:, g:])
    acc = chunk(COL_B_C)
    store(P_G, acc[:, :g] * acc[:, g:])
    acc = chunk(COL_B_Z)
    store(P_ZB, _silu(acc[:, :g]))
    for s0 in range(0, g, LANES):
        t = rope(acc[:, g + s0:g + s0 + LANES]) * Q_SCALE
        store(P_CQ + s0, t)
        store(P_CQS + s0, pltpu.roll(t, HEAD_DIM, 1))
    acc = chunk(COL_C_K)
    store(P_CK, rope(acc[:, :LANES]))
    store(P_CV, acc[:, LANES:g])
    store(P_ZC, _silu(acc[:, g:]))
    acc = chunk(COL_D_Q)
    store(P_DQ, acc[:, :g] * Q_SCALE)
    store(P_DK, acc[:, g:])
    acc = chunk(COL_D_V)
    store(P_DV, acc[:, :g])
    store(P_ZD, _silu(acc[:, g:]))


def _in_proj(x, g, w, cos, sin, layer):
    b, s, d = x.shape
    tm = TM_PROJ
    return pl.pallas_call(
        functools.partial(_in_proj_kernel, layer=layer),
        grid=(b, s // tm),
        in_specs=[
            pl.BlockSpec((1, tm, d), lambda bi, i: (bi, i, 0)),
            pl.BlockSpec(g.shape, lambda bi, i: (0, 0)),
            pl.BlockSpec((1, d, D_IN), lambda bi, i: (layer, 0, 0)),
            pl.BlockSpec((tm, LANES), lambda bi, i: (i, 0)),
            pl.BlockSpec((tm, LANES), lambda bi, i: (i, 0)),
        ],
        out_specs=pl.BlockSpec((1, tm, D_PROJ), lambda bi, i: (bi, i, 0)),
        out_shape=jax.ShapeDtypeStruct((b, s, D_PROJ), jnp.bfloat16),
        compiler_params=_params(),
        name="in_proj",
    )(x, g, w, cos, sin)


def _conv_out_kernel(cur_ref, prev_ref, next_ref, aw_ref, ab_ref, lg_ref, lb_ref, bw_ref,
                     ycd_ref, x_ref, w_ref, g_ref, o_ref, hbuf, gbuf, hph, gph,
                     *, layer, final_norm):
    i = pl.program_id(1)
    n = pl.num_programs(1)
    t = T_MIX
    f32 = jnp.float32
    has_prev = (i > 0).astype(f32)
    has_next = (i < n - 1).astype(f32)
    row = lambda ref, j: ref[layer, j:j + 1, :]
    vec = lambda ref: ref[layer:layer + 1, :]
    col = lambda ref, rows, c: ref[0, rows, c:c + D_GROUP].astype(f32)

    full = slice(None)
    hwin = jnp.concatenate([col(prev_ref, full, P_H) * has_prev, col(cur_ref, full, P_H),
                            col(next_ref, full, P_H) * has_next], axis=0)
    gwin = jnp.concatenate([col(prev_ref, full, P_G) * has_prev, col(cur_ref, full, P_G),
                            col(next_ref, full, P_G) * has_next], axis=0)
    hbuf[...] = hwin
    gbuf[...] = gwin

    nwin = t + 2 * HALO
    nph = nwin - SUBLANES
    for p in range(1, SUBLANES):
        hph[p - 1] = pltpu.roll(hwin, nwin - p, 0)[0:nph]
    gph[0] = pltpu.roll(gwin, nwin - 1, 0)[0:nph]
    gph[1] = pltpu.roll(gwin, nwin - (SUBLANES - 1), 0)[0:nph]

    def tap(off, rows):
        p = off % SUBLANES
        if p == 0:
            return hbuf[off:off + rows, :]
        return hph[p - 1, off - p:off - p + rows, :]

    rc = CONV_ROWS
    for r0 in range(0, t, rc):
        rows = slice(r0, r0 + rc)
        acc = jnp.broadcast_to(vec(ab_ref), (rc, D_GROUP))
        for j in range(CONV_A_WIDTH):
            acc = acc + aw_ref[layer, j:j + 1, :] * tap(HALO - CONV_A_PAD + j + r0, rc)
        mu = jnp.mean(acc, axis=-1, keepdims=True)
        xc = acc - mu
        var = jnp.mean(xc * xc, axis=-1, keepdims=True)
        hn = xc * lax.rsqrt(var + EPS) * vec(lg_ref) + vec(lb_ref)
        ya = _silu(hn) * col(cur_ref, rows, P_ZA)
        gm = HALO + r0 - SUBLANES
        conv = (row(bw_ref, 0) * gph[1, gm:gm + rc, :]
                + row(bw_ref, 1) * gbuf[HALO + r0:HALO + r0 + rc, :]
                + row(bw_ref, 2) * gph[0, HALO + r0:HALO + r0 + rc, :])
        yb = col(cur_ref, rows, P_BB) * conv * col(cur_ref, rows, P_ZB)
        yab = jnp.concatenate([ya.astype(jnp.bfloat16), yb.astype(jnp.bfloat16)], axis=1)
        xo = x_ref[0, rows, :] + jnp.dot(ycd_ref[0, rows, :], w_ref[0, 2 * D_GROUP:4 * D_GROUP, :],
                                         preferred_element_type=f32)
        xo = xo + jnp.dot(yab, w_ref[0, 0:2 * D_GROUP, :], preferred_element_type=f32)
        if final_norm:
            ms = jnp.mean(xo * xo, axis=-1, keepdims=True)
            xo = xo * lax.rsqrt(ms + EPS) * g_ref[...]
        o_ref[0, rows, :] = xo


def _conv_out(proj, ycd, x, aw, ab, lg, lb, bw, w, g, layer, final_norm):
    b, s, d = x.shape
    t = T_MIX
    hb = t // HALO
    nh = s // HALO
    f32 = jnp.float32
    whole = lambda a: pl.BlockSpec(a.shape, lambda bi, i: (0,) * a.ndim)
    return pl.pallas_call(
        functools.partial(_conv_out_kernel, layer=layer, final_norm=final_norm),
        grid=(b, s // t),
        in_specs=[
            pl.BlockSpec((1, t, W_AB), lambda bi, i: (bi, i, 0)),
            pl.BlockSpec((1, HALO, W_AB), lambda bi, i: (bi, jnp.maximum(i * hb - 1, 0), 0)),
            pl.BlockSpec((1, HALO, W_AB), lambda bi, i: (bi, jnp.minimum((i + 1) * hb, nh - 1), 0)),
            whole(aw), whole(ab), whole(lg), whole(lb), whole(bw),
            pl.BlockSpec((1, t, 2 * D_GROUP), lambda bi, i: (bi, i, 0)),
            pl.BlockSpec((1, t, d), lambda bi, i: (bi, i, 0)),
            pl.BlockSpec((1,) + w.shape[1:], lambda bi, i: (layer, 0, 0)),
            pl.BlockSpec((1, d), lambda bi, i: (0, 0)),
        ],
        out_specs=pl.BlockSpec((1, t, d), lambda bi, i: (bi, i, 0)),
        out_shape=jax.ShapeDtypeStruct((b, s, d), f32),
        scratch_shapes=[pltpu.VMEM((t + 2 * HALO, D_GROUP), f32),
                        pltpu.VMEM((t + 2 * HALO, D_GROUP), f32),
                        pltpu.VMEM((SUBLANES - 1, t + 2 * HALO - SUBLANES, D_GROUP), f32),
                        pltpu.VMEM((2, t + 2 * HALO - SUBLANES, D_GROUP), f32)],
        compiler_params=_params(),
        name="conv_out",
    )(proj, proj, proj, aw, ab, lg, lb, bw, ycd, x, w, g)


def _swa_tile(sink_ref, q_ref, qs_ref, z_ref, kv_ref, y_ref, i, layer):
    s_len = kv_ref.shape[1]
    nkeys = 3 * SWA_BLOCK
    f32 = jnp.float32
    lane = lax.broadcasted_iota(jnp.int32, (1, LANES), 1)
    lo = lane < HEAD_DIM
    zero = jnp.zeros((), jnp.bfloat16)
    ones = jnp.ones((nkeys, LANES), jnp.bfloat16)
    for nb in range(T_ATTN // SWA_BLOCK):
        q0 = i * T_ATTN + nb * SWA_BLOCK
        ws = pl.multiple_of(jnp.clip(q0 - SWA_BLOCK, 0, s_len - nkeys), SWA_BLOCK)
        rows = slice(nb * SWA_BLOCK, (nb + 1) * SWA_BLOCK)
        win = pl.ds(ws, nkeys)
        lhs = [
            jnp.where(lo, q_ref[0, rows, 0:LANES], zero),
            jnp.where(lo, qs_ref[0, rows, 0:LANES], zero),
            jnp.where(lo, zero, qs_ref[0, rows, LANES:2 * LANES]),
            jnp.where(lo, zero, q_ref[0, rows, LANES:2 * LANES]),
        ]
        kw = kv_ref[0, win, 0:LANES]
        vw = jnp.concatenate([kv_ref[0, win, LANES:2 * LANES], ones], axis=1)
        qpos = q0 + lax.broadcasted_iota(jnp.int32, (SWA_BLOCK, nkeys), 0)
        kpos = ws + lax.broadcasted_iota(jnp.int32, (SWA_BLOCK, nkeys), 1)
        valid = jnp.abs(kpos - qpos) <= SWA_WINDOW
        num, den = [], []
        for h in range(SWA_HEADS):
            sc = jnp.einsum("qd,kd->qk", lhs[h], kw, preferred_element_type=f32)
            sh = jnp.where(valid, sc, NEG_INF)
            sink = sink_ref[layer, h]
            m = jnp.maximum(jnp.max(sh, axis=-1, keepdims=True), sink)
            e = jnp.exp2(sh - m)
            pv = jnp.dot(e.astype(jnp.bfloat16), vw, preferred_element_type=f32)
            num.append(pv[:, :LANES])
            den.append(pv[:, LANES:] + jnp.exp2(sink - m))
        og0 = jnp.where(lo, num[0], pltpu.roll(num[1], HEAD_DIM, 1)) / jnp.where(lo, den[0], den[1])
        og1 = jnp.where(lo, pltpu.roll(num[2], HEAD_DIM, 1), num[3]) / jnp.where(lo, den[2], den[3])
        y = jnp.concatenate([og0, og1], axis=-1) * z_ref[0, rows, :].astype(f32)
        y_ref[rows, 0:D_GROUP] = y.astype(y_ref.dtype)


def _nbr_tile(q_ref, z_ref, k_ref, v_ref, bias_ref, y_ref, i):
    rows_total = k_ref.shape[1] // GRID_W
    rows_tile = T_ATTN // GRID_W
    nkeys = NA_KH * GRID_W
    f32 = jnp.float32
    lane = lax.broadcasted_iota(jnp.int32, (1, LANES), 1)
    lo = lane < HEAD_DIM
    zero = jnp.zeros((), jnp.bfloat16)
    ones = jnp.ones((nkeys, LANES), jnp.bfloat16)

    for rr in range(rows_tile):
        r = i * rows_tile + rr
        r0 = jnp.clip(r - NA_KH // 2, 0, rows_total - NA_KH)
        d0 = r0 - r + (NA_KH - 1)
        ks = pl.multiple_of(r0 * GRID_W, GRID_W)
        qrow = slice(rr * GRID_W, (rr + 1) * GRID_W)
        outs = []
        for g in range(2):
            cols = slice(g * LANES, (g + 1) * LANES)
            qg = q_ref[0, qrow, cols]
            lhs = jnp.concatenate([jnp.where(lo, qg, zero), jnp.where(lo, zero, qg)], axis=0)
            kw = k_ref[0, pl.ds(ks, nkeys), cols]
            vw = jnp.concatenate([v_ref[0, pl.ds(ks, nkeys), cols], ones], axis=1)
            sc = jnp.einsum("qd,kd->qk", lhs, kw, preferred_element_type=f32)
            bias = jnp.concatenate(
                [jnp.concatenate([bias_ref[0, 2 * g + hh, d0 + 2 * m]
                                  for m in range(NA_KH // 2)], axis=-1)
                 for hh in range(2)], axis=0)
            sc = sc + bias
            m_ = jnp.max(sc, axis=-1, keepdims=True)
            e = jnp.exp2(sc - m_)
            pv = jnp.dot(e.astype(jnp.bfloat16), vw, preferred_element_type=f32)
            pv = jnp.where(jnp.concatenate([lo, lo], axis=1), pv[0:GRID_W], pv[GRID_W:2 * GRID_W])
            outs.append(pv[:, :LANES] / pv[:, LANES:])
        y = jnp.concatenate(outs, axis=-1) * z_ref[0, qrow, :].astype(f32)
        y_ref[qrow, D_GROUP:2 * D_GROUP] = y.astype(y_ref.dtype)


def _attn_kernel(sink_ref, cq_ref, cqs_ref, cz_ref, ckv_ref, dq_ref, dz_ref, dk_ref, dv_ref,
                 bias_ref, o_ref, *, layer):
    i = pl.program_id(1)
    y_ref = o_ref.at[0]
    _swa_tile(sink_ref, cq_ref, cqs_ref, cz_ref, ckv_ref, y_ref, i, layer)
    _nbr_tile(dq_ref, dz_ref, dk_ref, dv_ref, bias_ref, y_ref, i)


def _attn(proj, sink, bias, layer):
    b, s, _ = proj.shape
    t = T_ATTN
    tile = lambda col: pl.BlockSpec((1, t, D_GROUP), lambda bi, i, sk: (bi, i, col // D_GROUP))
    seq = lambda col: pl.BlockSpec((1, s, D_GROUP), lambda bi, i, sk: (bi, 0, col // D_GROUP))
    return pl.pallas_call(
        functools.partial(_attn_kernel, layer=layer),
        grid_spec=pltpu.PrefetchScalarGridSpec(
            num_scalar_prefetch=1,
            grid=(b, s // t),
            in_specs=[
                tile(P_CQ), tile(P_CQS), tile(P_ZC), seq(P_CK),
                tile(P_DQ), tile(P_ZD), seq(P_DK), seq(P_DV),
                pl.BlockSpec((1,) + bias.shape[1:], lambda bi, i, sk: (layer, 0, 0, 0, 0)),
            ],
            out_specs=pl.BlockSpec((1, t, 2 * D_GROUP), lambda bi, i, sk: (bi, i, 0)),
        ),
        out_shape=jax.ShapeDtypeStruct((b, s, 2 * D_GROUP), jnp.bfloat16),
        compiler_params=_params(),
        name="attn",
    )(sink, proj, proj, proj, proj, proj, proj, proj, proj, bias)


def _rope_tables(s):
    inv_freq = ROPE_THETA ** (-jnp.arange(0, HEAD_DIM, 2, dtype=jnp.float32) / HEAD_DIM)
    inv_freq = jnp.tile(inv_freq, LANES // HALF)
    ang = jnp.arange(s, dtype=jnp.float32)[:, None] * inv_freq[None, :]
    return jnp.cos(ang), jnp.sin(ang)


def _nbr_bias_table(rpb):
    c = np.arange(GRID_W)
    c0 = np.clip(c - NA_KW // 2, 0, GRID_W - NA_KW)
    col_ok = (c[None, :] >= c0[:, None]) & (c[None, :] < c0[:, None] + NA_KW)
    dc = np.clip(c[None, :] - c[:, None], -(NA_KW - 1), NA_KW - 1) + (NA_KW - 1)
    onehot = (dc[None] == np.arange(2 * NA_KW - 1)[:, None, None]).astype(np.float32)
    full = jnp.einsum("lhdc,cqk->lhdqk", rpb, jnp.asarray(onehot),
                      precision=lax.Precision.HIGHEST)
    full = jnp.where(jnp.asarray(col_ok), full * LOG2E, NEG_INF)
    pad_rows = lambda a, n: jnp.pad(a, ((0, 0), (0, 0), (0, n), (0, 0), (0, 0)),
                                    constant_values=NEG_INF)
    return jnp.concatenate([pad_rows(full, 1), pad_rows(full[:, :, 1:], 2)], axis=-1)


def kernel(x, norm_g, w_in, w_out, conv_a_w, conv_a_b, ln_a_g, ln_a_b, conv_b_w, swa_sink,
           na_rpb, final_norm_g):
    depth = norm_g.shape[0]
    s = x.shape[1]
    assert s % max(TM_PROJ, T_MIX, T_ATTN) == 0 and s % GRID_W == 0, x.shape
    assert s // GRID_W >= NA_KH and s >= 3 * SWA_BLOCK, x.shape
    assert w_in.shape[1:] == (x.shape[2], D_IN) and w_out.shape[1:] == (4 * D_GROUP, x.shape[2])
    cos_t, sin_t = _rope_tables(s)
    w_in_b = w_in.astype(jnp.bfloat16)
    w_out_b = w_out.astype(jnp.bfloat16)
    bias = _nbr_bias_table(na_rpb)
    sink = swa_sink * LOG2E
    fg = final_norm_g.reshape(1, -1)
    for l in range(depth):
        proj = _in_proj(x, norm_g, w_in_b, cos_t, sin_t, l)
        ycd = _attn(proj, sink, bias, l)
        x = _conv_out(proj, ycd, x, conv_a_w, conv_a_b, ln_a_g, ln_a_b, conv_b_w,
                      w_out_b, fg, l, final_norm=(l == depth - 1))
    return x
```

```python
import functools
import math

import numpy as np
import jax
import jax.numpy as jnp
from jax import lax
from jax.experimental import pallas as pl
from jax.experimental.pallas import tpu as pltpu

D_GROUP = 256
HEAD_DIM = 64
HALF = HEAD_DIM // 2
GRID_W = 64
CONV_A_WIDTH = 31
CONV_A_PAD = (CONV_A_WIDTH - 1) // 2
SWA_WINDOW = 128
SWA_BLOCK = 128
SWA_HEADS = 4
NA_KH = 8
NA_KW = 16
ROPE_THETA = 10000.0
EPS = 1e-6
NEG_INF = -1e30
LOG2E = math.log2(math.e)
Q_SCALE = HEAD_DIM ** -0.5 * LOG2E

COL_A_U, COL_A_V, COL_A_Z = 0, 256, 512
COL_B_B, COL_B_C, COL_B_X, COL_B_Z = 768, 1024, 1280, 1536
COL_C_Q, COL_C_K, COL_C_V, COL_C_Z = 1792, 2048, 2176, 2304
COL_D_Q, COL_D_K, COL_D_V, COL_D_Z = 2560, 2816, 3072, 3328
D_IN = 3584

P_H, P_ZA = 0, 256
P_BB, P_G, P_ZB = 512, 768, 1024
P_CQ, P_CQS, P_CK, P_CV, P_ZC = 1280, 1536, 1792, 1920, 2048
P_DQ, P_DK, P_DV, P_ZD = 2304, 2560, 2816, 3072
D_PROJ = 3328
W_AB = P_CQ

LANES = 128
SUBLANES = 8
HALO = 16
VMEM_LIMIT = 56 * 1024 * 1024

TM_PROJ = 1024
T_MIX = 1024
T_ATTN = 1024
PROJ_CHUNK = 512
CONV_ROWS = 256


def _silu(x):
    return x * jax.nn.sigmoid(x)


def _params():
    return pltpu.CompilerParams(
        dimension_semantics=("parallel", "parallel"), vmem_limit_bytes=VMEM_LIMIT)


def _in_proj_kernel(x_ref, g_ref, w_ref, cos_ref, sin_ref, o_ref, *, layer):
    x = x_ref[0]
    ms = jnp.mean(x * x, axis=-1, keepdims=True)
    h = (x * lax.rsqrt(ms + EPS) * g_ref[layer:layer + 1, :]).astype(jnp.bfloat16)
    lane = lax.broadcasted_iota(jnp.int32, (1, LANES), 1)
    first_half = (lane % HEAD_DIM) < HALF

    def rope(t):
        partner = jnp.where(first_half, -pltpu.roll(t, LANES - HALF, 1), pltpu.roll(t, HALF, 1))
        return t * cos_ref[...] + partner * sin_ref[...]

    def store(col, t):
        o_ref[0, :, col:col + t.shape[1]] = t.astype(o_ref.dtype)

    def chunk(c0):
        return jnp.dot(h, w_ref[0, :, c0:c0 + PROJ_CHUNK], preferred_element_type=jnp.float32)

    g = D_GROUP
    acc = chunk(COL_A_U)
    store(P_H, acc[:, :g] * jax.nn.sigmoid(acc[:, g:]))
    acc = chunk(COL_A_Z)
    store(P_ZA, _silu(acc[:, :g]))
    store(P_BB, acc[:, g:])
    acc = chunk(COL_B_C)
    store(P_G, acc[:, :g] * acc[:, g:])
    acc = chunk(COL_B_Z)
    store(P_ZB, _silu(acc[:, :g]))
    for s0 in range(0, g, LANES):
        t = rope(acc[:, g + s0:g + s0 + LANES]) * Q_SCALE
        store(P_CQ + s0, t)
        store(P_CQS + s0, pltpu.roll(t, HEAD_DIM, 1))
    acc = chunk(COL_C_K)
    store(P_CK, rope(acc[:, :LANES]))
    store(P_CV, acc[:, LANES:g])
    store(P_ZC, _silu(acc[:, g:]))
    acc = chunk(COL_D_Q)
    store(P_DQ, acc[:, :g] * Q_SCALE)
    store(P_DK, acc[:, g:])
    acc = chunk(COL_D_V)
    store(P_DV, acc[:, :g])
    store(P_ZD, _silu(acc[:, g:]))


def _in_proj(x, g, w, cos, sin, layer):
    b, s, d = x.shape
    tm = TM_PROJ
    return pl.pallas_call(
        functools.partial(_in_proj_kernel, layer=layer),
        grid=(b, s // tm),
        in_specs=[
            pl.BlockSpec((1, tm, d), lambda bi, i: (bi, i, 0)),
            pl.BlockSpec(g.shape, lambda bi, i: (0, 0)),
            pl.BlockSpec((1, d, D_IN), lambda bi, i: (layer, 0, 0)),
            pl.BlockSpec((tm, LANES), lambda bi, i: (i, 0)),
            pl.BlockSpec((tm, LANES), lambda bi, i: (i, 0)),
        ],
        out_specs=pl.BlockSpec((1, tm, D_PROJ), lambda bi, i: (bi, i, 0)),
        out_shape=jax.ShapeDtypeStruct((b, s, D_PROJ), jnp.bfloat16),
        compiler_params=_params(),
        name="in_proj",
    )(x, g, w, cos, sin)


def _conv_out_kernel(cur_ref, prev_ref, next_ref, aw_ref, ab_ref, lg_ref, lb_ref, bw_ref,
                     ycd_ref, x_ref, w_ref, g_ref, o_ref, hbuf, gbuf, hph, gph,
                     *, layer, final_norm):
    i = pl.program_id(1)
    n = pl.num_programs(1)
    t = T_MIX
    f32 = jnp.float32
    has_prev = (i > 0).astype(f32)
    has_next = (i < n - 1).astype(f32)
    row = lambda ref, j: ref[layer, j:j + 1, :]
    vec = lambda ref: ref[layer:layer + 1, :]
    col = lambda ref, rows, c: ref[0, rows, c:c + D_GROUP].astype(f32)

    full = slice(None)
    hwin = jnp.concatenate([col(prev_ref, full, P_H) * has_prev, col(cur_ref, full, P_H),
                            col(next_ref, full, P_H) * has_next], axis=0)
    gwin = jnp.concatenate([col(prev_ref, full, P_G) * has_prev, col(cur_ref, full, P_G),
                            col(next_ref, full, P_G) * has_next], axis=0)
    hbuf[...] = hwin
    gbuf[...] = gwin

    nwin = t + 2 * HALO
    nph = nwin - SUBLANES
    for p in range(1, SUBLANES):
        hph[p - 1] = pltpu.roll(hwin, nwin - p, 0)[0:nph]
    gph[0] = pltpu.roll(gwin, nwin - 1, 0)[0:nph]
    gph[1] = pltpu.roll(gwin, nwin - (SUBLANES - 1), 0)[0:nph]

    def tap(off, rows):
        p = off % SUBLANES
        if p == 0:
            return hbuf[off:off + rows, :]
        return hph[p - 1, off - p:off - p + rows, :]

    rc = CONV_ROWS
    for r0 in range(0, t, rc):
        rows = slice(r0, r0 + rc)
        acc = jnp.broadcast_to(vec(ab_ref), (rc, D_GROUP))
        for j in range(CONV_A_WIDTH):
            acc = acc + aw_ref[layer, j:j + 1, :] * tap(HALO - CONV_A_PAD + j + r0, rc)
        mu = jnp.mean(acc, axis=-1, keepdims=True)
        xc = acc - mu
        var = jnp.mean(xc * xc, axis=-1, keepdims=True)
        hn = xc * lax.rsqrt(var + EPS) * vec(lg_ref) + vec(lb_ref)
        ya = _silu(hn) * col(cur_ref, rows, P_ZA)
        gm = HALO + r0 - SUBLANES
        conv = (row(bw_ref, 0) * gph[1, gm:gm + rc, :]
                + row(bw_ref, 1) * gbuf[HALO + r0:HALO + r0 + rc, :]
                + row(bw_ref, 2) * gph[0, HALO + r0:HALO + r0 + rc, :])
        yb = col(cur_ref, rows, P_BB) * conv * col(cur_ref, rows, P_ZB)
        yab = jnp.concatenate([ya.astype(jnp.bfloat16), yb.astype(jnp.bfloat16)], axis=1)
        xo = x_ref[0, rows, :] + jnp.dot(ycd_ref[0, rows, :], w_ref[0, 2 * D_GROUP:4 * D_GROUP, :],
                                         preferred_element_type=f32)
        xo = xo + jnp.dot(yab, w_ref[0, 0:2 * D_GROUP, :], preferred_element_type=f32)
        if final_norm:
            ms = jnp.mean(xo * xo, axis=-1, keepdims=True)
            xo = xo * lax.rsqrt(ms + EPS) * g_ref[...]
        o_ref[0, rows, :] = xo


def _conv_out(proj, ycd, x, aw, ab, lg, lb, bw, w, g, layer, final_norm):
    b, s, d = x.shape
    t = T_MIX
    hb = t // HALO
    nh = s // HALO
    f32 = jnp.float32
    whole = lambda a: pl.BlockSpec(a.shape, lambda bi, i: (0,) * a.ndim)
    return pl.pallas_call(
        functools.partial(_conv_out_kernel, layer=layer, final_norm=final_norm),
        grid=(b, s // t),
        in_specs=[
            pl.BlockSpec((1, t, W_AB), lambda bi, i: (bi, i, 0)),
            pl.BlockSpec((1, HALO, W_AB), lambda bi, i: (bi, jnp.maximum(i * hb - 1, 0), 0)),
            pl.BlockSpec((1, HALO, W_AB), lambda bi, i: (bi, jnp.minimum((i + 1) * hb, nh - 1), 0)),
            whole(aw), whole(ab), whole(lg), whole(lb), whole(bw),
            pl.BlockSpec((1, t, 2 * D_GROUP), lambda bi, i: (bi, i, 0)),
            pl.BlockSpec((1, t, d), lambda bi, i: (bi, i, 0)),
            pl.BlockSpec((1,) + w.shape[1:], lambda bi, i: (layer, 0, 0)),
            pl.BlockSpec((1, d), lambda bi, i: (0, 0)),
        ],
        out_specs=pl.BlockSpec((1, t, d), lambda bi, i: (bi, i, 0)),
        out_shape=jax.ShapeDtypeStruct((b, s, d), f32),
        scratch_shapes=[pltpu.VMEM((t + 2 * HALO, D_GROUP), f32),
                        pltpu.VMEM((t + 2 * HALO, D_GROUP), f32),
                        pltpu.VMEM((SUBLANES - 1, t + 2 * HALO - SUBLANES, D_GROUP), f32),
                        pltpu.VMEM((2, t + 2 * HALO - SUBLANES, D_GROUP), f32)],
        compiler_params=_params(),
        name="conv_out",
    )(proj, proj, proj, aw, ab, lg, lb, bw, ycd, x, w, g)


def _swa_tile(sink_ref, q_ref, qs_ref, z_ref, kv_ref, y_ref, i, layer):
    s_len = kv_ref.shape[1]
    nkeys = 3 * SWA_BLOCK
    f32 = jnp.float32
    lane = lax.broadcasted_iota(jnp.int32, (1, LANES), 1)
    lo = lane < HEAD_DIM
    zero = jnp.zeros((), jnp.bfloat16)
    ones = jnp.ones((nkeys, LANES), jnp.bfloat16)
    for nb in range(T_ATTN // SWA_BLOCK):
        q0 = i * T_ATTN + nb * SWA_BLOCK
        ws = pl.multiple_of(jnp.clip(q0 - SWA_BLOCK, 0, s_len - nkeys), SWA_BLOCK)
        rows = slice(nb * SWA_BLOCK, (nb + 1) * SWA_BLOCK)
        win = pl.ds(ws, nkeys)
        lhs = [
            jnp.where(lo, q_ref[0, rows, 0:LANES], zero),
            jnp.where(lo, qs_ref[0, rows, 0:LANES], zero),
            jnp.where(lo, zero, qs_ref[0, rows, LANES:2 * LANES]),
            jnp.where(lo, zero, q_ref[0, rows, LANES:2 * LANES]),
        ]
        kw = kv_ref[0, win, 0:LANES]
        vw = jnp.concatenate([kv_ref[0, win, LANES:2 * LANES], ones], axis=1)
        qpos = q0 + lax.broadcasted_iota(jnp.int32, (SWA_BLOCK, nkeys), 0)
        kpos = ws + lax.broadcasted_iota(jnp.int32, (SWA_BLOCK, nkeys), 1)
        valid = jnp.abs(kpos - qpos) <= SWA_WINDOW
        num, den = [], []
        for h in range(SWA_HEADS):
            sc = jnp.einsum("qd,kd->qk", lhs[h], kw, preferred_element_type=f32)
            sh = jnp.where(valid, sc, NEG_INF)
            sink = sink_ref[layer, h]
            m = jnp.maximum(jnp.max(sh, axis=-1, keepdims=True), sink)
            e = jnp.exp2(sh - m)
            pv = jnp.dot(e.astype(jnp.bfloat16), vw, preferred_element_type=f32)
            num.append(pv[:, :LANES])
            den.append(pv[:, LANES:] + jnp.exp2(sink - m))
        og0 = jnp.where(lo, num[0], pltpu.roll(num[1], HEAD_DIM, 1)) / jnp.where(lo, den[0], den[1])
        og1 = jnp.where(lo, pltpu.roll(num[2], HEAD_DIM, 1), num[3]) / jnp.where(lo, den[2], den[3])
        y = jnp.concatenate([og0, og1], axis=-1) * z_ref[0, rows, :].astype(f32)
        y_ref[rows, 0:D_GROUP] = y.astype(y_ref.dtype)


def _nbr_tile(q_ref, z_ref, k_ref, v_ref, bias_ref, y_ref, i):
    rows_total = k_ref.shape[1] // GRID_W
    rows_tile = T_ATTN // GRID_W
    nkeys = NA_KH * GRID_W
    f32 = jnp.float32
    lane = lax.broadcasted_iota(jnp.int32, (1, LANES), 1)
    lo = lane < HEAD_DIM
    zero = jnp.zeros((), jnp.bfloat16)
    ones = jnp.ones((nkeys, LANES), jnp.bfloat16)

    for rr in range(rows_tile):
        r = i * rows_tile + rr
        r0 = jnp.clip(r - NA_KH // 2, 0, rows_total - NA_KH)
        d0 = r0 - r + (NA_KH - 1)
        ks = pl.multiple_of(r0 * GRID_W, GRID_W)
        qrow = slice(rr * GRID_W, (rr + 1) * GRID_W)
        outs = []
        for g in range(2):
            cols = slice(g * LANES, (g + 1) * LANES)
            qg = q_ref[0, qrow, cols]
            lhs = jnp.concatenate([jnp.where(lo, qg, zero), jnp.where(lo, zero, qg)], axis=0)
            kw = k_ref[0, pl.ds(ks, nkeys), cols]
            vw = jnp.concatenate([v_ref[0, pl.ds(ks, nkeys), cols], ones], axis=1)
            sc = jnp.einsum("qd,kd->qk", lhs, kw, preferred_element_type=f32)
            bias = jnp.concatenate(
                [jnp.concatenate([bias_ref[0, 2 * g + hh, d0 + 2 * m]
                                  for m in range(NA_KH // 2)], axis=-1)
                 for hh in range(2)], axis=0)
            sc = sc + bias
            m_ = jnp.max(sc, axis=-1, keepdims=True)
            e = jnp.exp2(sc - m_)
            pv = jnp.dot(e.astype(jnp.bfloat16), vw, preferred_element_type=f32)
            pv = jnp.where(jnp.concatenate([lo, lo], axis=1), pv[0:GRID_W], pv[GRID_W:2 * GRID_W])
            outs.append(pv[:, :LANES] / pv[:, LANES:])
        y = jnp.concatenate(outs, axis=-1) * z_ref[0, qrow, :].astype(f32)
        y_ref[qrow, D_GROUP:2 * D_GROUP] = y.astype(y_ref.dtype)


def _attn_kernel(sink_ref, cq_ref, cqs_ref, cz_ref, ckv_ref, dq_ref, dz_ref, dk_ref, dv_ref,
                 bias_ref, o_ref, *, layer):
    i = pl.program_id(1)
    y_ref = o_ref.at[0]
    _swa_tile(sink_ref, cq_ref, cqs_ref, cz_ref, ckv_ref, y_ref, i, layer)
    _nbr_tile(dq_ref, dz_ref, dk_ref, dv_ref, bias_ref, y_ref, i)


def _attn(proj, sink, bias, layer):
    b, s, _ = proj.shape
    t = T_ATTN
    tile = lambda col: pl.BlockSpec((1, t, D_GROUP), lambda bi, i, sk: (bi, i, col // D_GROUP))
    seq = lambda col: pl.BlockSpec((1, s, D_GROUP), lambda bi, i, sk: (bi, 0, col // D_GROUP))
    return pl.pallas_call(
        functools.partial(_attn_kernel, layer=layer),
        grid_spec=pltpu.PrefetchScalarGridSpec(
            num_scalar_prefetch=1,
            grid=(b, s // t),
            in_specs=[
                tile(P_CQ), tile(P_CQS), tile(P_ZC), seq(P_CK),
                tile(P_DQ), tile(P_ZD), seq(P_DK), seq(P_DV),
                pl.BlockSpec((1,) + bias.shape[1:], lambda bi, i, sk: (layer, 0, 0, 0, 0)),
            ],
            out_specs=pl.BlockSpec((1, t, 2 * D_GROUP), lambda bi, i, sk: (bi, i, 0)),
        ),
        out_shape=jax.ShapeDtypeStruct((b, s, 2 * D_GROUP), jnp.bfloat16),
        compiler_params=_params(),
        name="attn",
    )(sink, proj, proj, proj, proj, proj, proj, proj, proj, bias)


def _rope_tables(s):
    inv_freq = ROPE_THETA ** (-jnp.arange(0, HEAD_DIM, 2, dtype=jnp.float32) / HEAD_DIM)
    inv_freq = jnp.tile(inv_freq, LANES // HALF)
    hi = jnp.arange(0, s, GRID_W, dtype=jnp.float32)[:, None, None] * inv_freq
    lo = jnp.arange(GRID_W, dtype=jnp.float32)[None, :, None] * inv_freq
    ch, sh, cl, sl = jnp.cos(hi), jnp.sin(hi), jnp.cos(lo), jnp.sin(lo)
    cos = ch * cl - sh * sl
    sin = sh * cl + ch * sl
    return cos.reshape(s, LANES), sin.reshape(s, LANES)


def _nbr_bias_table(rpb):
    nrel = 2 * NA_KW - 1
    ndr = 2 * NA_KH - 1
    c = np.arange(GRID_W)
    c0 = np.clip(c - NA_KW // 2, 0, GRID_W - NA_KW)
    col_ok = (c[None, :] >= c0[:, None]) & (c[None, :] < c0[:, None] + NA_KW)
    dc = np.clip(c[None, :] - c[:, None], -(NA_KW - 1), NA_KW - 1) + (NA_KW - 1)
    onehot = (dc[None] == np.arange(nrel)[:, None, None]).astype(np.float32)
    pick = np.zeros((2, nrel, GRID_W, 2, GRID_W), np.float32)
    pick[0, :, :, 0, :] = onehot
    pick[1, :, :, 1, :] = onehot
    pick = pick.reshape(2 * nrel, GRID_W, LANES)
    dr = np.arange(ndr + 1)[:, None] + np.arange(2)[None, :]
    valid = (dr < ndr)[:, None, :, None] & col_ok[None, :, None, :]
    valid = valid.reshape(ndr + 1, GRID_W, LANES)
    ext = jnp.pad(rpb, ((0, 0), (0, 0), (0, 2), (0, 0)))
    pair = jnp.concatenate([ext[:, :, :-1], ext[:, :, 1:]], axis=-1)
    full = jnp.einsum("lhdc,cqj->lhdqj", pair, jnp.asarray(pick),
                      precision=lax.Precision.HIGHEST)
    return jnp.where(jnp.asarray(valid), full * LOG2E, NEG_INF)


def kernel(x, norm_g, w_in, w_out, conv_a_w, conv_a_b, ln_a_g, ln_a_b, conv_b_w, swa_sink,
           na_rpb, final_norm_g):
    depth = norm_g.shape[0]
    s = x.shape[1]
    assert s % max(TM_PROJ, T_MIX, T_ATTN) == 0 and s % GRID_W == 0, x.shape
    assert s // GRID_W >= NA_KH and s >= 3 * SWA_BLOCK, x.shape
    assert w_in.shape[1:] == (x.shape[2], D_IN) and w_out.shape[1:] == (4 * D_GROUP, x.shape[2])
    cos_t, sin_t = _rope_tables(s)
    w_in_b = w_in.astype(jnp.bfloat16)
    w_out_b = w_out.astype(jnp.bfloat16)
    bias = _nbr_bias_table(na_rpb)
    sink = swa_sink * LOG2E
    fg = final_norm_g.reshape(1, -1)
    for l in range(depth):
        proj = _in_proj(x, norm_g, w_in_b, cos_t, sin_t, l)
        ycd = _attn(proj, sink, bias, l)
        x = _conv_out(proj, ycd, x, conv_a_w, conv_a_b, ln_a_g, ln_a_b, conv_b_w,
                      w_out_b, fg, l, final_norm=(l == depth - 1))
    return x
```

```python
import functools
import math

import numpy as np
import jax
import jax.numpy as jnp
from jax import lax
from jax.experimental import pallas as pl
from jax.experimental.pallas import tpu as pltpu

D_GROUP = 256
HEAD_DIM = 64
HALF = HEAD_DIM // 2
GRID_W = 64
CONV_A_WIDTH = 31
CONV_A_PAD = (CONV_A_WIDTH - 1) // 2
SWA_WINDOW = 128
SWA_BLOCK = 128
SWA_HEADS = 4
NA_KH = 8
NA_KW = 16
ROPE_THETA = 10000.0
EPS = 1e-6
NEG_INF = -1e30
LOG2E = math.log2(math.e)
Q_SCALE = HEAD_DIM ** -0.5 * LOG2E

COL_A_U, COL_A_V, COL_A_Z = 0, 256, 512
COL_B_B, COL_B_C, COL_B_X, COL_B_Z = 768, 1024, 1280, 1536
COL_C_Q, COL_C_K, COL_C_V, COL_C_Z = 1792, 2048, 2176, 2304
COL_D_Q, COL_D_K, COL_D_V, COL_D_Z = 2560, 2816, 3072, 3328
D_IN = 3584

P_H, P_ZA = 0, 256
P_BB, P_G, P_ZB = 512, 768, 1024
P_CQ, P_CQS, P_CK, P_CV, P_ZC = 1280, 1536, 1792, 1920, 2048
P_DQ, P_DK, P_DV, P_ZD = 2304, 2560, 2816, 3072
D_PROJ = 3328
W_AB = P_CQ

LANES = 128
SUBLANES = 8
HALO = 16
VMEM_LIMIT = 56 * 1024 * 1024

TM_PROJ = 1024
T_MIX = 1024
T_ATTN = 1024
PROJ_CHUNK = 512
CONV_ROWS = 256


def _silu(x):
    return x * jax.nn.sigmoid(x)


def _params():
    return pltpu.CompilerParams(
        dimension_semantics=("parallel", "parallel"), vmem_limit_bytes=VMEM_LIMIT)


def _in_proj_kernel(x_ref, g_ref, w32_ref, cos_ref, sin_ref, o_ref, w_ref, *, layer):
    @pl.when((pl.program_id(0) == 0) & (pl.program_id(1) == 0))
    def _():
        for c0 in range(0, D_IN, PROJ_CHUNK):
            w_ref[:, c0:c0 + PROJ_CHUNK] = w32_ref[0, :, c0:c0 + PROJ_CHUNK].astype(w_ref.dtype)

    x = x_ref[0]
    ms = jnp.mean(x * x, axis=-1, keepdims=True)
    h = (x * lax.rsqrt(ms + EPS) * g_ref[layer:layer + 1, :]).astype(jnp.bfloat16)
    lane = lax.broadcasted_iota(jnp.int32, (1, LANES), 1)
    first_half = (lane % HEAD_DIM) < HALF

    def rope(t):
        partner = jnp.where(first_half, -pltpu.roll(t, LANES - HALF, 1), pltpu.roll(t, HALF, 1))
        return t * cos_ref[...] + partner * sin_ref[...]

    def store(col, t):
        o_ref[0, :, col:col + t.shape[1]] = t.astype(o_ref.dtype)

    def chunk(c0):
        return jnp.dot(h, w_ref[:, c0:c0 + PROJ_CHUNK], preferred_element_type=jnp.float32)

    g = D_GROUP
    acc = chunk(COL_A_U)
    store(P_H, acc[:, :g] * jax.nn.sigmoid(acc[:, g:]))
    acc = chunk(COL_A_Z)
    store(P_ZA, _silu(acc[:, :g]))
    store(P_BB, acc[:, g:])
    acc = chunk(COL_B_C)
    store(P_G, acc[:, :g] * acc[:, g:])
    acc = chunk(COL_B_Z)
    store(P_ZB, _silu(acc[:, :g]))
    for s0 in range(0, g, LANES):
        t = rope(acc[:, g + s0:g + s0 + LANES]) * Q_SCALE
        store(P_CQ + s0, t)
        store(P_CQS + s0, pltpu.roll(t, HEAD_DIM, 1))
    acc = chunk(COL_C_K)
    store(P_CK, rope(acc[:, :LANES]))
    store(P_CV, acc[:, LANES:g])
    store(P_ZC, _silu(acc[:, g:]))
    acc = chunk(COL_D_Q)
    store(P_DQ, acc[:, :g] * Q_SCALE)
    store(P_DK, acc[:, g:])
    acc = chunk(COL_D_V)
    store(P_DV, acc[:, :g])
    store(P_ZD, _silu(acc[:, g:]))


def _in_proj(x, g, w, cos, sin, layer):
    b, s, d = x.shape
    tm = TM_PROJ
    return pl.pallas_call(
        functools.partial(_in_proj_kernel, layer=layer),
        grid=(b, s // tm),
        in_specs=[
            pl.BlockSpec((1, tm, d), lambda bi, i: (bi, i, 0)),
            pl.BlockSpec(g.shape, lambda bi, i: (0, 0)),
            pl.BlockSpec((1, d, D_IN), lambda bi, i: (layer, 0, 0), pipeline_mode=pl.Buffered(1)),
            pl.BlockSpec((tm, LANES), lambda bi, i: (i, 0)),
            pl.BlockSpec((tm, LANES), lambda bi, i: (i, 0)),
        ],
        out_specs=pl.BlockSpec((1, tm, D_PROJ), lambda bi, i: (bi, i, 0)),
        out_shape=jax.ShapeDtypeStruct((b, s, D_PROJ), jnp.bfloat16),
        scratch_shapes=[pltpu.VMEM((d, D_IN), jnp.bfloat16)],
        compiler_params=pltpu.CompilerParams(
            dimension_semantics=("arbitrary", "arbitrary"), vmem_limit_bytes=VMEM_LIMIT),
        name="in_proj",
    )(x, g, w, cos, sin)


def _conv_out_kernel(cur_ref, prev_ref, next_ref, aw_ref, ab_ref, lg_ref, lb_ref, bw_ref,
                     ycd_ref, x_ref, w_ref, g_ref, o_ref, hbuf, gbuf, hph, gph,
                     *, layer, final_norm):
    i = pl.program_id(1)
    n = pl.num_programs(1)
    t = T_MIX
    f32 = jnp.float32
    has_prev = (i > 0).astype(f32)
    has_next = (i < n - 1).astype(f32)
    row = lambda ref, j: ref[layer, j:j + 1, :]
    vec = lambda ref: ref[layer:layer + 1, :]
    col = lambda ref, rows, c: ref[0, rows, c:c + D_GROUP].astype(f32)

    full = slice(None)
    hwin = jnp.concatenate([col(prev_ref, full, P_H) * has_prev, col(cur_ref, full, P_H),
                            col(next_ref, full, P_H) * has_next], axis=0)
    gwin = jnp.concatenate([col(prev_ref, full, P_G) * has_prev, col(cur_ref, full, P_G),
                            col(next_ref, full, P_G) * has_next], axis=0)
    hbuf[...] = hwin
    gbuf[...] = gwin

    nwin = t + 2 * HALO
    nph = nwin - SUBLANES
    for p in range(1, SUBLANES):
        hph[p - 1] = pltpu.roll(hwin, nwin - p, 0)[0:nph]
    gph[0] = pltpu.roll(gwin, nwin - 1, 0)[0:nph]
    gph[1] = pltpu.roll(gwin, nwin - (SUBLANES - 1), 0)[0:nph]

    def tap(off, rows):
        p = off % SUBLANES
        if p == 0:
            return hbuf[off:off + rows, :]
        return hph[p - 1, off - p:off - p + rows, :]

    rc = CONV_ROWS
    for r0 in range(0, t, rc):
        rows = slice(r0, r0 + rc)
        acc = jnp.broadcast_to(vec(ab_ref), (rc, D_GROUP))
        for j in range(CONV_A_WIDTH):
            acc = acc + aw_ref[layer, j:j + 1, :] * tap(HALO - CONV_A_PAD + j + r0, rc)
        mu = jnp.mean(acc, axis=-1, keepdims=True)
        xc = acc - mu
        var = jnp.mean(xc * xc, axis=-1, keepdims=True)
        hn = xc * lax.rsqrt(var + EPS) * vec(lg_ref) + vec(lb_ref)
        ya = _silu(hn) * col(cur_ref, rows, P_ZA)
        gm = HALO + r0 - SUBLANES
        conv = (row(bw_ref, 0) * gph[1, gm:gm + rc, :]
                + row(bw_ref, 1) * gbuf[HALO + r0:HALO + r0 + rc, :]
                + row(bw_ref, 2) * gph[0, HALO + r0:HALO + r0 + rc, :])
        yb = col(cur_ref, rows, P_BB) * conv * col(cur_ref, rows, P_ZB)
        yab = jnp.concatenate([ya.astype(jnp.bfloat16), yb.astype(jnp.bfloat16)], axis=1)
        xo = x_ref[0, rows, :] + jnp.dot(ycd_ref[0, rows, :], w_ref[0, 2 * D_GROUP:4 * D_GROUP, :],
                                         preferred_element_type=f32)
        xo = xo + jnp.dot(yab, w_ref[0, 0:2 * D_GROUP, :], preferred_element_type=f32)
        if final_norm:
            ms = jnp.mean(xo * xo, axis=-1, keepdims=True)
            xo = xo * lax.rsqrt(ms + EPS) * g_ref[...]
        o_ref[0, rows, :] = xo


def _conv_out(proj, ycd, x, aw, ab, lg, lb, bw, w, g, layer, final_norm):
    b, s, d = x.shape
    t = T_MIX
    hb = t // HALO
    nh = s // HALO
    f32 = jnp.float32
    whole = lambda a: pl.BlockSpec(a.shape, lambda bi, i: (0,) * a.ndim)
    return pl.pallas_call(
        functools.partial(_conv_out_kernel, layer=layer, final_norm=final_norm),
        grid=(b, s // t),
        in_specs=[
            pl.BlockSpec((1, t, W_AB), lambda bi, i: (bi, i, 0)),
            pl.BlockSpec((1, HALO, W_AB), lambda bi, i: (bi, jnp.maximum(i * hb - 1, 0), 0)),
            pl.BlockSpec((1, HALO, W_AB), lambda bi, i: (bi, jnp.minimum((i + 1) * hb, nh - 1), 0)),
            whole(aw), whole(ab), whole(lg), whole(lb), whole(bw),
            pl.BlockSpec((1, t, 2 * D_GROUP), lambda bi, i: (bi, i, 0)),
            pl.BlockSpec((1, t, d), lambda bi, i: (bi, i, 0)),
            pl.BlockSpec((1,) + w.shape[1:], lambda bi, i: (layer, 0, 0)),
            pl.BlockSpec((1, d), lambda bi, i: (0, 0)),
        ],
        out_specs=pl.BlockSpec((1, t, d), lambda bi, i: (bi, i, 0)),
        out_shape=jax.ShapeDtypeStruct((b, s, d), f32),
        scratch_shapes=[pltpu.VMEM((t + 2 * HALO, D_GROUP), f32),
                        pltpu.VMEM((t + 2 * HALO, D_GROUP), f32),
                        pltpu.VMEM((SUBLANES - 1, t + 2 * HALO - SUBLANES, D_GROUP), f32),
                        pltpu.VMEM((2, t + 2 * HALO - SUBLANES, D_GROUP), f32)],
        compiler_params=_params(),
        name="conv_out",
    )(proj, proj, proj, aw, ab, lg, lb, bw, ycd, x, w, g)


def _swa_tile(sink_ref, q_ref, qs_ref, z_ref, kv_ref, y_ref, i, layer):
    s_len = kv_ref.shape[1]
    nkeys = 3 * SWA_BLOCK
    f32 = jnp.float32
    lane = lax.broadcasted_iota(jnp.int32, (1, LANES), 1)
    lo = lane < HEAD_DIM
    zero = jnp.zeros((), jnp.bfloat16)
    ones = jnp.ones((nkeys, LANES), jnp.bfloat16)
    for nb in range(T_ATTN // SWA_BLOCK):
        q0 = i * T_ATTN + nb * SWA_BLOCK
        ws = pl.multiple_of(jnp.clip(q0 - SWA_BLOCK, 0, s_len - nkeys), SWA_BLOCK)
        rows = slice(nb * SWA_BLOCK, (nb + 1) * SWA_BLOCK)
        win = pl.ds(ws, nkeys)
        lhs = [
            jnp.where(lo, q_ref[0, rows, 0:LANES], zero),
            jnp.where(lo, qs_ref[0, rows, 0:LANES], zero),
            jnp.where(lo, zero, qs_ref[0, rows, LANES:2 * LANES]),
            jnp.where(lo, zero, q_ref[0, rows, LANES:2 * LANES]),
        ]
        kw = kv_ref[0, win, 0:LANES]
        vw = jnp.concatenate([kv_ref[0, win, LANES:2 * LANES], ones], axis=1)
        qpos = q0 + lax.broadcasted_iota(jnp.int32, (SWA_BLOCK, nkeys), 0)
        kpos = ws + lax.broadcasted_iota(jnp.int32, (SWA_BLOCK, nkeys), 1)
        valid = jnp.abs(kpos - qpos) <= SWA_WINDOW
        num, den = [], []
        for h in range(SWA_HEADS):
            sc = jnp.einsum("qd,kd->qk", lhs[h], kw, preferred_element_type=f32)
            sh = jnp.where(valid, sc, NEG_INF)
            sink = sink_ref[layer, h]
            m = jnp.maximum(jnp.max(sh, axis=-1, keepdims=True), sink)
            e = jnp.exp2(sh - m)
            pv = jnp.dot(e.astype(jnp.bfloat16), vw, preferred_element_type=f32)
            num.append(pv[:, :LANES])
            den.append(pv[:, LANES:] + jnp.exp2(sink - m))
        og0 = jnp.where(lo, num[0], pltpu.roll(num[1], HEAD_DIM, 1)) / jnp.where(lo, den[0], den[1])
        og1 = jnp.where(lo, pltpu.roll(num[2], HEAD_DIM, 1), num[3]) / jnp.where(lo, den[2], den[3])
        y = jnp.concatenate([og0, og1], axis=-1) * z_ref[0, rows, :].astype(f32)
        y_ref[rows, 0:D_GROUP] = y.astype(y_ref.dtype)


def _nbr_tile(q_ref, z_ref, k_ref, v_ref, bias_ref, y_ref, i):
    rows_total = k_ref.shape[1] // GRID_W
    rows_tile = T_ATTN // GRID_W
    nkeys = NA_KH * GRID_W
    f32 = jnp.float32
    lane = lax.broadcasted_iota(jnp.int32, (1, LANES), 1)
    lo = lane < HEAD_DIM
    zero = jnp.zeros((), jnp.bfloat16)
    ones = jnp.ones((nkeys, LANES), jnp.bfloat16)

    for rr in range(rows_tile):
        r = i * rows_tile + rr
        r0 = jnp.clip(r - NA_KH // 2, 0, rows_total - NA_KH)
        d0 = r0 - r + (NA_KH - 1)
        ks = pl.multiple_of(r0 * GRID_W, GRID_W)
        qrow = slice(rr * GRID_W, (rr + 1) * GRID_W)
        outs = []
        for g in range(2):
            cols = slice(g * LANES, (g + 1) * LANES)
            qg = q_ref[0, qrow, cols]
            lhs = jnp.concatenate([jnp.where(lo, qg, zero), jnp.where(lo, zero, qg)], axis=0)
            kw = k_ref[0, pl.ds(ks, nkeys), cols]
            vw = jnp.concatenate([v_ref[0, pl.ds(ks, nkeys), cols], ones], axis=1)
            sc = jnp.einsum("qd,kd->qk", lhs, kw, preferred_element_type=f32)
            bias = jnp.concatenate(
                [jnp.concatenate([bias_ref[0, 2 * g + hh, d0 + 2 * m]
                                  for m in range(NA_KH // 2)], axis=-1)
                 for hh in range(2)], axis=0)
            sc = sc + bias
            m_ = jnp.max(sc, axis=-1, keepdims=True)
            e = jnp.exp2(sc - m_)
            pv = jnp.dot(e.astype(jnp.bfloat16), vw, preferred_element_type=f32)
            pv = jnp.where(jnp.concatenate([lo, lo], axis=1), pv[0:GRID_W], pv[GRID_W:2 * GRID_W])
            outs.append(pv[:, :LANES] / pv[:, LANES:])
        y = jnp.concatenate(outs, axis=-1) * z_ref[0, qrow, :].astype(f32)
        y_ref[qrow, D_GROUP:2 * D_GROUP] = y.astype(y_ref.dtype)


def _attn_kernel(sink_ref, cq_ref, cqs_ref, cz_ref, ckv_ref, dq_ref, dz_ref, dk_ref, dv_ref,
                 bias_ref, o_ref, *, layer):
    i = pl.program_id(1)
    y_ref = o_ref.at[0]
    _swa_tile(sink_ref, cq_ref, cqs_ref, cz_ref, ckv_ref, y_ref, i, layer)
    _nbr_tile(dq_ref, dz_ref, dk_ref, dv_ref, bias_ref, y_ref, i)


def _attn(proj, sink, bias, layer):
    b, s, _ = proj.shape
    t = T_ATTN
    tile = lambda col: pl.BlockSpec((1, t, D_GROUP), lambda bi, i, sk: (bi, i, col // D_GROUP))
    seq = lambda col: pl.BlockSpec((1, s, D_GROUP), lambda bi, i, sk: (bi, 0, col // D_GROUP))
    return pl.pallas_call(
        functools.partial(_attn_kernel, layer=layer),
        grid_spec=pltpu.PrefetchScalarGridSpec(
            num_scalar_prefetch=1,
            grid=(b, s // t),
            in_specs=[
                tile(P_CQ), tile(P_CQS), tile(P_ZC), seq(P_CK),
                tile(P_DQ), tile(P_ZD), seq(P_DK), seq(P_DV),
                pl.BlockSpec((1,) + bias.shape[1:], lambda bi, i, sk: (layer, 0, 0, 0, 0)),
            ],
            out_specs=pl.BlockSpec((1, t, 2 * D_GROUP), lambda bi, i, sk: (bi, i, 0)),
        ),
        out_shape=jax.ShapeDtypeStruct((b, s, 2 * D_GROUP), jnp.bfloat16),
        compiler_params=_params(),
        name="attn",
    )(sink, proj, proj, proj, proj, proj, proj, proj, proj, bias)


def _rope_tables(s):
    inv_freq = ROPE_THETA ** (-jnp.arange(0, HEAD_DIM, 2, dtype=jnp.float32) / HEAD_DIM)
    inv_freq = jnp.tile(inv_freq, LANES // HALF)
    hi = jnp.arange(0, s, GRID_W, dtype=jnp.float32)[:, None, None] * inv_freq
    lo = jnp.arange(GRID_W, dtype=jnp.float32)[None, :, None] * inv_freq
    ch, sh, cl, sl = jnp.cos(hi), jnp.sin(hi), jnp.cos(lo), jnp.sin(lo)
    cos = ch * cl - sh * sl
    sin = sh * cl + ch * sl
    return cos.reshape(s, LANES), sin.reshape(s, LANES)


def _nbr_bias_table(rpb):
    nrel = 2 * NA_KW - 1
    ndr = 2 * NA_KH - 1
    c = np.arange(GRID_W)
    c0 = np.clip(c - NA_KW // 2, 0, GRID_W - NA_KW)
    col_ok = (c[None, :] >= c0[:, None]) & (c[None, :] < c0[:, None] + NA_KW)
    dc = np.clip(c[None, :] - c[:, None], -(NA_KW - 1), NA_KW - 1) + (NA_KW - 1)
    onehot = (dc[None] == np.arange(nrel)[:, None, None]).astype(np.float32)
    pick = np.zeros((2, nrel, GRID_W, 2, GRID_W), np.float32)
    pick[0, :, :, 0, :] = onehot
    pick[1, :, :, 1, :] = onehot
    pick = pick.reshape(2 * nrel, GRID_W, LANES)
    dr = np.arange(ndr + 1)[:, None] + np.arange(2)[None, :]
    valid = (dr < ndr)[:, None, :, None] & col_ok[None, :, None, :]
    valid = valid.reshape(ndr + 1, GRID_W, LANES)
    ext = jnp.pad(rpb, ((0, 0), (0, 0), (0, 2), (0, 0)))
    pair = jnp.concatenate([ext[:, :, :-1], ext[:, :, 1:]], axis=-1)
    full = jnp.einsum("lhdc,cqj->lhdqj", pair, jnp.asarray(pick),
                      precision=lax.Precision.HIGHEST)
    return jnp.where(jnp.asarray(valid), full * LOG2E, NEG_INF)


def kernel(x, norm_g, w_in, w_out, conv_a_w, conv_a_b, ln_a_g, ln_a_b, conv_b_w, swa_sink,
           na_rpb, final_norm_g):
    depth = norm_g.shape[0]
    s = x.shape[1]
    assert s % max(TM_PROJ, T_MIX, T_ATTN) == 0 and s % GRID_W == 0, x.shape
    assert s // GRID_W >= NA_KH and s >= 3 * SWA_BLOCK, x.shape
    assert w_in.shape[1:] == (x.shape[2], D_IN) and w_out.shape[1:] == (4 * D_GROUP, x.shape[2])
    cos_t, sin_t = _rope_tables(s)
    w_out_b = w_out.astype(jnp.bfloat16)
    bias = _nbr_bias_table(na_rpb)
    sink = swa_sink * LOG2E
    fg = final_norm_g.reshape(1, -1)
    for l in range(depth):
        proj = _in_proj(x, norm_g, w_in, cos_t, sin_t, l)
        ycd = _attn(proj, sink, bias, l)
        x = _conv_out(proj, ycd, x, conv_a_w, conv_a_b, ln_a_g, ln_a_b, conv_b_w,
                      w_out_b, fg, l, final_norm=(l == depth - 1))
    return x
```

```python
import functools
import math

import numpy as np
import jax
import jax.numpy as jnp
from jax import lax
from jax.experimental import pallas as pl
from jax.experimental.pallas import tpu as pltpu

D_GROUP = 256
HEAD_DIM = 64
HALF = HEAD_DIM // 2
GRID_W = 64
CONV_A_WIDTH = 31
CONV_A_PAD = (CONV_A_WIDTH - 1) // 2
SWA_WINDOW = 128
SWA_BLOCK = 128
SWA_HEADS = 4
NA_KH = 8
NA_KW = 16
ROPE_THETA = 10000.0
EPS = 1e-6
NEG_INF = -1e30
LOG2E = math.log2(math.e)
Q_SCALE = HEAD_DIM ** -0.5 * LOG2E

COL_A_U, COL_A_V, COL_A_Z = 0, 256, 512
COL_B_B, COL_B_C, COL_B_X, COL_B_Z = 768, 1024, 1280, 1536
COL_C_Q, COL_C_K, COL_C_V, COL_C_Z = 1792, 2048, 2176, 2304
COL_D_Q, COL_D_K, COL_D_V, COL_D_Z = 2560, 2816, 3072, 3328
D_IN = 3584

P_H, P_ZA = 0, 256
P_BB, P_G, P_ZB = 512, 768, 1024
P_CQ, P_CQS, P_CK, P_CV, P_ZC = 1280, 1536, 1792, 1920, 2048
P_DQ, P_DK, P_DV, P_ZD = 2304, 2560, 2816, 3072
D_PROJ = 3328
W_AB = P_CQ

LANES = 128
SUBLANES = 8
HALO = 16
VMEM_LIMIT = 56 * 1024 * 1024

TM_PROJ = 1024
T_MIX = 1024
T_ATTN = 1024
PROJ_CHUNK = 512
CONV_ROWS = 256


def _silu(x):
    return x * jax.nn.sigmoid(x)


def _params():
    return pltpu.CompilerParams(
        dimension_semantics=("parallel", "parallel"), vmem_limit_bytes=VMEM_LIMIT)


def _in_proj_kernel(x_ref, g_ref, w32_ref, cos_ref, sin_ref, o_ref, w_ref, *, layer):
    @pl.when((pl.program_id(0) == 0) & (pl.program_id(1) == 0))
    def _():
        for c0 in range(0, D_IN, PROJ_CHUNK):
            w_ref[:, c0:c0 + PROJ_CHUNK] = w32_ref[0, :, c0:c0 + PROJ_CHUNK].astype(w_ref.dtype)

    x = x_ref[0]
    ms = jnp.mean(x * x, axis=-1, keepdims=True)
    h = (x * lax.rsqrt(ms + EPS) * g_ref[layer:layer + 1, :]).astype(jnp.bfloat16)
    lane = lax.broadcasted_iota(jnp.int32, (1, LANES), 1)
    first_half = (lane % HEAD_DIM) < HALF

    def rope(t):
        partner = jnp.where(first_half, -pltpu.roll(t, LANES - HALF, 1), pltpu.roll(t, HALF, 1))
        return t * cos_ref[...] + partner * sin_ref[...]

    def store(col, t):
        o_ref[0, :, col:col + t.shape[1]] = t.astype(o_ref.dtype)

    def chunk(c0):
        return jnp.dot(h, w_ref[:, c0:c0 + PROJ_CHUNK], preferred_element_type=jnp.float32)

    g = D_GROUP
    acc = chunk(COL_A_U)
    store(P_H, acc[:, :g] * jax.nn.sigmoid(acc[:, g:]))
    acc = chunk(COL_A_Z)
    store(P_ZA, _silu(acc[:, :g]))
    store(P_BB, acc[:, g:])
    acc = chunk(COL_B_C)
    store(P_G, acc[:, :g] * acc[:, g:])
    acc = chunk(COL_B_Z)
    store(P_ZB, _silu(acc[:, :g]))
    for s0 in range(0, g, LANES):
        t = rope(acc[:, g + s0:g + s0 + LANES]) * Q_SCALE
        store(P_CQ + s0, t)
        store(P_CQS + s0, pltpu.roll(t, HEAD_DIM, 1))
    acc = chunk(COL_C_K)
    store(P_CK, rope(acc[:, :LANES]))
    store(P_CV, acc[:, LANES:g])
    store(P_ZC, _silu(acc[:, g:]))
    acc = chunk(COL_D_Q)
    store(P_DQ, acc[:, :g] * Q_SCALE)
    store(P_DK, acc[:, g:])
    acc = chunk(COL_D_V)
    store(P_DV, acc[:, :g])
    store(P_ZD, _silu(acc[:, g:]))


def _in_proj(x, g, w, cos, sin, layer):
    b, s, d = x.shape
    tm = TM_PROJ
    return pl.pallas_call(
        functools.partial(_in_proj_kernel, layer=layer),
        grid=(b, s // tm),
        in_specs=[
            pl.BlockSpec((1, tm, d), lambda bi, i: (bi, i, 0)),
            pl.BlockSpec(g.shape, lambda bi, i: (0, 0)),
            pl.BlockSpec((1, d, D_IN), lambda bi, i: (layer, 0, 0), pipeline_mode=pl.Buffered(1)),
            pl.BlockSpec((tm, LANES), lambda bi, i: (i, 0)),
            pl.BlockSpec((tm, LANES), lambda bi, i: (i, 0)),
        ],
        out_specs=pl.BlockSpec((1, tm, D_PROJ), lambda bi, i: (bi, i, 0)),
        out_shape=jax.ShapeDtypeStruct((b, s, D_PROJ), jnp.bfloat16),
        scratch_shapes=[pltpu.VMEM((d, D_IN), jnp.bfloat16)],
        compiler_params=pltpu.CompilerParams(
            dimension_semantics=("arbitrary", "arbitrary"), vmem_limit_bytes=VMEM_LIMIT),
        name="in_proj",
    )(x, g, w, cos, sin)


def _conv_out_kernel(cur_ref, prev_ref, next_ref, aw_ref, ab_ref, lg_ref, lb_ref, bw_ref,
                     ycd_ref, x_ref, w32_ref, g_ref, o_ref, w_ref, hbuf, gbuf, hph, gph,
                     *, layer, final_norm):
    @pl.when((pl.program_id(0) == 0) & (pl.program_id(1) == 0))
    def _():
        w_ref[...] = w32_ref[0].astype(w_ref.dtype)

    i = pl.program_id(1)
    n = pl.num_programs(1)
    t = T_MIX
    f32 = jnp.float32
    has_prev = (i > 0).astype(f32)
    has_next = (i < n - 1).astype(f32)
    row = lambda ref, j: ref[layer, j:j + 1, :]
    vec = lambda ref: ref[layer:layer + 1, :]
    col = lambda ref, rows, c: ref[0, rows, c:c + D_GROUP].astype(f32)

    full = slice(None)
    hwin = jnp.concatenate([col(prev_ref, full, P_H) * has_prev, col(cur_ref, full, P_H),
                            col(next_ref, full, P_H) * has_next], axis=0)
    gwin = jnp.concatenate([col(prev_ref, full, P_G) * has_prev, col(cur_ref, full, P_G),
                            col(next_ref, full, P_G) * has_next], axis=0)
    hbuf[...] = hwin
    gbuf[...] = gwin

    nwin = t + 2 * HALO
    nph = nwin - SUBLANES
    for p in range(1, SUBLANES):
        hph[p - 1] = pltpu.roll(hwin, nwin - p, 0)[0:nph]
    gph[0] = pltpu.roll(gwin, nwin - 1, 0)[0:nph]
    gph[1] = pltpu.roll(gwin, nwin - (SUBLANES - 1), 0)[0:nph]

    def tap(off, rows):
        p = off % SUBLANES
        if p == 0:
            return hbuf[off:off + rows, :]
        return hph[p - 1, off - p:off - p + rows, :]

    rc = CONV_ROWS
    for r0 in range(0, t, rc):
        rows = slice(r0, r0 + rc)
        acc = jnp.broadcast_to(vec(ab_ref), (rc, D_GROUP))
        for j in range(CONV_A_WIDTH):
            acc = acc + aw_ref[layer, j:j + 1, :] * tap(HALO - CONV_A_PAD + j + r0, rc)
        mu = jnp.mean(acc, axis=-1, keepdims=True)
        xc = acc - mu
        var = jnp.mean(xc * xc, axis=-1, keepdims=True)
        hn = xc * lax.rsqrt(var + EPS) * vec(lg_ref) + vec(lb_ref)
        ya = _silu(hn) * col(cur_ref, rows, P_ZA)
        gm = HALO + r0 - SUBLANES
        conv = (row(bw_ref, 0) * gph[1, gm:gm + rc, :]
                + row(bw_ref, 1) * gbuf[HALO + r0:HALO + r0 + rc, :]
                + row(bw_ref, 2) * gph[0, HALO + r0:HALO + r0 + rc, :])
        yb = col(cur_ref, rows, P_BB) * conv * col(cur_ref, rows, P_ZB)
        yab = jnp.concatenate([ya.astype(jnp.bfloat16), yb.astype(jnp.bfloat16)], axis=1)
        xo = x_ref[0, rows, :] + jnp.dot(ycd_ref[0, rows, :], w_ref[2 * D_GROUP:4 * D_GROUP, :],
                                         preferred_element_type=f32)
        xo = xo + jnp.dot(yab, w_ref[0:2 * D_GROUP, :], preferred_element_type=f32)
        if final_norm:
            ms = jnp.mean(xo * xo, axis=-1, keepdims=True)
            xo = xo * lax.rsqrt(ms + EPS) * g_ref[...]
        o_ref[0, rows, :] = xo


def _conv_out(proj, ycd, x, aw, ab, lg, lb, bw, w, g, layer, final_norm):
    b, s, d = x.shape
    t = T_MIX
    hb = t // HALO
    nh = s // HALO
    f32 = jnp.float32
    whole = lambda a: pl.BlockSpec(a.shape, lambda bi, i: (0,) * a.ndim)
    return pl.pallas_call(
        functools.partial(_conv_out_kernel, layer=layer, final_norm=final_norm),
        grid=(b, s // t),
        in_specs=[
            pl.BlockSpec((1, t, W_AB), lambda bi, i: (bi, i, 0)),
            pl.BlockSpec((1, HALO, W_AB), lambda bi, i: (bi, jnp.maximum(i * hb - 1, 0), 0)),
            pl.BlockSpec((1, HALO, W_AB), lambda bi, i: (bi, jnp.minimum((i + 1) * hb, nh - 1), 0)),
            whole(aw), whole(ab), whole(lg), whole(lb), whole(bw),
            pl.BlockSpec((1, t, 2 * D_GROUP), lambda bi, i: (bi, i, 0)),
            pl.BlockSpec((1, t, d), lambda bi, i: (bi, i, 0)),
            pl.BlockSpec((1,) + w.shape[1:], lambda bi, i: (layer, 0, 0),
                         pipeline_mode=pl.Buffered(1)),
            pl.BlockSpec((1, d), lambda bi, i: (0, 0)),
        ],
        out_specs=pl.BlockSpec((1, t, d), lambda bi, i: (bi, i, 0)),
        out_shape=jax.ShapeDtypeStruct((b, s, d), f32),
        scratch_shapes=[pltpu.VMEM(w.shape[1:], jnp.bfloat16),
                        pltpu.VMEM((t + 2 * HALO, D_GROUP), f32),
                        pltpu.VMEM((t + 2 * HALO, D_GROUP), f32),
                        pltpu.VMEM((SUBLANES - 1, t + 2 * HALO - SUBLANES, D_GROUP), f32),
                        pltpu.VMEM((2, t + 2 * HALO - SUBLANES, D_GROUP), f32)],
        compiler_params=pltpu.CompilerParams(
            dimension_semantics=("arbitrary", "arbitrary"), vmem_limit_bytes=VMEM_LIMIT),
        name="conv_out",
    )(proj, proj, proj, aw, ab, lg, lb, bw, ycd, x, w, g)


def _swa_tile(sink_ref, q_ref, qs_ref, z_ref, kv_ref, y_ref, i, layer):
    s_len = kv_ref.shape[1]
    nkeys = 3 * SWA_BLOCK
    f32 = jnp.float32
    lane = lax.broadcasted_iota(jnp.int32, (1, LANES), 1)
    lo = lane < HEAD_DIM
    zero = jnp.zeros((), jnp.bfloat16)
    ones = jnp.ones((nkeys, LANES), jnp.bfloat16)
    for nb in range(T_ATTN // SWA_BLOCK):
        q0 = i * T_ATTN + nb * SWA_BLOCK
        ws = pl.multiple_of(jnp.clip(q0 - SWA_BLOCK, 0, s_len - nkeys), SWA_BLOCK)
        rows = slice(nb * SWA_BLOCK, (nb + 1) * SWA_BLOCK)
        win = pl.ds(ws, nkeys)
        lhs = [
            jnp.where(lo, q_ref[0, rows, 0:LANES], zero),
            jnp.where(lo, qs_ref[0, rows, 0:LANES], zero),
            jnp.where(lo, zero, qs_ref[0, rows, LANES:2 * LANES]),
            jnp.where(lo, zero, q_ref[0, rows, LANES:2 * LANES]),
        ]
        kw = kv_ref[0, win, 0:LANES]
        vw = jnp.concatenate([kv_ref[0, win, LANES:2 * LANES], ones], axis=1)
        qpos = q0 + lax.broadcasted_iota(jnp.int32, (SWA_BLOCK, nkeys), 0)
        kpos = ws + lax.broadcasted_iota(jnp.int32, (SWA_BLOCK, nkeys), 1)
        valid = jnp.abs(kpos - qpos) <= SWA_WINDOW
        num, den = [], []
        for h in range(SWA_HEADS):
            sc = jnp.einsum("qd,kd->qk", lhs[h], kw, preferred_element_type=f32)
            sh = jnp.where(valid, sc, NEG_INF)
            sink = sink_ref[layer, h]
            m = jnp.maximum(jnp.max(sh, axis=-1, keepdims=True), sink)
            e = jnp.exp2(sh - m)
            pv = jnp.dot(e.astype(jnp.bfloat16), vw, preferred_element_type=f32)
            num.append(pv[:, :LANES])
            den.append(pv[:, LANES:] + jnp.exp2(sink - m))
        og0 = jnp.where(lo, num[0], pltpu.roll(num[1], HEAD_DIM, 1)) / jnp.where(lo, den[0], den[1])
        og1 = jnp.where(lo, pltpu.roll(num[2], HEAD_DIM, 1), num[3]) / jnp.where(lo, den[2], den[3])
        y = jnp.concatenate([og0, og1], axis=-1) * z_ref[0, rows, :].astype(f32)
        y_ref[rows, 0:D_GROUP] = y.astype(y_ref.dtype)


def _nbr_tile(q_ref, z_ref, k_ref, v_ref, bias_ref, y_ref, i):
    rows_total = k_ref.shape[1] // GRID_W
    rows_tile = T_ATTN // GRID_W
    nkeys = NA_KH * GRID_W
    f32 = jnp.float32
    lane = lax.broadcasted_iota(jnp.int32, (1, LANES), 1)
    lo = lane < HEAD_DIM
    zero = jnp.zeros((), jnp.bfloat16)
    ones = jnp.ones((nkeys, LANES), jnp.bfloat16)

    for rr in range(rows_tile):
        r = i * rows_tile + rr
        r0 = jnp.clip(r - NA_KH // 2, 0, rows_total - NA_KH)
        d0 = r0 - r + (NA_KH - 1)
        ks = pl.multiple_of(r0 * GRID_W, GRID_W)
        qrow = slice(rr * GRID_W, (rr + 1) * GRID_W)
        outs = []
        for g in range(2):
            cols = slice(g * LANES, (g + 1) * LANES)
            qg = q_ref[0, qrow, cols]
            lhs = jnp.concatenate([jnp.where(lo, qg, zero), jnp.where(lo, zero, qg)], axis=0)
            kw = k_ref[0, pl.ds(ks, nkeys), cols]
            vw = jnp.concatenate([v_ref[0, pl.ds(ks, nkeys), cols], ones], axis=1)
            sc = jnp.einsum("qd,kd->qk", lhs, kw, preferred_element_type=f32)
            bias = jnp.concatenate(
                [jnp.concatenate([bias_ref[0, 2 * g + hh, d0 + 2 * m]
                                  for m in range(NA_KH // 2)], axis=-1)
                 for hh in range(2)], axis=0)
            sc = sc + bias
            m_ = jnp.max(sc, axis=-1, keepdims=True)
            e = jnp.exp2(sc - m_)
            pv = jnp.dot(e.astype(jnp.bfloat16), vw, preferred_element_type=f32)
            pv = jnp.where(jnp.concatenate([lo, lo], axis=1), pv[0:GRID_W], pv[GRID_W:2 * GRID_W])
            outs.append(pv[:, :LANES] / pv[:, LANES:])
        y = jnp.concatenate(outs, axis=-1) * z_ref[0, qrow, :].astype(f32)
        y_ref[qrow, D_GROUP:2 * D_GROUP] = y.astype(y_ref.dtype)


def _attn_kernel(sink_ref, cq_ref, cqs_ref, cz_ref, ckv_ref, dq_ref, dz_ref, dk_ref, dv_ref,
                 bias_ref, o_ref, *, layer):
    i = pl.program_id(1)
    y_ref = o_ref.at[0]
    _swa_tile(sink_ref, cq_ref, cqs_ref, cz_ref, ckv_ref, y_ref, i, layer)
    _nbr_tile(dq_ref, dz_ref, dk_ref, dv_ref, bias_ref, y_ref, i)


def _attn(proj, sink, bias, layer):
    b, s, _ = proj.shape
    t = T_ATTN
    tile = lambda col: pl.BlockSpec((1, t, D_GROUP), lambda bi, i, sk: (bi, i, col // D_GROUP))
    seq = lambda col: pl.BlockSpec((1, s, D_GROUP), lambda bi, i, sk: (bi, 0, col // D_GROUP))
    return pl.pallas_call(
        functools.partial(_attn_kernel, layer=layer),
        grid_spec=pltpu.PrefetchScalarGridSpec(
            num_scalar_prefetch=1,
            grid=(b, s // t),
            in_specs=[
                tile(P_CQ), tile(P_CQS), tile(P_ZC), seq(P_CK),
                tile(P_DQ), tile(P_ZD), seq(P_DK), seq(P_DV),
                pl.BlockSpec((1,) + bias.shape[1:], lambda bi, i, sk: (layer, 0, 0, 0, 0)),
            ],
            out_specs=pl.BlockSpec((1, t, 2 * D_GROUP), lambda bi, i, sk: (bi, i, 0)),
        ),
        out_shape=jax.ShapeDtypeStruct((b, s, 2 * D_GROUP), jnp.bfloat16),
        compiler_params=_params(),
        name="attn",
    )(sink, proj, proj, proj, proj, proj, proj, proj, proj, bias)


def _rope_tables(s):
    inv_freq = ROPE_THETA ** (-jnp.arange(0, HEAD_DIM, 2, dtype=jnp.float32) / HEAD_DIM)
    inv_freq = jnp.tile(inv_freq, LANES // HALF)
    hi = jnp.arange(0, s, GRID_W, dtype=jnp.float32)[:, None, None] * inv_freq
    lo = jnp.arange(GRID_W, dtype=jnp.float32)[None, :, None] * inv_freq
    ch, sh, cl, sl = jnp.cos(hi), jnp.sin(hi), jnp.cos(lo), jnp.sin(lo)
    cos = ch * cl - sh * sl
    sin = sh * cl + ch * sl
    return cos.reshape(s, LANES), sin.reshape(s, LANES)


def _nbr_bias_table(rpb):
    nrel = 2 * NA_KW - 1
    ndr = 2 * NA_KH - 1
    c = np.arange(GRID_W)
    c0 = np.clip(c - NA_KW // 2, 0, GRID_W - NA_KW)
    col_ok = (c[None, :] >= c0[:, None]) & (c[None, :] < c0[:, None] + NA_KW)
    dc = np.clip(c[None, :] - c[:, None], -(NA_KW - 1), NA_KW - 1) + (NA_KW - 1)
    onehot = (dc[None] == np.arange(nrel)[:, None, None]).astype(np.float32)
    pick = np.zeros((2, nrel, GRID_W, 2, GRID_W), np.float32)
    pick[0, :, :, 0, :] = onehot
    pick[1, :, :, 1, :] = onehot
    pick = pick.reshape(2 * nrel, GRID_W, LANES)
    dr = np.arange(ndr + 1)[:, None] + np.arange(2)[None, :]
    valid = (dr < ndr)[:, None, :, None] & col_ok[None, :, None, :]
    valid = valid.reshape(ndr + 1, GRID_W, LANES)
    ext = jnp.pad(rpb, ((0, 0), (0, 0), (0, 2), (0, 0)))
    pair = jnp.concatenate([ext[:, :, :-1], ext[:, :, 1:]], axis=-1)
    full = jnp.einsum("lhdc,cqj->lhdqj", pair, jnp.asarray(pick),
                      precision=lax.Precision.HIGHEST)
    return jnp.where(jnp.asarray(valid), full * LOG2E, NEG_INF)


def kernel(x, norm_g, w_in, w_out, conv_a_w, conv_a_b, ln_a_g, ln_a_b, conv_b_w, swa_sink,
           na_rpb, final_norm_g):
    depth = norm_g.shape[0]
    s = x.shape[1]
    assert s % max(TM_PROJ, T_MIX, T_ATTN) == 0 and s % GRID_W == 0, x.shape
    assert s // GRID_W >= NA_KH and s >= 3 * SWA_BLOCK, x.shape
    assert w_in.shape[1:] == (x.shape[2], D_IN) and w_out.shape[1:] == (4 * D_GROUP, x.shape[2])
    cos_t, sin_t = _rope_tables(s)
    bias = _nbr_bias_table(na_rpb)
    sink = swa_sink * LOG2E
    fg = final_norm_g.reshape(1, -1)
    for l in range(depth):
        proj = _in_proj(x, norm_g, w_in, cos_t, sin_t, l)
        ycd = _attn(proj, sink, bias, l)
        x = _conv_out(proj, ycd, x, conv_a_w, conv_a_b, ln_a_g, ln_a_b, conv_b_w,
                      w_out, fg, l, final_norm=(l == depth - 1))
    return x
```

```python
import functools
import math

import numpy as np
import jax
import jax.numpy as jnp
from jax import lax
from jax.experimental import pallas as pl
from jax.experimental.pallas import tpu as pltpu

D_GROUP = 256
HEAD_DIM = 64
HALF = HEAD_DIM // 2
GRID_W = 64
CONV_A_WIDTH = 31
CONV_A_PAD = (CONV_A_WIDTH - 1) // 2
SWA_WINDOW = 128
SWA_BLOCK = 128
SWA_HEADS = 4
NA_KH = 8
NA_KW = 16
ROPE_THETA = 10000.0
EPS = 1e-6
NEG_INF = -1e30
LOG2E = math.log2(math.e)
Q_SCALE = HEAD_DIM ** -0.5 * LOG2E

COL_A_U, COL_A_V, COL_A_Z = 0, 256, 512
COL_B_B, COL_B_C, COL_B_X, COL_B_Z = 768, 1024, 1280, 1536
COL_C_Q, COL_C_K, COL_C_V, COL_C_Z = 1792, 2048, 2176, 2304
COL_D_Q, COL_D_K, COL_D_V, COL_D_Z = 2560, 2816, 3072, 3328
D_IN = 3584

P_H, P_ZA = 0, 256
P_BZ, P_G = 512, 768
P_CQ, P_CQS, P_CK, P_CV, P_ZC = 1024, 1280, 1536, 1664, 1792
P_DQ, P_DK, P_DV, P_ZD = 2048, 2304, 2560, 2816
D_PROJ = 3072
W_AB = P_CQ

LANES = 128
SUBLANES = 8
HALO = 16
VMEM_LIMIT = 56 * 1024 * 1024

TM_PROJ = 1024
T_MIX = 1024
T_ATTN = 1024
PROJ_CHUNK = 512
CONV_ROWS = 256


def _silu(x):
    return x * jax.nn.sigmoid(x)


def _params():
    return pltpu.CompilerParams(
        dimension_semantics=("parallel", "parallel"), vmem_limit_bytes=VMEM_LIMIT)


def _in_proj_kernel(x_ref, g_ref, w32_ref, cos_ref, sin_ref, o_ref, w_ref, *, layer):
    @pl.when((pl.program_id(0) == 0) & (pl.program_id(1) == 0))
    def _():
        for c0 in range(0, D_IN, PROJ_CHUNK):
            w_ref[:, c0:c0 + PROJ_CHUNK] = w32_ref[0, :, c0:c0 + PROJ_CHUNK].astype(w_ref.dtype)

    x = x_ref[0]
    ms = jnp.mean(x * x, axis=-1, keepdims=True)
    h = (x * lax.rsqrt(ms + EPS) * g_ref[layer:layer + 1, :]).astype(jnp.bfloat16)
    lane = lax.broadcasted_iota(jnp.int32, (1, LANES), 1)
    first_half = (lane % HEAD_DIM) < HALF

    def rope(t):
        partner = jnp.where(first_half, -pltpu.roll(t, LANES - HALF, 1), pltpu.roll(t, HALF, 1))
        return t * cos_ref[...] + partner * sin_ref[...]

    def store(col, t):
        o_ref[0, :, col:col + t.shape[1]] = t.astype(o_ref.dtype)

    def chunk(c0):
        return jnp.dot(h, w_ref[:, c0:c0 + PROJ_CHUNK], preferred_element_type=jnp.float32)

    g = D_GROUP
    acc = chunk(COL_A_U)
    store(P_H, acc[:, :g] * jax.nn.sigmoid(acc[:, g:]))
    acc = chunk(COL_A_Z)
    store(P_ZA, _silu(acc[:, :g]))
    b_gate = acc[:, g:]
    acc = chunk(COL_B_C)
    store(P_G, acc[:, :g] * acc[:, g:])
    acc = chunk(COL_B_Z)
    store(P_BZ, b_gate * _silu(acc[:, :g]))
    for s0 in range(0, g, LANES):
        t = rope(acc[:, g + s0:g + s0 + LANES]) * Q_SCALE
        store(P_CQ + s0, t)
        store(P_CQS + s0, pltpu.roll(t, HEAD_DIM, 1))
    acc = chunk(COL_C_K)
    store(P_CK, rope(acc[:, :LANES]))
    store(P_CV, acc[:, LANES:g])
    store(P_ZC, _silu(acc[:, g:]))
    acc = chunk(COL_D_Q)
    store(P_DQ, acc[:, :g] * Q_SCALE)
    store(P_DK, acc[:, g:])
    acc = chunk(COL_D_V)
    store(P_DV, acc[:, :g])
    store(P_ZD, _silu(acc[:, g:]))


def _in_proj(x, g, w, cos, sin, layer):
    b, s, d = x.shape
    tm = TM_PROJ
    return pl.pallas_call(
        functools.partial(_in_proj_kernel, layer=layer),
        grid=(b, s // tm),
        in_specs=[
            pl.BlockSpec((1, tm, d), lambda bi, i: (bi, i, 0)),
            pl.BlockSpec(g.shape, lambda bi, i: (0, 0)),
            pl.BlockSpec((1, d, D_IN), lambda bi, i: (layer, 0, 0), pipeline_mode=pl.Buffered(1)),
            pl.BlockSpec((tm, LANES), lambda bi, i: (i, 0)),
            pl.BlockSpec((tm, LANES), lambda bi, i: (i, 0)),
        ],
        out_specs=pl.BlockSpec((1, tm, D_PROJ), lambda bi, i: (bi, i, 0)),
        out_shape=jax.ShapeDtypeStruct((b, s, D_PROJ), jnp.bfloat16),
        scratch_shapes=[pltpu.VMEM((d, D_IN), jnp.bfloat16)],
        compiler_params=pltpu.CompilerParams(
            dimension_semantics=("arbitrary", "arbitrary"), vmem_limit_bytes=VMEM_LIMIT),
        name="in_proj",
    )(x, g, w, cos, sin)


def _conv_out_kernel(cur_ref, prev_ref, next_ref, aw_ref, ab_ref, lg_ref, lb_ref, bw_ref,
                     ycd_ref, x_ref, w_ref, g_ref, o_ref, hbuf, gbuf, hph, gph,
                     *, layer, final_norm):
    i = pl.program_id(1)
    n = pl.num_programs(1)
    t = T_MIX
    f32 = jnp.float32
    has_prev = (i > 0).astype(f32)
    has_next = (i < n - 1).astype(f32)
    row = lambda ref, j: ref[layer, j:j + 1, :]
    vec = lambda ref: ref[layer:layer + 1, :]
    col = lambda ref, rows, c: ref[0, rows, c:c + D_GROUP].astype(f32)

    full = slice(None)
    hwin = jnp.concatenate([col(prev_ref, full, P_H) * has_prev, col(cur_ref, full, P_H),
                            col(next_ref, full, P_H) * has_next], axis=0)
    gwin = jnp.concatenate([col(prev_ref, full, P_G) * has_prev, col(cur_ref, full, P_G),
                            col(next_ref, full, P_G) * has_next], axis=0)
    hbuf[...] = hwin
    gbuf[...] = gwin

    nwin = t + 2 * HALO
    nph = nwin - SUBLANES
    for p in range(1, SUBLANES):
        hph[p - 1] = pltpu.roll(hwin, nwin - p, 0)[0:nph]
    gph[0] = pltpu.roll(gwin, nwin - 1, 0)[0:nph]
    gph[1] = pltpu.roll(gwin, nwin - (SUBLANES - 1), 0)[0:nph]

    def tap(off, rows):
        p = off % SUBLANES
        if p == 0:
            return hbuf[off:off + rows, :]
        return hph[p - 1, off - p:off - p + rows, :]

    rc = CONV_ROWS
    for r0 in range(0, t, rc):
        rows = slice(r0, r0 + rc)
        acc = jnp.broadcast_to(vec(ab_ref), (rc, D_GROUP))
        for j in range(CONV_A_WIDTH):
            acc = acc + aw_ref[layer, j:j + 1, :] * tap(HALO - CONV_A_PAD + j + r0, rc)
        mu = jnp.mean(acc, axis=-1, keepdims=True)
        xc = acc - mu
        var = jnp.mean(xc * xc, axis=-1, keepdims=True)
        hn = xc * lax.rsqrt(var + EPS) * vec(lg_ref) + vec(lb_ref)
        ya = _silu(hn) * col(cur_ref, rows, P_ZA)
        gm = HALO + r0 - SUBLANES
        conv = (row(bw_ref, 0) * gph[1, gm:gm + rc, :]
                + row(bw_ref, 1) * gbuf[HALO + r0:HALO + r0 + rc, :]
                + row(bw_ref, 2) * gph[0, HALO + r0:HALO + r0 + rc, :])
        yb = col(cur_ref, rows, P_BZ) * conv
        yab = jnp.concatenate([ya.astype(jnp.bfloat16), yb.astype(jnp.bfloat16)], axis=1)
        xo = x_ref[0, rows, :] + jnp.dot(ycd_ref[0, rows, :], w_ref[0, 2 * D_GROUP:4 * D_GROUP, :],
                                         preferred_element_type=f32)
        xo = xo + jnp.dot(yab, w_ref[0, 0:2 * D_GROUP, :], preferred_element_type=f32)
        if final_norm:
            ms = jnp.mean(xo * xo, axis=-1, keepdims=True)
            xo = xo * lax.rsqrt(ms + EPS) * g_ref[...]
        o_ref[0, rows, :] = xo


def _conv_out(proj, ycd, x, aw, ab, lg, lb, bw, w, g, layer, final_norm):
    b, s, d = x.shape
    t = T_MIX
    hb = t // HALO
    nh = s // HALO
    f32 = jnp.float32
    whole = lambda a: pl.BlockSpec(a.shape, lambda bi, i: (0,) * a.ndim)
    return pl.pallas_call(
        functools.partial(_conv_out_kernel, layer=layer, final_norm=final_norm),
        grid=(b, s // t),
        in_specs=[
            pl.BlockSpec((1, t, W_AB), lambda bi, i: (bi, i, 0)),
            pl.BlockSpec((1, HALO, W_AB), lambda bi, i: (bi, jnp.maximum(i * hb - 1, 0), 0)),
            pl.BlockSpec((1, HALO, W_AB), lambda bi, i: (bi, jnp.minimum((i + 1) * hb, nh - 1), 0)),
            whole(aw), whole(ab), whole(lg), whole(lb), whole(bw),
            pl.BlockSpec((1, t, 2 * D_GROUP), lambda bi, i: (bi, i, 0)),
            pl.BlockSpec((1, t, d), lambda bi, i: (bi, i, 0)),
            pl.BlockSpec((1,) + w.shape[1:], lambda bi, i: (layer, 0, 0)),
            pl.BlockSpec((1, d), lambda bi, i: (0, 0)),
        ],
        out_specs=pl.BlockSpec((1, t, d), lambda bi, i: (bi, i, 0)),
        out_shape=jax.ShapeDtypeStruct((b, s, d), f32),
        scratch_shapes=[pltpu.VMEM((t + 2 * HALO, D_GROUP), f32),
                        pltpu.VMEM((t + 2 * HALO, D_GROUP), f32),
                        pltpu.VMEM((SUBLANES - 1, t + 2 * HALO - SUBLANES, D_GROUP), f32),
                        pltpu.VMEM((2, t + 2 * HALO - SUBLANES, D_GROUP), f32)],
        compiler_params=_params(),
        name="conv_out",
    )(proj, proj, proj, aw, ab, lg, lb, bw, ycd, x, w, g)


def _swa_tile(sink_ref, q_ref, qs_ref, z_ref, kv_ref, y_ref, i, layer):
    s_len = kv_ref.shape[1]
    nkeys = 3 * SWA_BLOCK
    f32 = jnp.float32
    lane = lax.broadcasted_iota(jnp.int32, (1, LANES), 1)
    lo = lane < HEAD_DIM
    zero = jnp.zeros((), jnp.bfloat16)
    ones = jnp.ones((nkeys, LANES), jnp.bfloat16)
    for nb in range(T_ATTN // SWA_BLOCK):
        q0 = i * T_ATTN + nb * SWA_BLOCK
        ws = pl.multiple_of(jnp.clip(q0 - SWA_BLOCK, 0, s_len - nkeys), SWA_BLOCK)
        rows = slice(nb * SWA_BLOCK, (nb + 1) * SWA_BLOCK)
        win = pl.ds(ws, nkeys)
        lhs = [
            jnp.where(lo, q_ref[0, rows, 0:LANES], zero),
            jnp.where(lo, qs_ref[0, rows, 0:LANES], zero),
            jnp.where(lo, zero, qs_ref[0, rows, LANES:2 * LANES]),
            jnp.where(lo, zero, q_ref[0, rows, LANES:2 * LANES]),
        ]
        kw = kv_ref[0, win, 0:LANES]
        vw = jnp.concatenate([kv_ref[0, win, LANES:2 * LANES], ones], axis=1)
        qpos = q0 + lax.broadcasted_iota(jnp.int32, (SWA_BLOCK, nkeys), 0)
        kpos = ws + lax.broadcasted_iota(jnp.int32, (SWA_BLOCK, nkeys), 1)
        valid = jnp.abs(kpos - qpos) <= SWA_WINDOW
        num, den = [], []
        for h in range(SWA_HEADS):
            sc = jnp.einsum("qd,kd->qk", lhs[h], kw, preferred_element_type=f32)
            sh = jnp.where(valid, sc, NEG_INF)
            sink = sink_ref[layer, h]
            m = jnp.maximum(jnp.max(sh, axis=-1, keepdims=True), sink)
            e = jnp.exp2(sh - m)
            pv = jnp.dot(e.astype(jnp.bfloat16), vw, preferred_element_type=f32)
            num.append(pv[:, :LANES])
            den.append(pv[:, LANES:] + jnp.exp2(sink - m))
        og0 = jnp.where(lo, num[0], pltpu.roll(num[1], HEAD_DIM, 1)) / jnp.where(lo, den[0], den[1])
        og1 = jnp.where(lo, pltpu.roll(num[2], HEAD_DIM, 1), num[3]) / jnp.where(lo, den[2], den[3])
        y = jnp.concatenate([og0, og1], axis=-1) * z_ref[0, rows, :].astype(f32)
        y_ref[rows, 0:D_GROUP] = y.astype(y_ref.dtype)


def _nbr_tile(q_ref, z_ref, k_ref, v_ref, bias_ref, y_ref, i):
    rows_total = k_ref.shape[1] // GRID_W
    rows_tile = T_ATTN // GRID_W
    nkeys = NA_KH * GRID_W
    f32 = jnp.float32
    lane = lax.broadcasted_iota(jnp.int32, (1, LANES), 1)
    lo = lane < HEAD_DIM
    zero = jnp.zeros((), jnp.bfloat16)
    ones = jnp.ones((nkeys, LANES), jnp.bfloat16)

    for rr in range(rows_tile):
        r = i * rows_tile + rr
        r0 = jnp.clip(r - NA_KH // 2, 0, rows_total - NA_KH)
        d0 = r0 - r + (NA_KH - 1)
        ks = pl.multiple_of(r0 * GRID_W, GRID_W)
        qrow = slice(rr * GRID_W, (rr + 1) * GRID_W)
        outs = []
        for g in range(2):
            cols = slice(g * LANES, (g + 1) * LANES)
            qg = q_ref[0, qrow, cols]
            lhs = jnp.concatenate([jnp.where(lo, qg, zero), jnp.where(lo, zero, qg)], axis=0)
            kw = k_ref[0, pl.ds(ks, nkeys), cols]
            vw = jnp.concatenate([v_ref[0, pl.ds(ks, nkeys), cols], ones], axis=1)
            sc = jnp.einsum("qd,kd->qk", lhs, kw, preferred_element_type=f32)
            bias = jnp.concatenate(
                [jnp.concatenate([bias_ref[0, 2 * g + hh, d0 + 2 * m]
                                  for m in range(NA_KH // 2)], axis=-1)
                 for hh in range(2)], axis=0)
            sc = sc + bias
            m_ = jnp.max(sc, axis=-1, keepdims=True)
            e = jnp.exp2(sc - m_)
            pv = jnp.dot(e.astype(jnp.bfloat16), vw, preferred_element_type=f32)
            pv = jnp.where(jnp.concatenate([lo, lo], axis=1), pv[0:GRID_W], pv[GRID_W:2 * GRID_W])
            outs.append(pv[:, :LANES] / pv[:, LANES:])
        y = jnp.concatenate(outs, axis=-1) * z_ref[0, qrow, :].astype(f32)
        y_ref[qrow, D_GROUP:2 * D_GROUP] = y.astype(y_ref.dtype)


def _attn_kernel(sink_ref, cq_ref, cqs_ref, cz_ref, ckv_ref, dq_ref, dz_ref, dk_ref, dv_ref,
                 bias_ref, o_ref, *, layer):
    i = pl.program_id(1)
    y_ref = o_ref.at[0]
    _swa_tile(sink_ref, cq_ref, cqs_ref, cz_ref, ckv_ref, y_ref, i, layer)
    _nbr_tile(dq_ref, dz_ref, dk_ref, dv_ref, bias_ref, y_ref, i)


def _attn(proj, sink, bias, layer):
    b, s, _ = proj.shape
    t = T_ATTN
    tile = lambda col: pl.BlockSpec((1, t, D_GROUP), lambda bi, i, sk: (bi, i, col // D_GROUP))
    seq = lambda col: pl.BlockSpec((1, s, D_GROUP), lambda bi, i, sk: (bi, 0, col // D_GROUP))
    return pl.pallas_call(
        functools.partial(_attn_kernel, layer=layer),
        grid_spec=pltpu.PrefetchScalarGridSpec(
            num_scalar_prefetch=1,
            grid=(b, s // t),
            in_specs=[
                tile(P_CQ), tile(P_CQS), tile(P_ZC), seq(P_CK),
                tile(P_DQ), tile(P_ZD), seq(P_DK), seq(P_DV),
                pl.BlockSpec((1,) + bias.shape[1:], lambda bi, i, sk: (layer, 0, 0, 0, 0)),
            ],
            out_specs=pl.BlockSpec((1, t, 2 * D_GROUP), lambda bi, i, sk: (bi, i, 0)),
        ),
        out_shape=jax.ShapeDtypeStruct((b, s, 2 * D_GROUP), jnp.bfloat16),
        compiler_params=_params(),
        name="attn",
    )(sink, proj, proj, proj, proj, proj, proj, proj, proj, bias)


def _rope_tables(s):
    inv_freq = ROPE_THETA ** (-jnp.arange(0, HEAD_DIM, 2, dtype=jnp.float32) / HEAD_DIM)
    inv_freq = jnp.tile(inv_freq, LANES // HALF)
    hi = jnp.arange(0, s, GRID_W, dtype=jnp.float32)[:, None, None] * inv_freq
    lo = jnp.arange(GRID_W, dtype=jnp.float32)[None, :, None] * inv_freq
    ch, sh, cl, sl = jnp.cos(hi), jnp.sin(hi), jnp.cos(lo), jnp.sin(lo)
    cos = ch * cl - sh * sl
    sin = sh * cl + ch * sl
    return cos.reshape(s, LANES), sin.reshape(s, LANES)


def _nbr_bias_table(rpb):
    nrel = 2 * NA_KW - 1
    ndr = 2 * NA_KH - 1
    c = np.arange(GRID_W)
    c0 = np.clip(c - NA_KW // 2, 0, GRID_W - NA_KW)
    col_ok = (c[None, :] >= c0[:, None]) & (c[None, :] < c0[:, None] + NA_KW)
    dc = np.clip(c[None, :] - c[:, None], -(NA_KW - 1), NA_KW - 1) + (NA_KW - 1)
    onehot = (dc[None] == np.arange(nrel)[:, None, None]).astype(np.float32)
    pick = np.zeros((2, nrel, GRID_W, 2, GRID_W), np.float32)
    pick[0, :, :, 0, :] = onehot
    pick[1, :, :, 1, :] = onehot
    pick = pick.reshape(2 * nrel, GRID_W, LANES)
    dr = np.arange(ndr + 1)[:, None] + np.arange(2)[None, :]
    valid = (dr < ndr)[:, None, :, None] & col_ok[None, :, None, :]
    valid = valid.reshape(ndr + 1, GRID_W, LANES)
    ext = jnp.pad(rpb, ((0, 0), (0, 0), (0, 2), (0, 0)))
    pair = jnp.concatenate([ext[:, :, :-1], ext[:, :, 1:]], axis=-1)
    full = jnp.einsum("lhdc,cqj->lhdqj", pair, jnp.asarray(pick),
                      precision=lax.Precision.HIGHEST)
    return jnp.where(jnp.asarray(valid), full * LOG2E, NEG_INF)


def kernel(x, norm_g, w_in, w_out, conv_a_w, conv_a_b, ln_a_g, ln_a_b, conv_b_w, swa_sink,
           na_rpb, final_norm_g):
    depth = norm_g.shape[0]
    s = x.shape[1]
    assert s % max(TM_PROJ, T_MIX, T_ATTN) == 0 and s % GRID_W == 0, x.shape
    assert s // GRID_W >= NA_KH and s >= 3 * SWA_BLOCK, x.shape
    assert w_in.shape[1:] == (x.shape[2], D_IN) and w_out.shape[1:] == (4 * D_GROUP, x.shape[2])
    cos_t, sin_t = _rope_tables(s)
    w_out_b = w_out.astype(jnp.bfloat16)
    bias = _nbr_bias_table(na_rpb)
    sink = swa_sink * LOG2E
    fg = final_norm_g.reshape(1, -1)
    for l in range(depth):
        proj = _in_proj(x, norm_g, w_in, cos_t, sin_t, l)
        ycd = _attn(proj, sink, bias, l)
        x = _conv_out(proj, ycd, x, conv_a_w, conv_a_b, ln_a_g, ln_a_b, conv_b_w,
                      w_out_b, fg, l, final_norm=(l == depth - 1))
    return x
```

```python
import functools
import math

import numpy as np
import jax
import jax.numpy as jnp
from jax import lax
from jax.experimental import pallas as pl
from jax.experimental.pallas import tpu as pltpu

D_GROUP = 256
HEAD_DIM = 64
HALF = HEAD_DIM // 2
GRID_W = 64
CONV_A_WIDTH = 31
CONV_A_PAD = (CONV_A_WIDTH - 1) // 2
SWA_WINDOW = 128
SWA_BLOCK = 128
SWA_HEADS = 4
NA_KH = 8
NA_KW = 16
ROPE_THETA = 10000.0
EPS = 1e-6
NEG_INF = -1e30
LOG2E = math.log2(math.e)
Q_SCALE = HEAD_DIM ** -0.5 * LOG2E

COL_A_U, COL_A_V, COL_A_Z = 0, 256, 512
COL_B_B, COL_B_C, COL_B_X, COL_B_Z = 768, 1024, 1280, 1536
COL_C_Q, COL_C_K, COL_C_V, COL_C_Z = 1792, 2048, 2176, 2304
COL_D_Q, COL_D_K, COL_D_V, COL_D_Z = 2560, 2816, 3072, 3328
D_IN = 3584

P_H, P_ZA = 0, 256
P_BZ, P_G = 512, 768
P_CQ, P_CQS, P_CK, P_CV, P_ZC = 1024, 1280, 1536, 1664, 1792
P_DQ, P_DK, P_DV, P_ZD = 2048, 2304, 2560, 2816
D_PROJ = 3072
W_AB = P_CQ

LANES = 128
SUBLANES = 8
HALO = 16
VMEM_LIMIT = 56 * 1024 * 1024

TM_PROJ = 1024
T_MIX = 512
T_ATTN = 1024
PROJ_CHUNK = 512
CONV_ROWS = 256


def _silu(x):
    return x * jax.nn.sigmoid(x)


def _params():
    return pltpu.CompilerParams(
        dimension_semantics=("parallel", "parallel"), vmem_limit_bytes=VMEM_LIMIT)


def _in_proj_kernel(x_ref, g_ref, w32_ref, cos_ref, sin_ref, o_ref, w_ref, *, layer):
    @pl.when((pl.program_id(0) == 0) & (pl.program_id(1) == 0))
    def _():
        for c0 in range(0, D_IN, PROJ_CHUNK):
            w_ref[:, c0:c0 + PROJ_CHUNK] = w32_ref[0, :, c0:c0 + PROJ_CHUNK].astype(w_ref.dtype)

    x = x_ref[0]
    ms = jnp.mean(x * x, axis=-1, keepdims=True)
    h = (x * lax.rsqrt(ms + EPS) * g_ref[layer:layer + 1, :]).astype(jnp.bfloat16)
    lane = lax.broadcasted_iota(jnp.int32, (1, LANES), 1)
    first_half = (lane % HEAD_DIM) < HALF

    def rope(t):
        partner = jnp.where(first_half, -pltpu.roll(t, LANES - HALF, 1), pltpu.roll(t, HALF, 1))
        return t * cos_ref[...] + partner * sin_ref[...]

    def store(col, t):
        o_ref[0, :, col:col + t.shape[1]] = t.astype(o_ref.dtype)

    def chunk(c0):
        return jnp.dot(h, w_ref[:, c0:c0 + PROJ_CHUNK], preferred_element_type=jnp.float32)

    g = D_GROUP
    acc = chunk(COL_A_U)
    store(P_H, acc[:, :g] * jax.nn.sigmoid(acc[:, g:]))
    acc = chunk(COL_A_Z)
    store(P_ZA, _silu(acc[:, :g]))
    b_gate = acc[:, g:]
    acc = chunk(COL_B_C)
    store(P_G, acc[:, :g] * acc[:, g:])
    acc = chunk(COL_B_Z)
    store(P_BZ, b_gate * _silu(acc[:, :g]))
    for s0 in range(0, g, LANES):
        t = rope(acc[:, g + s0:g + s0 + LANES]) * Q_SCALE
        store(P_CQ + s0, t)
        store(P_CQS + s0, pltpu.roll(t, HEAD_DIM, 1))
    acc = chunk(COL_C_K)
    store(P_CK, rope(acc[:, :LANES]))
    store(P_CV, acc[:, LANES:g])
    store(P_ZC, _silu(acc[:, g:]))
    acc = chunk(COL_D_Q)
    store(P_DQ, acc[:, :g] * Q_SCALE)
    store(P_DK, acc[:, g:])
    acc = chunk(COL_D_V)
    store(P_DV, acc[:, :g])
    store(P_ZD, _silu(acc[:, g:]))


def _in_proj(x, g, w, cos, sin, layer):
    b, s, d = x.shape
    tm = TM_PROJ
    return pl.pallas_call(
        functools.partial(_in_proj_kernel, layer=layer),
        grid=(b, s // tm),
        in_specs=[
            pl.BlockSpec((1, tm, d), lambda bi, i: (bi, i, 0)),
            pl.BlockSpec(g.shape, lambda bi, i: (0, 0)),
            pl.BlockSpec((1, d, D_IN), lambda bi, i: (layer, 0, 0), pipeline_mode=pl.Buffered(1)),
            pl.BlockSpec((tm, LANES), lambda bi, i: (i, 0)),
            pl.BlockSpec((tm, LANES), lambda bi, i: (i, 0)),
        ],
        out_specs=pl.BlockSpec((1, tm, D_PROJ), lambda bi, i: (bi, i, 0)),
        out_shape=jax.ShapeDtypeStruct((b, s, D_PROJ), jnp.bfloat16),
        scratch_shapes=[pltpu.VMEM((d, D_IN), jnp.bfloat16)],
        compiler_params=pltpu.CompilerParams(
            dimension_semantics=("arbitrary", "arbitrary"), vmem_limit_bytes=VMEM_LIMIT),
        name="in_proj",
    )(x, g, w, cos, sin)


def _conv_out_kernel(cur_ref, prev_ref, next_ref, aw_ref, ab_ref, lg_ref, lb_ref, bw_ref,
                     ycd_ref, x_ref, w_ref, g_ref, o_ref, hbuf, gbuf, hph, gph,
                     *, layer, final_norm):
    i = pl.program_id(1)
    n = pl.num_programs(1)
    t = T_MIX
    f32 = jnp.float32
    has_prev = (i > 0).astype(f32)
    has_next = (i < n - 1).astype(f32)
    row = lambda ref, j: ref[layer, j:j + 1, :]
    vec = lambda ref: ref[layer:layer + 1, :]
    col = lambda ref, rows, c: ref[0, rows, c:c + D_GROUP].astype(f32)

    full = slice(None)
    hwin = jnp.concatenate([col(prev_ref, full, P_H) * has_prev, col(cur_ref, full, P_H),
                            col(next_ref, full, P_H) * has_next], axis=0)
    gwin = jnp.concatenate([col(prev_ref, full, P_G) * has_prev, col(cur_ref, full, P_G),
                            col(next_ref, full, P_G) * has_next], axis=0)
    hbuf[...] = hwin
    gbuf[...] = gwin

    nwin = t + 2 * HALO
    nph = nwin - SUBLANES
    for p in range(1, SUBLANES):
        hph[p - 1] = pltpu.roll(hwin, nwin - p, 0)[0:nph]
    gph[0] = pltpu.roll(gwin, nwin - 1, 0)[0:nph]
    gph[1] = pltpu.roll(gwin, nwin - (SUBLANES - 1), 0)[0:nph]

    def tap(off, rows):
        p = off % SUBLANES
        if p == 0:
            return hbuf[off:off + rows, :]
        return hph[p - 1, off - p:off - p + rows, :]

    rc = CONV_ROWS
    for r0 in range(0, t, rc):
        rows = slice(r0, r0 + rc)
        acc = jnp.broadcast_to(vec(ab_ref), (rc, D_GROUP))
        for j in range(CONV_A_WIDTH):
            acc = acc + aw_ref[layer, j:j + 1, :] * tap(HALO - CONV_A_PAD + j + r0, rc)
        mu = jnp.mean(acc, axis=-1, keepdims=True)
        xc = acc - mu
        var = jnp.mean(xc * xc, axis=-1, keepdims=True)
        hn = xc * lax.rsqrt(var + EPS) * vec(lg_ref) + vec(lb_ref)
        ya = _silu(hn) * col(cur_ref, rows, P_ZA)
        gm = HALO + r0 - SUBLANES
        conv = (row(bw_ref, 0) * gph[1, gm:gm + rc, :]
                + row(bw_ref, 1) * gbuf[HALO + r0:HALO + r0 + rc, :]
                + row(bw_ref, 2) * gph[0, HALO + r0:HALO + r0 + rc, :])
        yb = col(cur_ref, rows, P_BZ) * conv
        yab = jnp.concatenate([ya.astype(jnp.bfloat16), yb.astype(jnp.bfloat16)], axis=1)
        xo = x_ref[0, rows, :] + jnp.dot(ycd_ref[0, rows, :], w_ref[0, 2 * D_GROUP:4 * D_GROUP, :],
                                         preferred_element_type=f32)
        xo = xo + jnp.dot(yab, w_ref[0, 0:2 * D_GROUP, :], preferred_element_type=f32)
        if final_norm:
            ms = jnp.mean(xo * xo, axis=-1, keepdims=True)
            xo = xo * lax.rsqrt(ms + EPS) * g_ref[...]
        o_ref[0, rows, :] = xo


def _conv_out(proj, ycd, x, aw, ab, lg, lb, bw, w, g, layer, final_norm):
    b, s, d = x.shape
    t = T_MIX
    hb = t // HALO
    nh = s // HALO
    f32 = jnp.float32
    whole = lambda a: pl.BlockSpec(a.shape, lambda bi, i: (0,) * a.ndim)
    return pl.pallas_call(
        functools.partial(_conv_out_kernel, layer=layer, final_norm=final_norm),
        grid=(b, s // t),
        in_specs=[
            pl.BlockSpec((1, t, W_AB), lambda bi, i: (bi, i, 0)),
            pl.BlockSpec((1, HALO, W_AB), lambda bi, i: (bi, jnp.maximum(i * hb - 1, 0), 0)),
            pl.BlockSpec((1, HALO, W_AB), lambda bi, i: (bi, jnp.minimum((i + 1) * hb, nh - 1), 0)),
            whole(aw), whole(ab), whole(lg), whole(lb), whole(bw),
            pl.BlockSpec((1, t, 2 * D_GROUP), lambda bi, i: (bi, i, 0)),
            pl.BlockSpec((1, t, d), lambda bi, i: (bi, i, 0)),
            pl.BlockSpec((1,) + w.shape[1:], lambda bi, i: (layer, 0, 0)),
            pl.BlockSpec((1, d), lambda bi, i: (0, 0)),
        ],
        out_specs=pl.BlockSpec((1, t, d), lambda bi, i: (bi, i, 0)),
        out_shape=jax.ShapeDtypeStruct((b, s, d), f32),
        scratch_shapes=[pltpu.VMEM((t + 2 * HALO, D_GROUP), f32),
                        pltpu.VMEM((t + 2 * HALO, D_GROUP), f32),
                        pltpu.VMEM((SUBLANES - 1, t + 2 * HALO - SUBLANES, D_GROUP), f32),
                        pltpu.VMEM((2, t + 2 * HALO - SUBLANES, D_GROUP), f32)],
        compiler_params=_params(),
        name="conv_out",
    )(proj, proj, proj, aw, ab, lg, lb, bw, ycd, x, w, g)


def _swa_tile(sink_ref, q_ref, qs_ref, z_ref, kv_ref, y_ref, i, layer):
    s_len = kv_ref.shape[1]
    nkeys = 3 * SWA_BLOCK
    f32 = jnp.float32
    lane = lax.broadcasted_iota(jnp.int32, (1, LANES), 1)
    lo = lane < HEAD_DIM
    zero = jnp.zeros((), jnp.bfloat16)
    ones = jnp.ones((nkeys, LANES), jnp.bfloat16)
    for nb in range(T_ATTN // SWA_BLOCK):
        q0 = i * T_ATTN + nb * SWA_BLOCK
        ws = pl.multiple_of(jnp.clip(q0 - SWA_BLOCK, 0, s_len - nkeys), SWA_BLOCK)
        rows = slice(nb * SWA_BLOCK, (nb + 1) * SWA_BLOCK)
        win = pl.ds(ws, nkeys)
        lhs = [
            jnp.where(lo, q_ref[0, rows, 0:LANES], zero),
            jnp.where(lo, qs_ref[0, rows, 0:LANES], zero),
            jnp.where(lo, zero, qs_ref[0, rows, LANES:2 * LANES]),
            jnp.where(lo, zero, q_ref[0, rows, LANES:2 * LANES]),
        ]
        kw = kv_ref[0, win, 0:LANES]
        vw = jnp.concatenate([kv_ref[0, win, LANES:2 * LANES], ones], axis=1)
        qpos = q0 + lax.broadcasted_iota(jnp.int32, (SWA_BLOCK, nkeys), 0)
        kpos = ws + lax.broadcasted_iota(jnp.int32, (SWA_BLOCK, nkeys), 1)
        valid = jnp.abs(kpos - qpos) <= SWA_WINDOW
        num, den = [], []
        for h in range(SWA_HEADS):
            sc = jnp.einsum("qd,kd->qk", lhs[h], kw, preferred_element_type=f32)
            sh = jnp.where(valid, sc, NEG_INF)
            sink = sink_ref[layer, h]
            m = jnp.maximum(jnp.max(sh, axis=-1, keepdims=True), sink)
            e = jnp.exp2(sh - m)
            pv = jnp.dot(e.astype(jnp.bfloat16), vw, preferred_element_type=f32)
            num.append(pv[:, :LANES])
            den.append(pv[:, LANES:] + jnp.exp2(sink - m))
        og0 = jnp.where(lo, num[0], pltpu.roll(num[1], HEAD_DIM, 1)) / jnp.where(lo, den[0], den[1])
        og1 = jnp.where(lo, pltpu.roll(num[2], HEAD_DIM, 1), num[3]) / jnp.where(lo, den[2], den[3])
        y = jnp.concatenate([og0, og1], axis=-1) * z_ref[0, rows, :].astype(f32)
        y_ref[rows, 0:D_GROUP] = y.astype(y_ref.dtype)


def _nbr_tile(q_ref, z_ref, k_ref, v_ref, bias_ref, y_ref, i):
    rows_total = k_ref.shape[1] // GRID_W
    rows_tile = T_ATTN // GRID_W
    nkeys = NA_KH * GRID_W
    f32 = jnp.float32
    lane = lax.broadcasted_iota(jnp.int32, (1, LANES), 1)
    lo = lane < HEAD_DIM
    zero = jnp.zeros((), jnp.bfloat16)
    ones = jnp.ones((nkeys, LANES), jnp.bfloat16)

    for rr in range(rows_tile):
        r = i * rows_tile + rr
        r0 = jnp.clip(r - NA_KH // 2, 0, rows_total - NA_KH)
        d0 = r0 - r + (NA_KH - 1)
        ks = pl.multiple_of(r0 * GRID_W, GRID_W)
        qrow = slice(rr * GRID_W, (rr + 1) * GRID_W)
        outs = []
        for g in range(2):
            cols = slice(g * LANES, (g + 1) * LANES)
            qg = q_ref[0, qrow, cols]
            lhs = jnp.concatenate([jnp.where(lo, qg, zero), jnp.where(lo, zero, qg)], axis=0)
            kw = k_ref[0, pl.ds(ks, nkeys), cols]
            vw = jnp.concatenate([v_ref[0, pl.ds(ks, nkeys), cols], ones], axis=1)
            sc = jnp.einsum("qd,kd->qk", lhs, kw, preferred_element_type=f32)
            bias = jnp.concatenate(
                [jnp.concatenate([bias_ref[0, 2 * g + hh, d0 + 2 * m]
                                  for m in range(NA_KH // 2)], axis=-1)
                 for hh in range(2)], axis=0)
            sc = sc + bias
            m_ = jnp.max(sc, axis=-1, keepdims=True)
            e = jnp.exp2(sc - m_)
            pv = jnp.dot(e.astype(jnp.bfloat16), vw, preferred_element_type=f32)
            pv = jnp.where(jnp.concatenate([lo, lo], axis=1), pv[0:GRID_W], pv[GRID_W:2 * GRID_W])
            outs.append(pv[:, :LANES] / pv[:, LANES:])
        y = jnp.concatenate(outs, axis=-1) * z_ref[0, qrow, :].astype(f32)
        y_ref[qrow, D_GROUP:2 * D_GROUP] = y.astype(y_ref.dtype)


def _attn_kernel(sink_ref, cq_ref, cqs_ref, cz_ref, ckv_ref, dq_ref, dz_ref, dk_ref, dv_ref,
                 bias_ref, o_ref, *, layer):
    i = pl.program_id(1)
    y_ref = o_ref.at[0]
    _swa_tile(sink_ref, cq_ref, cqs_ref, cz_ref, ckv_ref, y_ref, i, layer)
    _nbr_tile(dq_ref, dz_ref, dk_ref, dv_ref, bias_ref, y_ref, i)


def _attn(proj, sink, bias, layer):
    b, s, _ = proj.shape
    t = T_ATTN
    tile = lambda col: pl.BlockSpec((1, t, D_GROUP), lambda bi, i, sk: (bi, i, col // D_GROUP))
    seq = lambda col: pl.BlockSpec((1, s, D_GROUP), lambda bi, i, sk: (bi, 0, col // D_GROUP))
    return pl.pallas_call(
        functools.partial(_attn_kernel, layer=layer),
        grid_spec=pltpu.PrefetchScalarGridSpec(
            num_scalar_prefetch=1,
            grid=(b, s // t),
            in_specs=[
                tile(P_CQ), tile(P_CQS), tile(P_ZC), seq(P_CK),
                tile(P_DQ), tile(P_ZD), seq(P_DK), seq(P_DV),
                pl.BlockSpec((1,) + bias.shape[1:], lambda bi, i, sk: (layer, 0, 0, 0, 0)),
            ],
            out_specs=pl.BlockSpec((1, t, 2 * D_GROUP), lambda bi, i, sk: (bi, i, 0)),
        ),
        out_shape=jax.ShapeDtypeStruct((b, s, 2 * D_GROUP), jnp.bfloat16),
        compiler_params=_params(),
        name="attn",
    )(sink, proj, proj, proj, proj, proj, proj, proj, proj, bias)


def _rope_tables(s):
    inv_freq = ROPE_THETA ** (-jnp.arange(0, HEAD_DIM, 2, dtype=jnp.float32) / HEAD_DIM)
    inv_freq = jnp.tile(inv_freq, LANES // HALF)
    hi = jnp.arange(0, s, GRID_W, dtype=jnp.float32)[:, None, None] * inv_freq
    lo = jnp.arange(GRID_W, dtype=jnp.float32)[None, :, None] * inv_freq
    ch, sh, cl, sl = jnp.cos(hi), jnp.sin(hi), jnp.cos(lo), jnp.sin(lo)
    cos = ch * cl - sh * sl
    sin = sh * cl + ch * sl
    return cos.reshape(s, LANES), sin.reshape(s, LANES)


def _nbr_bias_table(rpb):
    nrel = 2 * NA_KW - 1
    ndr = 2 * NA_KH - 1
    c = np.arange(GRID_W)
    c0 = np.clip(c - NA_KW // 2, 0, GRID_W - NA_KW)
    col_ok = (c[None, :] >= c0[:, None]) & (c[None, :] < c0[:, None] + NA_KW)
    dc = np.clip(c[None, :] - c[:, None], -(NA_KW - 1), NA_KW - 1) + (NA_KW - 1)
    onehot = (dc[None] == np.arange(nrel)[:, None, None]).astype(np.float32)
    pick = np.zeros((2, nrel, GRID_W, 2, GRID_W), np.float32)
    pick[0, :, :, 0, :] = onehot
    pick[1, :, :, 1, :] = onehot
    pick = pick.reshape(2 * nrel, GRID_W, LANES)
    dr = np.arange(ndr + 1)[:, None] + np.arange(2)[None, :]
    valid = (dr < ndr)[:, None, :, None] & col_ok[None, :, None, :]
    valid = valid.reshape(ndr + 1, GRID_W, LANES)
    ext = jnp.pad(rpb, ((0, 0), (0, 0), (0, 2), (0, 0)))
    pair = jnp.concatenate([ext[:, :, :-1], ext[:, :, 1:]], axis=-1)
    full = jnp.einsum("lhdc,cqj->lhdqj", pair, jnp.asarray(pick),
                      precision=lax.Precision.HIGHEST)
    return jnp.where(jnp.asarray(valid), full * LOG2E, NEG_INF)


def kernel(x, norm_g, w_in, w_out, conv_a_w, conv_a_b, ln_a_g, ln_a_b, conv_b_w, swa_sink,
           na_rpb, final_norm_g):
    depth = norm_g.shape[0]
    s = x.shape[1]
    assert s % max(TM_PROJ, T_MIX, T_ATTN) == 0 and s % GRID_W == 0, x.shape
    assert s // GRID_W >= NA_KH and s >= 3 * SWA_BLOCK, x.shape
    assert w_in.shape[1:] == (x.shape[2], D_IN) and w_out.shape[1:] == (4 * D_GROUP, x.shape[2])
    cos_t, sin_t = _rope_tables(s)
    w_out_b = w_out.astype(jnp.bfloat16)
    bias = _nbr_bias_table(na_rpb)
    sink = swa_sink * LOG2E
    fg = final_norm_g.reshape(1, -1)
    for l in range(depth):
        proj = _in_proj(x, norm_g, w_in, cos_t, sin_t, l)
        ycd = _attn(proj, sink, bias, l)
        x = _conv_out(proj, ycd, x, conv_a_w, conv_a_b, ln_a_g, ln_a_b, conv_b_w,
                      w_out_b, fg, l, final_norm=(l == depth - 1))
    return x
```

```python
import functools
import math

import numpy as np
import jax
import jax.numpy as jnp
from jax import lax
from jax.experimental import pallas as pl
from jax.experimental.pallas import tpu as pltpu

D_GROUP = 256
HEAD_DIM = 64
HALF = HEAD_DIM // 2
GRID_W = 64
CONV_A_WIDTH = 31
CONV_A_PAD = (CONV_A_WIDTH - 1) // 2
SWA_WINDOW = 128
SWA_BLOCK = 128
SWA_HEADS = 4
NA_KH = 8
NA_KW = 16
ROPE_THETA = 10000.0
EPS = 1e-6
NEG_INF = -1e30
LOG2E = math.log2(math.e)
Q_SCALE = HEAD_DIM ** -0.5 * LOG2E

COL_A_U, COL_A_V, COL_A_Z = 0, 256, 512
COL_B_B, COL_B_C, COL_B_X, COL_B_Z = 768, 1024, 1280, 1536
COL_C_Q, COL_C_K, COL_C_V, COL_C_Z = 1792, 2048, 2176, 2304
COL_D_Q, COL_D_K, COL_D_V, COL_D_Z = 2560, 2816, 3072, 3328
D_IN = 3584

P_H, P_ZA = 0, 256
P_BZ, P_G = 512, 768
W_AB = 1024
P_CQ, P_CQS, P_CK, P_CV, P_ZC = 0, 256, 512, 640, 768
P_DQ, P_DK, P_DV, P_ZD = 1024, 1280, 1536, 1792
W_CD = 2048

LANES = 128
SUBLANES = 8
HALO = 16
VMEM_LIMIT = 56 * 1024 * 1024

TM_PROJ = 1024
T_MIX = 1024
T_ATTN = 1024
PROJ_CHUNK = 512
CONV_ROWS = 256


def _silu(x):
    return x * jax.nn.sigmoid(x)


def _params():
    return pltpu.CompilerParams(
        dimension_semantics=("parallel", "parallel"), vmem_limit_bytes=VMEM_LIMIT)


def _in_proj_kernel(x_ref, g_ref, w32_ref, cos_ref, sin_ref, oab_ref, ocd_ref, w_ref, *, layer):
    @pl.when((pl.program_id(0) == 0) & (pl.program_id(1) == 0))
    def _():
        for c0 in range(0, D_IN, PROJ_CHUNK):
            w_ref[:, c0:c0 + PROJ_CHUNK] = w32_ref[0, :, c0:c0 + PROJ_CHUNK].astype(w_ref.dtype)

    x = x_ref[0]
    ms = jnp.mean(x * x, axis=-1, keepdims=True)
    h = (x * lax.rsqrt(ms + EPS) * g_ref[layer:layer + 1, :]).astype(jnp.bfloat16)
    lane = lax.broadcasted_iota(jnp.int32, (1, LANES), 1)
    first_half = (lane % HEAD_DIM) < HALF

    def rope(t):
        partner = jnp.where(first_half, -pltpu.roll(t, LANES - HALF, 1), pltpu.roll(t, HALF, 1))
        return t * cos_ref[...] + partner * sin_ref[...]

    def store(ref, col, t):
        ref[0, :, col:col + t.shape[1]] = t.astype(ref.dtype)

    def chunk(c0):
        return jnp.dot(h, w_ref[:, c0:c0 + PROJ_CHUNK], preferred_element_type=jnp.float32)

    g = D_GROUP
    acc = chunk(COL_A_U)
    store(oab_ref, P_H, acc[:, :g] * jax.nn.sigmoid(acc[:, g:]))
    acc = chunk(COL_A_Z)
    store(oab_ref, P_ZA, _silu(acc[:, :g]))
    b_gate = acc[:, g:]
    acc = chunk(COL_B_C)
    store(oab_ref, P_G, acc[:, :g] * acc[:, g:])
    acc = chunk(COL_B_Z)
    store(oab_ref, P_BZ, b_gate * _silu(acc[:, :g]))
    for s0 in range(0, g, LANES):
        t = rope(acc[:, g + s0:g + s0 + LANES]) * Q_SCALE
        store(ocd_ref, P_CQ + s0, t)
        store(ocd_ref, P_CQS + s0, pltpu.roll(t, HEAD_DIM, 1))
    acc = chunk(COL_C_K)
    store(ocd_ref, P_CK, rope(acc[:, :LANES]))
    store(ocd_ref, P_CV, acc[:, LANES:g])
    store(ocd_ref, P_ZC, _silu(acc[:, g:]))
    acc = chunk(COL_D_Q)
    store(ocd_ref, P_DQ, acc[:, :g] * Q_SCALE)
    store(ocd_ref, P_DK, acc[:, g:])
    acc = chunk(COL_D_V)
    store(ocd_ref, P_DV, acc[:, :g])
    store(ocd_ref, P_ZD, _silu(acc[:, g:]))


def _in_proj(x, g, w, cos, sin, layer):
    b, s, d = x.shape
    tm = TM_PROJ
    return pl.pallas_call(
        functools.partial(_in_proj_kernel, layer=layer),
        grid=(b, s // tm),
        in_specs=[
            pl.BlockSpec((1, tm, d), lambda bi, i: (bi, i, 0)),
            pl.BlockSpec(g.shape, lambda bi, i: (0, 0)),
            pl.BlockSpec((1, d, D_IN), lambda bi, i: (layer, 0, 0), pipeline_mode=pl.Buffered(1)),
            pl.BlockSpec((tm, LANES), lambda bi, i: (i, 0)),
            pl.BlockSpec((tm, LANES), lambda bi, i: (i, 0)),
        ],
        out_specs=[pl.BlockSpec((1, tm, W_AB), lambda bi, i: (bi, i, 0)),
                   pl.BlockSpec((1, tm, W_CD), lambda bi, i: (bi, i, 0))],
        out_shape=[jax.ShapeDtypeStruct((b, s, W_AB), jnp.bfloat16),
                   jax.ShapeDtypeStruct((b, s, W_CD), jnp.bfloat16)],
        scratch_shapes=[pltpu.VMEM((d, D_IN), jnp.bfloat16)],
        compiler_params=pltpu.CompilerParams(
            dimension_semantics=("arbitrary", "arbitrary"), vmem_limit_bytes=VMEM_LIMIT),
        name="in_proj",
    )(x, g, w, cos, sin)


def _conv_out_kernel(cur_ref, prev_ref, next_ref, aw_ref, ab_ref, lg_ref, lb_ref, bw_ref,
                     ycd_ref, x_ref, w_ref, g_ref, o_ref, hbuf, gbuf, hph, gph,
                     *, layer, final_norm):
    i = pl.program_id(1)
    n = pl.num_programs(1)
    t = T_MIX
    f32 = jnp.float32
    has_prev = (i > 0).astype(f32)
    has_next = (i < n - 1).astype(f32)
    row = lambda ref, j: ref[layer, j:j + 1, :]
    vec = lambda ref: ref[layer:layer + 1, :]
    col = lambda ref, rows, c: ref[0, rows, c:c + D_GROUP].astype(f32)

    full = slice(None)
    hwin = jnp.concatenate([col(prev_ref, full, P_H) * has_prev, col(cur_ref, full, P_H),
                            col(next_ref, full, P_H) * has_next], axis=0)
    gwin = jnp.concatenate([col(prev_ref, full, P_G) * has_prev, col(cur_ref, full, P_G),
                            col(next_ref, full, P_G) * has_next], axis=0)
    hbuf[...] = hwin
    gbuf[...] = gwin

    nwin = t + 2 * HALO
    nph = nwin - SUBLANES
    for p in range(1, SUBLANES):
        hph[p - 1] = pltpu.roll(hwin, nwin - p, 0)[0:nph]
    gph[0] = pltpu.roll(gwin, nwin - 1, 0)[0:nph]
    gph[1] = pltpu.roll(gwin, nwin - (SUBLANES - 1), 0)[0:nph]

    def tap(off, rows):
        p = off % SUBLANES
        if p == 0:
            return hbuf[off:off + rows, :]
        return hph[p - 1, off - p:off - p + rows, :]

    rc = CONV_ROWS
    for r0 in range(0, t, rc):
        rows = slice(r0, r0 + rc)
        acc = jnp.broadcast_to(vec(ab_ref), (rc, D_GROUP))
        for j in range(CONV_A_WIDTH):
            acc = acc + aw_ref[layer, j:j + 1, :] * tap(HALO - CONV_A_PAD + j + r0, rc)
        mu = jnp.mean(acc, axis=-1, keepdims=True)
        xc = acc - mu
        var = jnp.mean(xc * xc, axis=-1, keepdims=True)
        hn = xc * lax.rsqrt(var + EPS) * vec(lg_ref) + vec(lb_ref)
        ya = _silu(hn) * col(cur_ref, rows, P_ZA)
        gm = HALO + r0 - SUBLANES
        conv = (row(bw_ref, 0) * gph[1, gm:gm + rc, :]
                + row(bw_ref, 1) * gbuf[HALO + r0:HALO + r0 + rc, :]
                + row(bw_ref, 2) * gph[0, HALO + r0:HALO + r0 + rc, :])
        yb = col(cur_ref, rows, P_BZ) * conv
        yab = jnp.concatenate([ya.astype(jnp.bfloat16), yb.astype(jnp.bfloat16)], axis=1)
        xo = x_ref[0, rows, :] + jnp.dot(ycd_ref[0, rows, :], w_ref[0, 2 * D_GROUP:4 * D_GROUP, :],
                                         preferred_element_type=f32)
        xo = xo + jnp.dot(yab, w_ref[0, 0:2 * D_GROUP, :], preferred_element_type=f32)
        if final_norm:
            ms = jnp.mean(xo * xo, axis=-1, keepdims=True)
            xo = xo * lax.rsqrt(ms + EPS) * g_ref[...]
        o_ref[0, rows, :] = xo


def _conv_out(proj, ycd, x, aw, ab, lg, lb, bw, w, g, layer, final_norm):
    b, s, d = x.shape
    t = T_MIX
    hb = t // HALO
    nh = s // HALO
    f32 = jnp.float32
    whole = lambda a: pl.BlockSpec(a.shape, lambda bi, i: (0,) * a.ndim)
    return pl.pallas_call(
        functools.partial(_conv_out_kernel, layer=layer, final_norm=final_norm),
        grid=(b, s // t),
        in_specs=[
            pl.BlockSpec((1, t, W_AB), lambda bi, i: (bi, i, 0)),
            pl.BlockSpec((1, HALO, W_AB), lambda bi, i: (bi, jnp.maximum(i * hb - 1, 0), 0)),
            pl.BlockSpec((1, HALO, W_AB), lambda bi, i: (bi, jnp.minimum((i + 1) * hb, nh - 1), 0)),
            whole(aw), whole(ab), whole(lg), whole(lb), whole(bw),
            pl.BlockSpec((1, t, 2 * D_GROUP), lambda bi, i: (bi, i, 0)),
            pl.BlockSpec((1, t, d), lambda bi, i: (bi, i, 0)),
            pl.BlockSpec((1,) + w.shape[1:], lambda bi, i: (layer, 0, 0)),
            pl.BlockSpec((1, d), lambda bi, i: (0, 0)),
        ],
        out_specs=pl.BlockSpec((1, t, d), lambda bi, i: (bi, i, 0)),
        out_shape=jax.ShapeDtypeStruct((b, s, d), f32),
        scratch_shapes=[pltpu.VMEM((t + 2 * HALO, D_GROUP), f32),
                        pltpu.VMEM((t + 2 * HALO, D_GROUP), f32),
                        pltpu.VMEM((SUBLANES - 1, t + 2 * HALO - SUBLANES, D_GROUP), f32),
                        pltpu.VMEM((2, t + 2 * HALO - SUBLANES, D_GROUP), f32)],
        compiler_params=_params(),
        name="conv_out",
    )(proj, proj, proj, aw, ab, lg, lb, bw, ycd, x, w, g)


def _swa_tile(sink_ref, q_ref, qs_ref, z_ref, kv_ref, y_ref, i, layer):
    s_len = kv_ref.shape[1]
    nkeys = 3 * SWA_BLOCK
    f32 = jnp.float32
    lane = lax.broadcasted_iota(jnp.int32, (1, LANES), 1)
    lo = lane < HEAD_DIM
    zero = jnp.zeros((), jnp.bfloat16)
    ones = jnp.ones((nkeys, LANES), jnp.bfloat16)
    for nb in range(T_ATTN // SWA_BLOCK):
        q0 = i * T_ATTN + nb * SWA_BLOCK
        ws = pl.multiple_of(jnp.clip(q0 - SWA_BLOCK, 0, s_len - nkeys), SWA_BLOCK)
        rows = slice(nb * SWA_BLOCK, (nb + 1) * SWA_BLOCK)
        win = pl.ds(ws, nkeys)
        lhs = [
            jnp.where(lo, q_ref[0, rows, 0:LANES], zero),
            jnp.where(lo, qs_ref[0, rows, 0:LANES], zero),
            jnp.where(lo, zero, qs_ref[0, rows, LANES:2 * LANES]),
            jnp.where(lo, zero, q_ref[0, rows, LANES:2 * LANES]),
        ]
        kw = kv_ref[0, win, 0:LANES]
        vw = jnp.concatenate([kv_ref[0, win, LANES:2 * LANES], ones], axis=1)
        qpos = q0 + lax.broadcasted_iota(jnp.int32, (SWA_BLOCK, nkeys), 0)
        kpos = ws + lax.broadcasted_iota(jnp.int32, (SWA_BLOCK, nkeys), 1)
        valid = jnp.abs(kpos - qpos) <= SWA_WINDOW
        num, den = [], []
        for h in range(SWA_HEADS):
            sc = jnp.einsum("qd,kd->qk", lhs[h], kw, preferred_element_type=f32)
            sh = jnp.where(valid, sc, NEG_INF)
            sink = sink_ref[layer, h]
            m = jnp.maximum(jnp.max(sh, axis=-1, keepdims=True), sink)
            e = jnp.exp2(sh - m)
            pv = jnp.dot(e.astype(jnp.bfloat16), vw, preferred_element_type=f32)
            num.append(pv[:, :LANES])
            den.append(pv[:, LANES:] + jnp.exp2(sink - m))
        og0 = jnp.where(lo, num[0], pltpu.roll(num[1], HEAD_DIM, 1)) / jnp.where(lo, den[0], den[1])
        og1 = jnp.where(lo, pltpu.roll(num[2], HEAD_DIM, 1), num[3]) / jnp.where(lo, den[2], den[3])
        y = jnp.concatenate([og0, og1], axis=-1) * z_ref[0, rows, :].astype(f32)
        y_ref[rows, 0:D_GROUP] = y.astype(y_ref.dtype)


def _nbr_tile(q_ref, z_ref, k_ref, v_ref, bias_ref, y_ref, i):
    rows_total = k_ref.shape[1] // GRID_W
    rows_tile = T_ATTN // GRID_W
    nkeys = NA_KH * GRID_W
    f32 = jnp.float32
    lane = lax.broadcasted_iota(jnp.int32, (1, LANES), 1)
    lo = lane < HEAD_DIM
    zero = jnp.zeros((), jnp.bfloat16)
    ones = jnp.ones((nkeys, LANES), jnp.bfloat16)

    for rr in range(rows_tile):
        r = i * rows_tile + rr
        r0 = jnp.clip(r - NA_KH // 2, 0, rows_total - NA_KH)
        d0 = r0 - r + (NA_KH - 1)
        ks = pl.multiple_of(r0 * GRID_W, GRID_W)
        qrow = slice(rr * GRID_W, (rr + 1) * GRID_W)
        outs = []
        for g in range(2):
            cols = slice(g * LANES, (g + 1) * LANES)
            qg = q_ref[0, qrow, cols]
            lhs = jnp.concatenate([jnp.where(lo, qg, zero), jnp.where(lo, zero, qg)], axis=0)
            kw = k_ref[0, pl.ds(ks, nkeys), cols]
            vw = jnp.concatenate([v_ref[0, pl.ds(ks, nkeys), cols], ones], axis=1)
            sc = jnp.einsum("qd,kd->qk", lhs, kw, preferred_element_type=f32)
            bias = jnp.concatenate(
                [jnp.concatenate([bias_ref[0, 2 * g + hh, d0 + 2 * m]
                                  for m in range(NA_KH // 2)], axis=-1)
                 for hh in range(2)], axis=0)
            sc = sc + bias
            m_ = jnp.max(sc, axis=-1, keepdims=True)
            e = jnp.exp2(sc - m_)
            pv = jnp.dot(e.astype(jnp.bfloat16), vw, preferred_element_type=f32)
            pv = jnp.where(jnp.concatenate([lo, lo], axis=1), pv[0:GRID_W], pv[GRID_W:2 * GRID_W])
            outs.append(pv[:, :LANES] / pv[:, LANES:])
        y = jnp.concatenate(outs, axis=-1) * z_ref[0, qrow, :].astype(f32)
        y_ref[qrow, D_GROUP:2 * D_GROUP] = y.astype(y_ref.dtype)


def _attn_kernel(sink_ref, cq_ref, cqs_ref, cz_ref, ckv_ref, dq_ref, dz_ref, dk_ref, dv_ref,
                 bias_ref, o_ref, *, layer):
    i = pl.program_id(1)
    y_ref = o_ref.at[0]
    _swa_tile(sink_ref, cq_ref, cqs_ref, cz_ref, ckv_ref, y_ref, i, layer)
    _nbr_tile(dq_ref, dz_ref, dk_ref, dv_ref, bias_ref, y_ref, i)


def _attn(proj, sink, bias, layer):
    b, s, _ = proj.shape
    t = T_ATTN
    tile = lambda col: pl.BlockSpec((1, t, D_GROUP), lambda bi, i, sk: (bi, i, col // D_GROUP))
    seq = lambda col: pl.BlockSpec((1, s, D_GROUP), lambda bi, i, sk: (bi, 0, col // D_GROUP))
    return pl.pallas_call(
        functools.partial(_attn_kernel, layer=layer),
        grid_spec=pltpu.PrefetchScalarGridSpec(
            num_scalar_prefetch=1,
            grid=(b, s // t),
            in_specs=[
                tile(P_CQ), tile(P_CQS), tile(P_ZC), seq(P_CK),
                tile(P_DQ), tile(P_ZD), seq(P_DK), seq(P_DV),
                pl.BlockSpec((1,) + bias.shape[1:], lambda bi, i, sk: (layer, 0, 0, 0, 0)),
            ],
            out_specs=pl.BlockSpec((1, t, 2 * D_GROUP), lambda bi, i, sk: (bi, i, 0)),
        ),
        out_shape=jax.ShapeDtypeStruct((b, s, 2 * D_GROUP), jnp.bfloat16),
        compiler_params=_params(),
        name="attn",
    )(sink, proj, proj, proj, proj, proj, proj, proj, proj, bias)


def _rope_tables(s):
    inv_freq = ROPE_THETA ** (-jnp.arange(0, HEAD_DIM, 2, dtype=jnp.float32) / HEAD_DIM)
    inv_freq = jnp.tile(inv_freq, LANES // HALF)
    hi = jnp.arange(0, s, GRID_W, dtype=jnp.float32)[:, None, None] * inv_freq
    lo = jnp.arange(GRID_W, dtype=jnp.float32)[None, :, None] * inv_freq
    ch, sh, cl, sl = jnp.cos(hi), jnp.sin(hi), jnp.cos(lo), jnp.sin(lo)
    cos = ch * cl - sh * sl
    sin = sh * cl + ch * sl
    return cos.reshape(s, LANES), sin.reshape(s, LANES)


def _nbr_bias_table(rpb):
    nrel = 2 * NA_KW - 1
    ndr = 2 * NA_KH - 1
    c = np.arange(GRID_W)
    c0 = np.clip(c - NA_KW // 2, 0, GRID_W - NA_KW)
    col_ok = (c[None, :] >= c0[:, None]) & (c[None, :] < c0[:, None] + NA_KW)
    dc = np.clip(c[None, :] - c[:, None], -(NA_KW - 1), NA_KW - 1) + (NA_KW - 1)
    onehot = (dc[None] == np.arange(nrel)[:, None, None]).astype(np.float32)
    pick = np.zeros((2, nrel, GRID_W, 2, GRID_W), np.float32)
    pick[0, :, :, 0, :] = onehot
    pick[1, :, :, 1, :] = onehot
    pick = pick.reshape(2 * nrel, GRID_W, LANES)
    dr = np.arange(ndr + 1)[:, None] + np.arange(2)[None, :]
    valid = (dr < ndr)[:, None, :, None] & col_ok[None, :, None, :]
    valid = valid.reshape(ndr + 1, GRID_W, LANES)
    ext = jnp.pad(rpb, ((0, 0), (0, 0), (0, 2), (0, 0)))
    pair = jnp.concatenate([ext[:, :, :-1], ext[:, :, 1:]], axis=-1)
    full = jnp.einsum("lhdc,cqj->lhdqj", pair, jnp.asarray(pick),
                      precision=lax.Precision.HIGHEST)
    return jnp.where(jnp.asarray(valid), full * LOG2E, NEG_INF)


def kernel(x, norm_g, w_in, w_out, conv_a_w, conv_a_b, ln_a_g, ln_a_b, conv_b_w, swa_sink,
           na_rpb, final_norm_g):
    depth = norm_g.shape[0]
    s = x.shape[1]
    assert s % max(TM_PROJ, T_MIX, T_ATTN) == 0 and s % GRID_W == 0, x.shape
    assert s // GRID_W >= NA_KH and s >= 3 * SWA_BLOCK, x.shape
    assert w_in.shape[1:] == (x.shape[2], D_IN) and w_out.shape[1:] == (4 * D_GROUP, x.shape[2])
    cos_t, sin_t = _rope_tables(s)
    w_out_b = w_out.astype(jnp.bfloat16)
    bias = _nbr_bias_table(na_rpb)
    sink = swa_sink * LOG2E
    fg = final_norm_g.reshape(1, -1)
    for l in range(depth):
        proj_ab, proj_cd = _in_proj(x, norm_g, w_in, cos_t, sin_t, l)
        ycd = _attn(proj_cd, sink, bias, l)
        x = _conv_out(proj_ab, ycd, x, conv_a_w, conv_a_b, ln_a_g, ln_a_b, conv_b_w,
                      w_out_b, fg, l, final_norm=(l == depth - 1))
    return x
```

```python
import functools
import math

import numpy as np
import jax
import jax.numpy as jnp
from jax import lax
from jax.experimental import pallas as pl
from jax.experimental.pallas import tpu as pltpu

D_GROUP = 256
HEAD_DIM = 64
HALF = HEAD_DIM // 2
GRID_W = 64
CONV_A_WIDTH = 31
CONV_A_PAD = (CONV_A_WIDTH - 1) // 2
SWA_WINDOW = 128
SWA_BLOCK = 128
SWA_HEADS = 4
NA_KH = 8
NA_KW = 16
ROPE_THETA = 10000.0
EPS = 1e-6
NEG_INF = -1e30
LOG2E = math.log2(math.e)
Q_SCALE = HEAD_DIM ** -0.5 * LOG2E

COL_A_U, COL_A_V, COL_A_Z = 0, 256, 512
COL_B_B, COL_B_C, COL_B_X, COL_B_Z = 768, 1024, 1280, 1536
COL_C_Q, COL_C_K, COL_C_V, COL_C_Z = 1792, 2048, 2176, 2304
COL_D_Q, COL_D_K, COL_D_V, COL_D_Z = 2560, 2816, 3072, 3328
D_IN = 3584

P_H, P_ZA = 0, 256
P_BZ, P_G = 512, 768
W_AB = 1024
P_CQ, P_CQS, P_ZC = 0, 256, 512
W_CT = 768
P_CK, P_CV = 0, 128
W_CKV = 256
P_DQ, P_ZD = 0, 256
W_DT = 512
P_DK, P_DV = 0, 256
W_DKV = 512

LANES = 128
SUBLANES = 8
HALO = 16
VMEM_LIMIT = 56 * 1024 * 1024

TM_PROJ = 1024
T_MIX = 1024
T_ATTN = 1024
PROJ_CHUNK = 512
CONV_ROWS = 256


def _silu(x):
    return x * jax.nn.sigmoid(x)


def _params():
    return pltpu.CompilerParams(
        dimension_semantics=("parallel", "parallel"), vmem_limit_bytes=VMEM_LIMIT)


def _in_proj_kernel(x_ref, g_ref, w32_ref, cos_ref, sin_ref, oab_ref, oct_ref, ockv_ref, odt_ref,
                    odkv_ref, w_ref, *, layer):
    @pl.when((pl.program_id(0) == 0) & (pl.program_id(1) == 0))
    def _():
        for c0 in range(0, D_IN, PROJ_CHUNK):
            w_ref[:, c0:c0 + PROJ_CHUNK] = w32_ref[0, :, c0:c0 + PROJ_CHUNK].astype(w_ref.dtype)

    x = x_ref[0]
    ms = jnp.mean(x * x, axis=-1, keepdims=True)
    h = (x * lax.rsqrt(ms + EPS) * g_ref[layer:layer + 1, :]).astype(jnp.bfloat16)
    lane = lax.broadcasted_iota(jnp.int32, (1, LANES), 1)
    first_half = (lane % HEAD_DIM) < HALF

    def rope(t):
        partner = jnp.where(first_half, -pltpu.roll(t, LANES - HALF, 1), pltpu.roll(t, HALF, 1))
        return t * cos_ref[...] + partner * sin_ref[...]

    def store(ref, col, t):
        ref[0, :, col:col + t.shape[1]] = t.astype(ref.dtype)

    def chunk(c0):
        return jnp.dot(h, w_ref[:, c0:c0 + PROJ_CHUNK], preferred_element_type=jnp.float32)

    g = D_GROUP
    acc = chunk(COL_A_U)
    store(oab_ref, P_H, acc[:, :g] * jax.nn.sigmoid(acc[:, g:]))
    acc = chunk(COL_A_Z)
    store(oab_ref, P_ZA, _silu(acc[:, :g]))
    b_gate = acc[:, g:]
    acc = chunk(COL_B_C)
    store(oab_ref, P_G, acc[:, :g] * acc[:, g:])
    acc = chunk(COL_B_Z)
    store(oab_ref, P_BZ, b_gate * _silu(acc[:, :g]))
    for s0 in range(0, g, LANES):
        t = rope(acc[:, g + s0:g + s0 + LANES]) * Q_SCALE
        store(oct_ref, P_CQ + s0, t)
        store(oct_ref, P_CQS + s0, pltpu.roll(t, HEAD_DIM, 1))
    acc = chunk(COL_C_K)
    store(ockv_ref, P_CK, rope(acc[:, :LANES]))
    store(ockv_ref, P_CV, acc[:, LANES:g])
    store(oct_ref, P_ZC, _silu(acc[:, g:]))
    acc = chunk(COL_D_Q)
    store(odt_ref, P_DQ, acc[:, :g] * Q_SCALE)
    store(odkv_ref, P_DK, acc[:, g:])
    acc = chunk(COL_D_V)
    store(odkv_ref, P_DV, acc[:, :g])
    store(odt_ref, P_ZD, _silu(acc[:, g:]))


def _in_proj(x, g, w, cos, sin, layer):
    b, s, d = x.shape
    tm = TM_PROJ
    widths = (W_AB, W_CT, W_CKV, W_DT, W_DKV)
    return pl.pallas_call(
        functools.partial(_in_proj_kernel, layer=layer),
        grid=(b, s // tm),
        in_specs=[
            pl.BlockSpec((1, tm, d), lambda bi, i: (bi, i, 0)),
            pl.BlockSpec(g.shape, lambda bi, i: (0, 0)),
            pl.BlockSpec((1, d, D_IN), lambda bi, i: (layer, 0, 0), pipeline_mode=pl.Buffered(1)),
            pl.BlockSpec((tm, LANES), lambda bi, i: (i, 0)),
            pl.BlockSpec((tm, LANES), lambda bi, i: (i, 0)),
        ],
        out_specs=[pl.BlockSpec((1, tm, wd), lambda bi, i: (bi, i, 0)) for wd in widths],
        out_shape=[jax.ShapeDtypeStruct((b, s, wd), jnp.bfloat16) for wd in widths],
        scratch_shapes=[pltpu.VMEM((d, D_IN), jnp.bfloat16)],
        compiler_params=pltpu.CompilerParams(
            dimension_semantics=("arbitrary", "arbitrary"), vmem_limit_bytes=VMEM_LIMIT),
        name="in_proj",
    )(x, g, w, cos, sin)


def _conv_out_kernel(cur_ref, prev_ref, next_ref, aw_ref, ab_ref, lg_ref, lb_ref, bw_ref,
                     ycd_ref, x_ref, w_ref, g_ref, o_ref, hbuf, gbuf, hph, gph,
                     *, layer, final_norm):
    i = pl.program_id(1)
    n = pl.num_programs(1)
    t = T_MIX
    f32 = jnp.float32
    has_prev = (i > 0).astype(f32)
    has_next = (i < n - 1).astype(f32)
    row = lambda ref, j: ref[layer, j:j + 1, :]
    vec = lambda ref: ref[layer:layer + 1, :]
    col = lambda ref, rows, c: ref[0, rows, c:c + D_GROUP].astype(f32)

    full = slice(None)
    hwin = jnp.concatenate([col(prev_ref, full, P_H) * has_prev, col(cur_ref, full, P_H),
                            col(next_ref, full, P_H) * has_next], axis=0)
    gwin = jnp.concatenate([col(prev_ref, full, P_G) * has_prev, col(cur_ref, full, P_G),
                            col(next_ref, full, P_G) * has_next], axis=0)
    hbuf[...] = hwin
    gbuf[...] = gwin

    nwin = t + 2 * HALO
    nph = nwin - SUBLANES
    for p in range(1, SUBLANES):
        hph[p - 1] = pltpu.roll(hwin, nwin - p, 0)[0:nph]
    gph[0] = pltpu.roll(gwin, nwin - 1, 0)[0:nph]
    gph[1] = pltpu.roll(gwin, nwin - (SUBLANES - 1), 0)[0:nph]

    def tap(off, rows):
        p = off % SUBLANES
        if p == 0:
            return hbuf[off:off + rows, :]
        return hph[p - 1, off - p:off - p + rows, :]

    rc = CONV_ROWS
    for r0 in range(0, t, rc):
        rows = slice(r0, r0 + rc)
        acc = jnp.broadcast_to(vec(ab_ref), (rc, D_GROUP))
        for j in range(CONV_A_WIDTH):
            acc = acc + aw_ref[layer, j:j + 1, :] * tap(HALO - CONV_A_PAD + j + r0, rc)
        mu = jnp.mean(acc, axis=-1, keepdims=True)
        xc = acc - mu
        var = jnp.mean(xc * xc, axis=-1, keepdims=True)
        hn = xc * lax.rsqrt(var + EPS) * vec(lg_ref) + vec(lb_ref)
        ya = _silu(hn) * col(cur_ref, rows, P_ZA)
        gm = HALO + r0 - SUBLANES
        conv = (row(bw_ref, 0) * gph[1, gm:gm + rc, :]
                + row(bw_ref, 1) * gbuf[HALO + r0:HALO + r0 + rc, :]
                + row(bw_ref, 2) * gph[0, HALO + r0:HALO + r0 + rc, :])
        yb = col(cur_ref, rows, P_BZ) * conv
        yab = jnp.concatenate([ya.astype(jnp.bfloat16), yb.astype(jnp.bfloat16)], axis=1)
        xo = x_ref[0, rows, :] + jnp.dot(ycd_ref[0, rows, :], w_ref[0, 2 * D_GROUP:4 * D_GROUP, :],
                                         preferred_element_type=f32)
        xo = xo + jnp.dot(yab, w_ref[0, 0:2 * D_GROUP, :], preferred_element_type=f32)
        if final_norm:
            ms = jnp.mean(xo * xo, axis=-1, keepdims=True)
            xo = xo * lax.rsqrt(ms + EPS) * g_ref[...]
        o_ref[0, rows, :] = xo


def _conv_out(proj, ycd, x, aw, ab, lg, lb, bw, w, g, layer, final_norm):
    b, s, d = x.shape
    t = T_MIX
    hb = t // HALO
    nh = s // HALO
    f32 = jnp.float32
    whole = lambda a: pl.BlockSpec(a.shape, lambda bi, i: (0,) * a.ndim)
    return pl.pallas_call(
        functools.partial(_conv_out_kernel, layer=layer, final_norm=final_norm),
        grid=(b, s // t),
        in_specs=[
            pl.BlockSpec((1, t, W_AB), lambda bi, i: (bi, i, 0)),
            pl.BlockSpec((1, HALO, W_AB), lambda bi, i: (bi, jnp.maximum(i * hb - 1, 0), 0)),
            pl.BlockSpec((1, HALO, W_AB), lambda bi, i: (bi, jnp.minimum((i + 1) * hb, nh - 1), 0)),
            whole(aw), whole(ab), whole(lg), whole(lb), whole(bw),
            pl.BlockSpec((1, t, 2 * D_GROUP), lambda bi, i: (bi, i, 0)),
            pl.BlockSpec((1, t, d), lambda bi, i: (bi, i, 0)),
            pl.BlockSpec((1,) + w.shape[1:], lambda bi, i: (layer, 0, 0)),
            pl.BlockSpec((1, d), lambda bi, i: (0, 0)),
        ],
        out_specs=pl.BlockSpec((1, t, d), lambda bi, i: (bi, i, 0)),
        out_shape=jax.ShapeDtypeStruct((b, s, d), f32),
        scratch_shapes=[pltpu.VMEM((t + 2 * HALO, D_GROUP), f32),
                        pltpu.VMEM((t + 2 * HALO, D_GROUP), f32),
                        pltpu.VMEM((SUBLANES - 1, t + 2 * HALO - SUBLANES, D_GROUP), f32),
                        pltpu.VMEM((2, t + 2 * HALO - SUBLANES, D_GROUP), f32)],
        compiler_params=_params(),
        name="conv_out",
    )(proj, proj, proj, aw, ab, lg, lb, bw, ycd, x, w, g)


def _swa_tile(sink_ref, q_ref, qs_ref, z_ref, kv_ref, y_ref, i, layer):
    s_len = kv_ref.shape[1]
    nkeys = 3 * SWA_BLOCK
    f32 = jnp.float32
    lane = lax.broadcasted_iota(jnp.int32, (1, LANES), 1)
    lo = lane < HEAD_DIM
    zero = jnp.zeros((), jnp.bfloat16)
    ones = jnp.ones((nkeys, LANES), jnp.bfloat16)
    for nb in range(T_ATTN // SWA_BLOCK):
        q0 = i * T_ATTN + nb * SWA_BLOCK
        ws = pl.multiple_of(jnp.clip(q0 - SWA_BLOCK, 0, s_len - nkeys), SWA_BLOCK)
        rows = slice(nb * SWA_BLOCK, (nb + 1) * SWA_BLOCK)
        win = pl.ds(ws, nkeys)
        lhs = [
            jnp.where(lo, q_ref[0, rows, 0:LANES], zero),
            jnp.where(lo, qs_ref[0, rows, 0:LANES], zero),
            jnp.where(lo, zero, qs_ref[0, rows, LANES:2 * LANES]),
            jnp.where(lo, zero, q_ref[0, rows, LANES:2 * LANES]),
        ]
        kw = kv_ref[0, win, 0:LANES]
        vw = jnp.concatenate([kv_ref[0, win, LANES:2 * LANES], ones], axis=1)
        qpos = q0 + lax.broadcasted_iota(jnp.int32, (SWA_BLOCK, nkeys), 0)
        kpos = ws + lax.broadcasted_iota(jnp.int32, (SWA_BLOCK, nkeys), 1)
        valid = jnp.abs(kpos - qpos) <= SWA_WINDOW
        num, den = [], []
        for h in range(SWA_HEADS):
            sc = jnp.einsum("qd,kd->qk", lhs[h], kw, preferred_element_type=f32)
            sh = jnp.where(valid, sc, NEG_INF)
            sink = sink_ref[layer, h]
            m = jnp.maximum(jnp.max(sh, axis=-1, keepdims=True), sink)
            e = jnp.exp2(sh - m)
            pv = jnp.dot(e.astype(jnp.bfloat16), vw, preferred_element_type=f32)
            num.append(pv[:, :LANES])
            den.append(pv[:, LANES:] + jnp.exp2(sink - m))
        og0 = jnp.where(lo, num[0], pltpu.roll(num[1], HEAD_DIM, 1)) / jnp.where(lo, den[0], den[1])
        og1 = jnp.where(lo, pltpu.roll(num[2], HEAD_DIM, 1), num[3]) / jnp.where(lo, den[2], den[3])
        y = jnp.concatenate([og0, og1], axis=-1) * z_ref[0, rows, :].astype(f32)
        y_ref[rows, 0:D_GROUP] = y.astype(y_ref.dtype)


def _nbr_tile(q_ref, z_ref, k_ref, v_ref, bias_ref, y_ref, i):
    rows_total = k_ref.shape[1] // GRID_W
    rows_tile = T_ATTN // GRID_W
    nkeys = NA_KH * GRID_W
    f32 = jnp.float32
    lane = lax.broadcasted_iota(jnp.int32, (1, LANES), 1)
    lo = lane < HEAD_DIM
    zero = jnp.zeros((), jnp.bfloat16)
    ones = jnp.ones((nkeys, LANES), jnp.bfloat16)

    for rr in range(rows_tile):
        r = i * rows_tile + rr
        r0 = jnp.clip(r - NA_KH // 2, 0, rows_total - NA_KH)
        d0 = r0 - r + (NA_KH - 1)
        ks = pl.multiple_of(r0 * GRID_W, GRID_W)
        qrow = slice(rr * GRID_W, (rr + 1) * GRID_W)
        outs = []
        for g in range(2):
            cols = slice(g * LANES, (g + 1) * LANES)
            qg = q_ref[0, qrow, cols]
            lhs = jnp.concatenate([jnp.where(lo, qg, zero), jnp.where(lo, zero, qg)], axis=0)
            kw = k_ref[0, pl.ds(ks, nkeys), cols]
            vw = jnp.concatenate([v_ref[0, pl.ds(ks, nkeys), cols], ones], axis=1)
            sc = jnp.einsum("qd,kd->qk", lhs, kw, preferred_element_type=f32)
            bias = jnp.concatenate(
                [jnp.concatenate([bias_ref[0, 2 * g + hh, d0 + 2 * m]
                                  for m in range(NA_KH // 2)], axis=-1)
                 for hh in range(2)], axis=0)
            sc = sc + bias
            m_ = jnp.max(sc, axis=-1, keepdims=True)
            e = jnp.exp2(sc - m_)
            pv = jnp.dot(e.astype(jnp.bfloat16), vw, preferred_element_type=f32)
            pv = jnp.where(jnp.concatenate([lo, lo], axis=1), pv[0:GRID_W], pv[GRID_W:2 * GRID_W])
            outs.append(pv[:, :LANES] / pv[:, LANES:])
        y = jnp.concatenate(outs, axis=-1) * z_ref[0, qrow, :].astype(f32)
        y_ref[qrow, D_GROUP:2 * D_GROUP] = y.astype(y_ref.dtype)


def _attn_kernel(sink_ref, ct_ref, ckv_ref, dt_ref, dkv_ref, bias_ref, o_ref, *, layer):
    i = pl.program_id(1)
    y_ref = o_ref.at[0]
    cols = lambda ref, c: ref.at[:, :, pl.ds(c, D_GROUP)]
    _swa_tile(sink_ref, cols(ct_ref, P_CQ), cols(ct_ref, P_CQS), cols(ct_ref, P_ZC), ckv_ref,
              y_ref, i, layer)
    _nbr_tile(cols(dt_ref, P_DQ), cols(dt_ref, P_ZD), cols(dkv_ref, P_DK), cols(dkv_ref, P_DV),
              bias_ref, y_ref, i)


def _attn(c_tile, c_kv, d_tile, d_kv, sink, bias, layer):
    b, s, _ = c_tile.shape
    t = T_ATTN
    tile = lambda a: pl.BlockSpec((1, t, a.shape[2]), lambda bi, i, sk: (bi, i, 0))
    seq = lambda a: pl.BlockSpec((1, s, a.shape[2]), lambda bi, i, sk: (bi, 0, 0))
    return pl.pallas_call(
        functools.partial(_attn_kernel, layer=layer),
        grid_spec=pltpu.PrefetchScalarGridSpec(
            num_scalar_prefetch=1,
            grid=(b, s // t),
            in_specs=[
                tile(c_tile), seq(c_kv), tile(d_tile), seq(d_kv),
                pl.BlockSpec((1,) + bias.shape[1:], lambda bi, i, sk: (layer, 0, 0, 0, 0)),
            ],
            out_specs=pl.BlockSpec((1, t, 2 * D_GROUP), lambda bi, i, sk: (bi, i, 0)),
        ),
        out_shape=jax.ShapeDtypeStruct((b, s, 2 * D_GROUP), jnp.bfloat16),
        compiler_params=_params(),
        name="attn",
    )(sink, c_tile, c_kv, d_tile, d_kv, bias)


def _rope_tables(s):
    inv_freq = ROPE_THETA ** (-jnp.arange(0, HEAD_DIM, 2, dtype=jnp.float32) / HEAD_DIM)
    inv_freq = jnp.tile(inv_freq, LANES // HALF)
    hi = jnp.arange(0, s, GRID_W, dtype=jnp.float32)[:, None, None] * inv_freq
    lo = jnp.arange(GRID_W, dtype=jnp.float32)[None, :, None] * inv_freq
    ch, sh, cl, sl = jnp.cos(hi), jnp.sin(hi), jnp.cos(lo), jnp.sin(lo)
    cos = ch * cl - sh * sl
    sin = sh * cl + ch * sl
    return cos.reshape(s, LANES), sin.reshape(s, LANES)


def _nbr_bias_table(rpb):
    nrel = 2 * NA_KW - 1
    ndr = 2 * NA_KH - 1
    c = np.arange(GRID_W)
    c0 = np.clip(c - NA_KW // 2, 0, GRID_W - NA_KW)
    col_ok = (c[None, :] >= c0[:, None]) & (c[None, :] < c0[:, None] + NA_KW)
    dc = np.clip(c[None, :] - c[:, None], -(NA_KW - 1), NA_KW - 1) + (NA_KW - 1)
    onehot = (dc[None] == np.arange(nrel)[:, None, None]).astype(np.float32)
    pick = np.zeros((2, nrel, GRID_W, 2, GRID_W), np.float32)
    pick[0, :, :, 0, :] = onehot
    pick[1, :, :, 1, :] = onehot
    pick = pick.reshape(2 * nrel, GRID_W, LANES)
    dr = np.arange(ndr + 1)[:, None] + np.arange(2)[None, :]
    valid = (dr < ndr)[:, None, :, None] & col_ok[None, :, None, :]
    valid = valid.reshape(ndr + 1, GRID_W, LANES)
    ext = jnp.pad(rpb, ((0, 0), (0, 0), (0, 2), (0, 0)))
    pair = jnp.concatenate([ext[:, :, :-1], ext[:, :, 1:]], axis=-1)
    full = jnp.einsum("lhdc,cqj->lhdqj", pair, jnp.asarray(pick),
                      precision=lax.Precision.HIGHEST)
    return jnp.where(jnp.asarray(valid), full * LOG2E, NEG_INF)


def kernel(x, norm_g, w_in, w_out, conv_a_w, conv_a_b, ln_a_g, ln_a_b, conv_b_w, swa_sink,
           na_rpb, final_norm_g):
    depth = norm_g.shape[0]
    s = x.shape[1]
    assert s % max(TM_PROJ, T_MIX, T_ATTN) == 0 and s % GRID_W == 0, x.shape
    assert s // GRID_W >= NA_KH and s >= 3 * SWA_BLOCK, x.shape
    assert w_in.shape[1:] == (x.shape[2], D_IN) and w_out.shape[1:] == (4 * D_GROUP, x.shape[2])
    cos_t, sin_t = _rope_tables(s)
    w_out_b = w_out.astype(jnp.bfloat16)
    bias = _nbr_bias_table(na_rpb)
    sink = swa_sink * LOG2E
    fg = final_norm_g.reshape(1, -1)
    for l in range(depth):
        proj_ab, c_tile, c_kv, d_tile, d_kv = _in_proj(x, norm_g, w_in, cos_t, sin_t, l)
        ycd = _attn(c_tile, c_kv, d_tile, d_kv, sink, bias, l)
        x = _conv_out(proj_ab, ycd, x, conv_a_w, conv_a_b, ln_a_g, ln_a_b, conv_b_w,
                      w_out_b, fg, l, final_norm=(l == depth - 1))
    return x
```

```python
import functools
import math

import numpy as np
import jax
import jax.numpy as jnp
from jax import lax
from jax.experimental import pallas as pl
from jax.experimental.pallas import tpu as pltpu

D_GROUP = 256
HEAD_DIM = 64
HALF = HEAD_DIM // 2
GRID_W = 64
CONV_A_WIDTH = 31
CONV_A_PAD = (CONV_A_WIDTH - 1) // 2
SWA_WINDOW = 128
SWA_BLOCK = 128
SWA_HEADS = 4
NA_KH = 8
NA_KW = 16
ROPE_THETA = 10000.0
EPS = 1e-6
NEG_INF = -1e30
LOG2E = math.log2(math.e)
Q_SCALE = HEAD_DIM ** -0.5 * LOG2E

COL_A_U, COL_A_V, COL_A_Z = 0, 256, 512
COL_B_B, COL_B_C, COL_B_X, COL_B_Z = 768, 1024, 1280, 1536
COL_C_Q, COL_C_K, COL_C_V, COL_C_Z = 1792, 2048, 2176, 2304
COL_D_Q, COL_D_K, COL_D_V, COL_D_Z = 2560, 2816, 3072, 3328
D_IN = 3584

P_H, P_ZA = 0, 256
P_BZ, P_G = 512, 768
W_AB = 1024
P_CQ, P_CQS, P_ZC = 0, 256, 512
W_CT = 768
P_CK, P_CV = 0, 128
W_CKV = 256
P_DQ, P_ZD = 0, 256
W_DT = 512
P_DK, P_DV = 0, 256
W_DKV = 512

LANES = 128
SUBLANES = 8
HALO = 16
VMEM_LIMIT = 56 * 1024 * 1024

TM_PROJ = 1024
T_MIX = 1024
T_ATTN = 2048
PROJ_CHUNK = 512
CONV_ROWS = 256


def _silu(x):
    return x * jax.nn.sigmoid(x)


def _params():
    return pltpu.CompilerParams(
        dimension_semantics=("parallel", "parallel"), vmem_limit_bytes=VMEM_LIMIT)


def _in_proj_kernel(x_ref, g_ref, w32_ref, cos_ref, sin_ref, oab_ref, oct_ref, ockv_ref, odt_ref,
                    odkv_ref, w_ref, *, layer):
    @pl.when((pl.program_id(0) == 0) & (pl.program_id(1) == 0))
    def _():
        for c0 in range(0, D_IN, PROJ_CHUNK):
            w_ref[:, c0:c0 + PROJ_CHUNK] = w32_ref[0, :, c0:c0 + PROJ_CHUNK].astype(w_ref.dtype)

    x = x_ref[0]
    ms = jnp.mean(x * x, axis=-1, keepdims=True)
    h = (x * lax.rsqrt(ms + EPS) * g_ref[layer:layer + 1, :]).astype(jnp.bfloat16)
    lane = lax.broadcasted_iota(jnp.int32, (1, LANES), 1)
    first_half = (lane % HEAD_DIM) < HALF

    def rope(t):
        partner = jnp.where(first_half, -pltpu.roll(t, LANES - HALF, 1), pltpu.roll(t, HALF, 1))
        return t * cos_ref[...] + partner * sin_ref[...]

    def store(ref, col, t):
        ref[0, :, col:col + t.shape[1]] = t.astype(ref.dtype)

    def chunk(c0):
        return jnp.dot(h, w_ref[:, c0:c0 + PROJ_CHUNK], preferred_element_type=jnp.float32)

    g = D_GROUP
    acc = chunk(COL_A_U)
    store(oab_ref, P_H, acc[:, :g] * jax.nn.sigmoid(acc[:, g:]))
    acc = chunk(COL_A_Z)
    store(oab_ref, P_ZA, _silu(acc[:, :g]))
    b_gate = acc[:, g:]
    acc = chunk(COL_B_C)
    store(oab_ref, P_G, acc[:, :g] * acc[:, g:])
    acc = chunk(COL_B_Z)
    store(oab_ref, P_BZ, b_gate * _silu(acc[:, :g]))
    for s0 in range(0, g, LANES):
        t = rope(acc[:, g + s0:g + s0 + LANES]) * Q_SCALE
        store(oct_ref, P_CQ + s0, t)
        store(oct_ref, P_CQS + s0, pltpu.roll(t, HEAD_DIM, 1))
    acc = chunk(COL_C_K)
    store(ockv_ref, P_CK, rope(acc[:, :LANES]))
    store(ockv_ref, P_CV, acc[:, LANES:g])
    store(oct_ref, P_ZC, _silu(acc[:, g:]))
    acc = chunk(COL_D_Q)
    store(odt_ref, P_DQ, acc[:, :g] * Q_SCALE)
    store(odkv_ref, P_DK, acc[:, g:])
    acc = chunk(COL_D_V)
    store(odkv_ref, P_DV, acc[:, :g])
    store(odt_ref, P_ZD, _silu(acc[:, g:]))


def _in_proj(x, g, w, cos, sin, layer):
    b, s, d = x.shape
    tm = TM_PROJ
    widths = (W_AB, W_CT, W_CKV, W_DT, W_DKV)
    return pl.pallas_call(
        functools.partial(_in_proj_kernel, layer=layer),
        grid=(b, s // tm),
        in_specs=[
            pl.BlockSpec((1, tm, d), lambda bi, i: (bi, i, 0)),
            pl.BlockSpec(g.shape, lambda bi, i: (0, 0)),
            pl.BlockSpec((1, d, D_IN), lambda bi, i: (layer, 0, 0), pipeline_mode=pl.Buffered(1)),
            pl.BlockSpec((tm, LANES), lambda bi, i: (i, 0)),
            pl.BlockSpec((tm, LANES), lambda bi, i: (i, 0)),
        ],
        out_specs=[pl.BlockSpec((1, tm, wd), lambda bi, i: (bi, i, 0)) for wd in widths],
        out_shape=[jax.ShapeDtypeStruct((b, s, wd), jnp.bfloat16) for wd in widths],
        scratch_shapes=[pltpu.VMEM((d, D_IN), jnp.bfloat16)],
        compiler_params=pltpu.CompilerParams(
            dimension_semantics=("arbitrary", "arbitrary"), vmem_limit_bytes=VMEM_LIMIT),
        name="in_proj",
    )(x, g, w, cos, sin)


def _conv_out_kernel(cur_ref, prev_ref, next_ref, aw_ref, ab_ref, lg_ref, lb_ref, bw_ref,
                     ycd_ref, x_ref, w_ref, g_ref, o_ref, hbuf, gbuf, hph, gph,
                     *, layer, final_norm):
    i = pl.program_id(1)
    n = pl.num_programs(1)
    t = T_MIX
    f32 = jnp.float32
    has_prev = (i > 0).astype(f32)
    has_next = (i < n - 1).astype(f32)
    row = lambda ref, j: ref[layer, j:j + 1, :]
    vec = lambda ref: ref[layer:layer + 1, :]
    col = lambda ref, rows, c: ref[0, rows, c:c + D_GROUP].astype(f32)

    full = slice(None)
    hwin = jnp.concatenate([col(prev_ref, full, P_H) * has_prev, col(cur_ref, full, P_H),
                            col(next_ref, full, P_H) * has_next], axis=0)
    gwin = jnp.concatenate([col(prev_ref, full, P_G) * has_prev, col(cur_ref, full, P_G),
                            col(next_ref, full, P_G) * has_next], axis=0)
    hbuf[...] = hwin
    gbuf[...] = gwin

    nwin = t + 2 * HALO
    nph = nwin - SUBLANES
    for p in range(1, SUBLANES):
        hph[p - 1] = pltpu.roll(hwin, nwin - p, 0)[0:nph]
    gph[0] = pltpu.roll(gwin, nwin - 1, 0)[0:nph]
    gph[1] = pltpu.roll(gwin, nwin - (SUBLANES - 1), 0)[0:nph]

    def tap(off, rows):
        p = off % SUBLANES
        if p == 0:
            return hbuf[off:off + rows, :]
        return hph[p - 1, off - p:off - p + rows, :]

    rc = CONV_ROWS
    for r0 in range(0, t, rc):
        rows = slice(r0, r0 + rc)
        acc = jnp.broadcast_to(vec(ab_ref), (rc, D_GROUP))
        for j in range(CONV_A_WIDTH):
            acc = acc + aw_ref[layer, j:j + 1, :] * tap(HALO - CONV_A_PAD + j + r0, rc)
        mu = jnp.mean(acc, axis=-1, keepdims=True)
        xc = acc - mu
        var = jnp.mean(xc * xc, axis=-1, keepdims=True)
        hn = xc * lax.rsqrt(var + EPS) * vec(lg_ref) + vec(lb_ref)
        ya = _silu(hn) * col(cur_ref, rows, P_ZA)
        gm = HALO + r0 - SUBLANES
        conv = (row(bw_ref, 0) * gph[1, gm:gm + rc, :]
                + row(bw_ref, 1) * gbuf[HALO + r0:HALO + r0 + rc, :]
                + row(bw_ref, 2) * gph[0, HALO + r0:HALO + r0 + rc, :])
        yb = col(cur_ref, rows, P_BZ) * conv
        yab = jnp.concatenate([ya.astype(jnp.bfloat16), yb.astype(jnp.bfloat16)], axis=1)
        xo = x_ref[0, rows, :] + jnp.dot(ycd_ref[0, rows, :], w_ref[0, 2 * D_GROUP:4 * D_GROUP, :],
                                         preferred_element_type=f32)
        xo = xo + jnp.dot(yab, w_ref[0, 0:2 * D_GROUP, :], preferred_element_type=f32)
        if final_norm:
            ms = jnp.mean(xo * xo, axis=-1, keepdims=True)
            xo = xo * lax.rsqrt(ms + EPS) * g_ref[...]
        o_ref[0, rows, :] = xo


def _conv_out(proj, ycd, x, aw, ab, lg, lb, bw, w, g, layer, final_norm):
    b, s, d = x.shape
    t = T_MIX
    hb = t // HALO
    nh = s // HALO
    f32 = jnp.float32
    whole = lambda a: pl.BlockSpec(a.shape, lambda bi, i: (0,) * a.ndim)
    return pl.pallas_call(
        functools.partial(_conv_out_kernel, layer=layer, final_norm=final_norm),
        grid=(b, s // t),
        in_specs=[
            pl.BlockSpec((1, t, W_AB), lambda bi, i: (bi, i, 0)),
            pl.BlockSpec((1, HALO, W_AB), lambda bi, i: (bi, jnp.maximum(i * hb - 1, 0), 0)),
            pl.BlockSpec((1, HALO, W_AB), lambda bi, i: (bi, jnp.minimum((i + 1) * hb, nh - 1), 0)),
            whole(aw), whole(ab), whole(lg), whole(lb), whole(bw),
            pl.BlockSpec((1, t, 2 * D_GROUP), lambda bi, i: (bi, i, 0)),
            pl.BlockSpec((1, t, d), lambda bi, i: (bi, i, 0)),
            pl.BlockSpec((1,) + w.shape[1:], lambda bi, i: (layer, 0, 0)),
            pl.BlockSpec((1, d), lambda bi, i: (0, 0)),
        ],
        out_specs=pl.BlockSpec((1, t, d), lambda bi, i: (bi, i, 0)),
        out_shape=jax.ShapeDtypeStruct((b, s, d), f32),
        scratch_shapes=[pltpu.VMEM((t + 2 * HALO, D_GROUP), f32),
                        pltpu.VMEM((t + 2 * HALO, D_GROUP), f32),
                        pltpu.VMEM((SUBLANES - 1, t + 2 * HALO - SUBLANES, D_GROUP), f32),
                        pltpu.VMEM((2, t + 2 * HALO - SUBLANES, D_GROUP), f32)],
        compiler_params=_params(),
        name="conv_out",
    )(proj, proj, proj, aw, ab, lg, lb, bw, ycd, x, w, g)


def _swa_tile(sink_ref, q_ref, qs_ref, z_ref, kv_ref, y_ref, i, layer):
    s_len = kv_ref.shape[1]
    nkeys = 3 * SWA_BLOCK
    f32 = jnp.float32
    lane = lax.broadcasted_iota(jnp.int32, (1, LANES), 1)
    lo = lane < HEAD_DIM
    zero = jnp.zeros((), jnp.bfloat16)
    ones = jnp.ones((nkeys, LANES), jnp.bfloat16)
    for nb in range(T_ATTN // SWA_BLOCK):
        q0 = i * T_ATTN + nb * SWA_BLOCK
        ws = pl.multiple_of(jnp.clip(q0 - SWA_BLOCK, 0, s_len - nkeys), SWA_BLOCK)
        rows = slice(nb * SWA_BLOCK, (nb + 1) * SWA_BLOCK)
        win = pl.ds(ws, nkeys)
        lhs = [
            jnp.where(lo, q_ref[0, rows, 0:LANES], zero),
            jnp.where(lo, qs_ref[0, rows, 0:LANES], zero),
            jnp.where(lo, zero, qs_ref[0, rows, LANES:2 * LANES]),
            jnp.where(lo, zero, q_ref[0, rows, LANES:2 * LANES]),
        ]
        kw = kv_ref[0, win, 0:LANES]
        vw = jnp.concatenate([kv_ref[0, win, LANES:2 * LANES], ones], axis=1)
        qpos = q0 + lax.broadcasted_iota(jnp.int32, (SWA_BLOCK, nkeys), 0)
        kpos = ws + lax.broadcasted_iota(jnp.int32, (SWA_BLOCK, nkeys), 1)
        valid = jnp.abs(kpos - qpos) <= SWA_WINDOW
        num, den = [], []
        for h in range(SWA_HEADS):
            sc = jnp.einsum("qd,kd->qk", lhs[h], kw, preferred_element_type=f32)
            sh = jnp.where(valid, sc, NEG_INF)
            sink = sink_ref[layer, h]
            m = jnp.maximum(jnp.max(sh, axis=-1, keepdims=True), sink)
            e = jnp.exp2(sh - m)
            pv = jnp.dot(e.astype(jnp.bfloat16), vw, preferred_element_type=f32)
            num.append(pv[:, :LANES])
            den.append(pv[:, LANES:] + jnp.exp2(sink - m))
        og0 = jnp.where(lo, num[0], pltpu.roll(num[1], HEAD_DIM, 1)) / jnp.where(lo, den[0], den[1])
        og1 = jnp.where(lo, pltpu.roll(num[2], HEAD_DIM, 1), num[3]) / jnp.where(lo, den[2], den[3])
        y = jnp.concatenate([og0, og1], axis=-1) * z_ref[0, rows, :].astype(f32)
        y_ref[rows, 0:D_GROUP] = y.astype(y_ref.dtype)


def _nbr_tile(q_ref, z_ref, k_ref, v_ref, bias_ref, y_ref, i):
    rows_total = k_ref.shape[1] // GRID_W
    rows_tile = T_ATTN // GRID_W
    nkeys = NA_KH * GRID_W
    f32 = jnp.float32
    lane = lax.broadcasted_iota(jnp.int32, (1, LANES), 1)
    lo = lane < HEAD_DIM
    zero = jnp.zeros((), jnp.bfloat16)
    ones = jnp.ones((nkeys, LANES), jnp.bfloat16)

    for rr in range(rows_tile):
        r = i * rows_tile + rr
        r0 = jnp.clip(r - NA_KH // 2, 0, rows_total - NA_KH)
        d0 = r0 - r + (NA_KH - 1)
        ks = pl.multiple_of(r0 * GRID_W, GRID_W)
        qrow = slice(rr * GRID_W, (rr + 1) * GRID_W)
        outs = []
        for g in range(2):
            cols = slice(g * LANES, (g + 1) * LANES)
            qg = q_ref[0, qrow, cols]
            lhs = jnp.concatenate([jnp.where(lo, qg, zero), jnp.where(lo, zero, qg)], axis=0)
            kw = k_ref[0, pl.ds(ks, nkeys), cols]
            vw = jnp.concatenate([v_ref[0, pl.ds(ks, nkeys), cols], ones], axis=1)
            sc = jnp.einsum("qd,kd->qk", lhs, kw, preferred_element_type=f32)
            bias = jnp.concatenate(
                [jnp.concatenate([bias_ref[0, 2 * g + hh, d0 + 2 * m]
                                  for m in range(NA_KH // 2)], axis=-1)
                 for hh in range(2)], axis=0)
            sc = sc + bias
            m_ = jnp.max(sc, axis=-1, keepdims=True)
            e = jnp.exp2(sc - m_)
            pv = jnp.dot(e.astype(jnp.bfloat16), vw, preferred_element_type=f32)
            pv = jnp.where(jnp.concatenate([lo, lo], axis=1), pv[0:GRID_W], pv[GRID_W:2 * GRID_W])
            outs.append(pv[:, :LANES] / pv[:, LANES:])
        y = jnp.concatenate(outs, axis=-1) * z_ref[0, qrow, :].astype(f32)
        y_ref[qrow, D_GROUP:2 * D_GROUP] = y.astype(y_ref.dtype)


def _attn_kernel(sink_ref, ct_ref, ckv_ref, dt_ref, dkv_ref, bias_ref, o_ref, *, layer):
    i = pl.program_id(1)
    y_ref = o_ref.at[0]
    cols = lambda ref, c: ref.at[:, :, pl.ds(c, D_GROUP)]
    _swa_tile(sink_ref, cols(ct_ref, P_CQ), cols(ct_ref, P_CQS), cols(ct_ref, P_ZC), ckv_ref,
              y_ref, i, layer)
    _nbr_tile(cols(dt_ref, P_DQ), cols(dt_ref, P_ZD), cols(dkv_ref, P_DK), cols(dkv_ref, P_DV),
              bias_ref, y_ref, i)


def _attn(c_tile, c_kv, d_tile, d_kv, sink, bias, layer):
    b, s, _ = c_tile.shape
    t = T_ATTN
    tile = lambda a: pl.BlockSpec((1, t, a.shape[2]), lambda bi, i, sk: (bi, i, 0))
    seq = lambda a: pl.BlockSpec((1, s, a.shape[2]), lambda bi, i, sk: (bi, 0, 0))
    return pl.pallas_call(
        functools.partial(_attn_kernel, layer=layer),
        grid_spec=pltpu.PrefetchScalarGridSpec(
            num_scalar_prefetch=1,
            grid=(b, s // t),
            in_specs=[
                tile(c_tile), seq(c_kv), tile(d_tile), seq(d_kv),
                pl.BlockSpec((1,) + bias.shape[1:], lambda bi, i, sk: (layer, 0, 0, 0, 0)),
            ],
            out_specs=pl.BlockSpec((1, t, 2 * D_GROUP), lambda bi, i, sk: (bi, i, 0)),
        ),
        out_shape=jax.ShapeDtypeStruct((b, s, 2 * D_GROUP), jnp.bfloat16),
        compiler_params=_params(),
        name="attn",
    )(sink, c_tile, c_kv, d_tile, d_kv, bias)


def _rope_tables(s):
    inv_freq = ROPE_THETA ** (-jnp.arange(0, HEAD_DIM, 2, dtype=jnp.float32) / HEAD_DIM)
    inv_freq = jnp.tile(inv_freq, LANES // HALF)
    hi = jnp.arange(0, s, GRID_W, dtype=jnp.float32)[:, None, None] * inv_freq
    lo = jnp.arange(GRID_W, dtype=jnp.float32)[None, :, None] * inv_freq
    ch, sh, cl, sl = jnp.cos(hi), jnp.sin(hi), jnp.cos(lo), jnp.sin(lo)
    cos = ch * cl - sh * sl
    sin = sh * cl + ch * sl
    return cos.reshape(s, LANES), sin.reshape(s, LANES)


def _nbr_bias_table(rpb):
    nrel = 2 * NA_KW - 1
    ndr = 2 * NA_KH - 1
    c = np.arange(GRID_W)
    c0 = np.clip(c - NA_KW // 2, 0, GRID_W - NA_KW)
    col_ok = (c[None, :] >= c0[:, None]) & (c[None, :] < c0[:, None] + NA_KW)
    dc = np.clip(c[None, :] - c[:, None], -(NA_KW - 1), NA_KW - 1) + (NA_KW - 1)
    onehot = (dc[None] == np.arange(nrel)[:, None, None]).astype(np.float32)
    pick = np.zeros((2, nrel, GRID_W, 2, GRID_W), np.float32)
    pick[0, :, :, 0, :] = onehot
    pick[1, :, :, 1, :] = onehot
    pick = pick.reshape(2 * nrel, GRID_W, LANES)
    dr = np.arange(ndr + 1)[:, None] + np.arange(2)[None, :]
    valid = (dr < ndr)[:, None, :, None] & col_ok[None, :, None, :]
    valid = valid.reshape(ndr + 1, GRID_W, LANES)
    ext = jnp.pad(rpb, ((0, 0), (0, 0), (0, 2), (0, 0)))
    pair = jnp.concatenate([ext[:, :, :-1], ext[:, :, 1:]], axis=-1)
    full = jnp.einsum("lhdc,cqj->lhdqj", pair, jnp.asarray(pick),
                      precision=lax.Precision.HIGHEST)
    return jnp.where(jnp.asarray(valid), full * LOG2E, NEG_INF)


def kernel(x, norm_g, w_in, w_out, conv_a_w, conv_a_b, ln_a_g, ln_a_b, conv_b_w, swa_sink,
           na_rpb, final_norm_g):
    depth = norm_g.shape[0]
    s = x.shape[1]
    assert s % max(TM_PROJ, T_MIX, T_ATTN) == 0 and s % GRID_W == 0, x.shape
    assert s // GRID_W >= NA_KH and s >= 3 * SWA_BLOCK, x.shape
    assert w_in.shape[1:] == (x.shape[2], D_IN) and w_out.shape[1:] == (4 * D_GROUP, x.shape[2])
    cos_t, sin_t = _rope_tables(s)
    w_out_b = w_out.astype(jnp.bfloat16)
    bias = _nbr_bias_table(na_rpb)
    sink = swa_sink * LOG2E
    fg = final_norm_g.reshape(1, -1)
    for l in range(depth):
        proj_ab, c_tile, c_kv, d_tile, d_kv = _in_proj(x, norm_g, w_in, cos_t, sin_t, l)
        ycd = _attn(c_tile, c_kv, d_tile, d_kv, sink, bias, l)
        x = _conv_out(proj_ab, ycd, x, conv_a_w, conv_a_b, ln_a_g, ln_a_b, conv_b_w,
                      w_out_b, fg, l, final_norm=(l == depth - 1))
    return x
```

```python
import functools
import math

import numpy as np
import jax
import jax.numpy as jnp
from jax import lax
from jax.experimental import pallas as pl
from jax.experimental.pallas import tpu as pltpu

D_GROUP = 256
HEAD_DIM = 64
HALF = HEAD_DIM // 2
GRID_W = 64
CONV_A_WIDTH = 31
CONV_A_PAD = (CONV_A_WIDTH - 1) // 2
SWA_WINDOW = 128
SWA_BLOCK = 128
SWA_HEADS = 4
NA_KH = 8
NA_KW = 16
ROPE_THETA = 10000.0
EPS = 1e-6
NEG_INF = -1e30
LOG2E = math.log2(math.e)
Q_SCALE = HEAD_DIM ** -0.5 * LOG2E

COL_A_U, COL_A_V, COL_A_Z = 0, 256, 512
COL_B_B, COL_B_C, COL_B_X, COL_B_Z = 768, 1024, 1280, 1536
COL_C_Q, COL_C_K, COL_C_V, COL_C_Z = 1792, 2048, 2176, 2304
COL_D_Q, COL_D_K, COL_D_V, COL_D_Z = 2560, 2816, 3072, 3328
D_IN = 3584

P_H, P_ZA = 0, 256
P_BZ, P_G = 512, 768
W_AB = 1024
P_CQ, P_CQS, P_ZC = 0, 256, 512
W_CT = 768
P_CK, P_CV = 0, 128
W_CKV = 256
P_DQ, P_ZD = 0, 256
W_DT = 512
P_DK, P_DV = 0, 256
W_DKV = 512

LANES = 128
SUBLANES = 8
HALO = 16
VMEM_LIMIT = 56 * 1024 * 1024

TM_PROJ = 1024
T_MIX = 1024
T_ATTN = 2048
PROJ_CHUNK = 512
PROJ_ROWS = 256
CONV_ROWS = 256


def _silu(x):
    return x * jax.nn.sigmoid(x)


def _params():
    return pltpu.CompilerParams(
        dimension_semantics=("parallel", "parallel"), vmem_limit_bytes=VMEM_LIMIT)


def _in_proj_kernel(x_ref, g_ref, w32_ref, cos_ref, sin_ref, oab_ref, oct_ref, ockv_ref, odt_ref,
                    odkv_ref, w_ref, *, layer):
    @pl.when((pl.program_id(0) == 0) & (pl.program_id(1) == 0))
    def _():
        for c0 in range(0, D_IN, PROJ_CHUNK):
            w_ref[:, c0:c0 + PROJ_CHUNK] = w32_ref[0, :, c0:c0 + PROJ_CHUNK].astype(w_ref.dtype)

    lane = lax.broadcasted_iota(jnp.int32, (1, LANES), 1)
    first_half = (lane % HEAD_DIM) < HALF
    for r0 in range(0, x_ref.shape[1], PROJ_ROWS):
        _in_proj_rows(x_ref, g_ref, cos_ref, sin_ref, oab_ref, oct_ref, ockv_ref, odt_ref, odkv_ref,
                      w_ref, first_half, r0, layer)


def _in_proj_rows(x_ref, g_ref, cos_ref, sin_ref, oab_ref, oct_ref, ockv_ref, odt_ref, odkv_ref,
                  w_ref, first_half, r0, layer):
    rows = pl.ds(r0, PROJ_ROWS)
    x = x_ref[0, rows, :]
    ms = jnp.mean(x * x, axis=-1, keepdims=True)
    h = (x * lax.rsqrt(ms + EPS) * g_ref[layer:layer + 1, :]).astype(jnp.bfloat16)

    def rope(t):
        partner = jnp.where(first_half, -pltpu.roll(t, LANES - HALF, 1), pltpu.roll(t, HALF, 1))
        return t * cos_ref[rows, :] + partner * sin_ref[rows, :]

    def store(ref, col, t):
        ref[0, rows, col:col + t.shape[1]] = t.astype(ref.dtype)

    def chunk(c0):
        return jnp.dot(h, w_ref[:, c0:c0 + PROJ_CHUNK], preferred_element_type=jnp.float32)

    g = D_GROUP
    acc = chunk(COL_A_U)
    store(oab_ref, P_H, acc[:, :g] * jax.nn.sigmoid(acc[:, g:]))
    acc = chunk(COL_A_Z)
    store(oab_ref, P_ZA, _silu(acc[:, :g]))
    b_gate = acc[:, g:]
    acc = chunk(COL_B_C)
    store(oab_ref, P_G, acc[:, :g] * acc[:, g:])
    acc = chunk(COL_B_Z)
    store(oab_ref, P_BZ, b_gate * _silu(acc[:, :g]))
    for s0 in range(0, g, LANES):
        t = rope(acc[:, g + s0:g + s0 + LANES]) * Q_SCALE
        store(oct_ref, P_CQ + s0, t)
        store(oct_ref, P_CQS + s0, pltpu.roll(t, HEAD_DIM, 1))
    acc = chunk(COL_C_K)
    store(ockv_ref, P_CK, rope(acc[:, :LANES]))
    store(ockv_ref, P_CV, acc[:, LANES:g])
    store(oct_ref, P_ZC, _silu(acc[:, g:]))
    acc = chunk(COL_D_Q)
    store(odt_ref, P_DQ, acc[:, :g] * Q_SCALE)
    store(odkv_ref, P_DK, acc[:, g:])
    acc = chunk(COL_D_V)
    store(odkv_ref, P_DV, acc[:, :g])
    store(odt_ref, P_ZD, _silu(acc[:, g:]))


def _in_proj(x, g, w, cos, sin, layer):
    b, s, d = x.shape
    tm = TM_PROJ
    widths = (W_AB, W_CT, W_CKV, W_DT, W_DKV)
    return pl.pallas_call(
        functools.partial(_in_proj_kernel, layer=layer),
        grid=(b, s // tm),
        in_specs=[
            pl.BlockSpec((1, tm, d), lambda bi, i: (bi, i, 0)),
            pl.BlockSpec(g.shape, lambda bi, i: (0, 0)),
            pl.BlockSpec((1, d, D_IN), lambda bi, i: (layer, 0, 0), pipeline_mode=pl.Buffered(1)),
            pl.BlockSpec((tm, LANES), lambda bi, i: (i, 0)),
            pl.BlockSpec((tm, LANES), lambda bi, i: (i, 0)),
        ],
        out_specs=[pl.BlockSpec((1, tm, wd), lambda bi, i: (bi, i, 0)) for wd in widths],
        out_shape=[jax.ShapeDtypeStruct((b, s, wd), jnp.bfloat16) for wd in widths],
        scratch_shapes=[pltpu.VMEM((d, D_IN), jnp.bfloat16)],
        compiler_params=pltpu.CompilerParams(
            dimension_semantics=("arbitrary", "arbitrary"), vmem_limit_bytes=VMEM_LIMIT),
        name="in_proj",
    )(x, g, w, cos, sin)


def _conv_out_kernel(cur_ref, prev_ref, next_ref, aw_ref, ab_ref, lg_ref, lb_ref, bw_ref,
                     ycd_ref, x_ref, w_ref, g_ref, o_ref, hbuf, gbuf, hph, gph,
                     *, layer, final_norm):
    i = pl.program_id(1)
    n = pl.num_programs(1)
    t = T_MIX
    f32 = jnp.float32
    has_prev = (i > 0).astype(f32)
    has_next = (i < n - 1).astype(f32)
    row = lambda ref, j: ref[layer, j:j + 1, :]
    vec = lambda ref: ref[layer:layer + 1, :]
    col = lambda ref, rows, c: ref[0, rows, c:c + D_GROUP].astype(f32)

    full = slice(None)
    hwin = jnp.concatenate([col(prev_ref, full, P_H) * has_prev, col(cur_ref, full, P_H),
                            col(next_ref, full, P_H) * has_next], axis=0)
    gwin = jnp.concatenate([col(prev_ref, full, P_G) * has_prev, col(cur_ref, full, P_G),
                            col(next_ref, full, P_G) * has_next], axis=0)
    hbuf[...] = hwin
    gbuf[...] = gwin

    nwin = t + 2 * HALO
    nph = nwin - SUBLANES
    for p in range(1, SUBLANES):
        hph[p - 1] = pltpu.roll(hwin, nwin - p, 0)[0:nph]
    gph[0] = pltpu.roll(gwin, nwin - 1, 0)[0:nph]
    gph[1] = pltpu.roll(gwin, nwin - (SUBLANES - 1), 0)[0:nph]

    def tap(off, rows):
        p = off % SUBLANES
        if p == 0:
            return hbuf[off:off + rows, :]
        return hph[p - 1, off - p:off - p + rows, :]

    rc = CONV_ROWS
    for r0 in range(0, t, rc):
        rows = slice(r0, r0 + rc)
        acc = jnp.broadcast_to(vec(ab_ref), (rc, D_GROUP))
        for j in range(CONV_A_WIDTH):
            acc = acc + aw_ref[layer, j:j + 1, :] * tap(HALO - CONV_A_PAD + j + r0, rc)
        mu = jnp.mean(acc, axis=-1, keepdims=True)
        xc = acc - mu
        var = jnp.mean(xc * xc, axis=-1, keepdims=True)
        hn = xc * lax.rsqrt(var + EPS) * vec(lg_ref) + vec(lb_ref)
        ya = _silu(hn) * col(cur_ref, rows, P_ZA)
        gm = HALO + r0 - SUBLANES
        conv = (row(bw_ref, 0) * gph[1, gm:gm + rc, :]
                + row(bw_ref, 1) * gbuf[HALO + r0:HALO + r0 + rc, :]
                + row(bw_ref, 2) * gph[0, HALO + r0:HALO + r0 + rc, :])
        yb = col(cur_ref, rows, P_BZ) * conv
        yab = jnp.concatenate([ya.astype(jnp.bfloat16), yb.astype(jnp.bfloat16)], axis=1)
        xo = x_ref[0, rows, :] + jnp.dot(ycd_ref[0, rows, :], w_ref[0, 2 * D_GROUP:4 * D_GROUP, :],
                                         preferred_element_type=f32)
        xo = xo + jnp.dot(yab, w_ref[0, 0:2 * D_GROUP, :], preferred_element_type=f32)
        if final_norm:
            ms = jnp.mean(xo * xo, axis=-1, keepdims=True)
            xo = xo * lax.rsqrt(ms + EPS) * g_ref[...]
        o_ref[0, rows, :] = xo


def _conv_out(proj, ycd, x, aw, ab, lg, lb, bw, w, g, layer, final_norm):
    b, s, d = x.shape
    t = T_MIX
    hb = t // HALO
    nh = s // HALO
    f32 = jnp.float32
    whole = lambda a: pl.BlockSpec(a.shape, lambda bi, i: (0,) * a.ndim)
    return pl.pallas_call(
        functools.partial(_conv_out_kernel, layer=layer, final_norm=final_norm),
        grid=(b, s // t),
        in_specs=[
            pl.BlockSpec((1, t, W_AB), lambda bi, i: (bi, i, 0)),
            pl.BlockSpec((1, HALO, W_AB), lambda bi, i: (bi, jnp.maximum(i * hb - 1, 0), 0)),
            pl.BlockSpec((1, HALO, W_AB), lambda bi, i: (bi, jnp.minimum((i + 1) * hb, nh - 1), 0)),
            whole(aw), whole(ab), whole(lg), whole(lb), whole(bw),
            pl.BlockSpec((1, t, 2 * D_GROUP), lambda bi, i: (bi, i, 0)),
            pl.BlockSpec((1, t, d), lambda bi, i: (bi, i, 0)),
            pl.BlockSpec((1,) + w.shape[1:], lambda bi, i: (layer, 0, 0)),
            pl.BlockSpec((1, d), lambda bi, i: (0, 0)),
        ],
        out_specs=pl.BlockSpec((1, t, d), lambda bi, i: (bi, i, 0)),
        out_shape=jax.ShapeDtypeStruct((b, s, d), f32),
        scratch_shapes=[pltpu.VMEM((t + 2 * HALO, D_GROUP), f32),
                        pltpu.VMEM((t + 2 * HALO, D_GROUP), f32),
                        pltpu.VMEM((SUBLANES - 1, t + 2 * HALO - SUBLANES, D_GROUP), f32),
                        pltpu.VMEM((2, t + 2 * HALO - SUBLANES, D_GROUP), f32)],
        compiler_params=_params(),
        name="conv_out",
    )(proj, proj, proj, aw, ab, lg, lb, bw, ycd, x, w, g)


def _swa_tile(sink_ref, q_ref, qs_ref, z_ref, kv_ref, y_ref, i, layer):
    s_len = kv_ref.shape[1]
    nkeys = 3 * SWA_BLOCK
    f32 = jnp.float32
    lane = lax.broadcasted_iota(jnp.int32, (1, LANES), 1)
    lo = lane < HEAD_DIM
    zero = jnp.zeros((), jnp.bfloat16)
    ones = jnp.ones((nkeys, LANES), jnp.bfloat16)
    for nb in range(T_ATTN // SWA_BLOCK):
        q0 = i * T_ATTN + nb * SWA_BLOCK
        ws = pl.multiple_of(jnp.clip(q0 - SWA_BLOCK, 0, s_len - nkeys), SWA_BLOCK)
        rows = slice(nb * SWA_BLOCK, (nb + 1) * SWA_BLOCK)
        win = pl.ds(ws, nkeys)
        lhs = [
            jnp.where(lo, q_ref[0, rows, 0:LANES], zero),
            jnp.where(lo, qs_ref[0, rows, 0:LANES], zero),
            jnp.where(lo, zero, qs_ref[0, rows, LANES:2 * LANES]),
            jnp.where(lo, zero, q_ref[0, rows, LANES:2 * LANES]),
        ]
        kw = kv_ref[0, win, 0:LANES]
        vw = jnp.concatenate([kv_ref[0, win, LANES:2 * LANES], ones], axis=1)
        qpos = q0 + lax.broadcasted_iota(jnp.int32, (SWA_BLOCK, nkeys), 0)
        kpos = ws + lax.broadcasted_iota(jnp.int32, (SWA_BLOCK, nkeys), 1)
        valid = jnp.abs(kpos - qpos) <= SWA_WINDOW
        num, den = [], []
        for h in range(SWA_HEADS):
            sc = jnp.einsum("qd,kd->qk", lhs[h], kw, preferred_element_type=f32)
            sh = jnp.where(valid, sc, NEG_INF)
            sink = sink_ref[layer, h]
            m = jnp.maximum(jnp.max(sh, axis=-1, keepdims=True), sink)
            e = jnp.exp2(sh - m)
            pv = jnp.dot(e.astype(jnp.bfloat16), vw, preferred_element_type=f32)
            num.append(pv[:, :LANES])
            den.append(pv[:, LANES:] + jnp.exp2(sink - m))
        og0 = jnp.where(lo, num[0], pltpu.roll(num[1], HEAD_DIM, 1)) / jnp.where(lo, den[0], den[1])
        og1 = jnp.where(lo, pltpu.roll(num[2], HEAD_DIM, 1), num[3]) / jnp.where(lo, den[2], den[3])
        y = jnp.concatenate([og0, og1], axis=-1) * z_ref[0, rows, :].astype(f32)
        y_ref[rows, 0:D_GROUP] = y.astype(y_ref.dtype)


def _nbr_tile(q_ref, z_ref, k_ref, v_ref, bias_ref, y_ref, i):
    rows_total = k_ref.shape[1] // GRID_W
    rows_tile = T_ATTN // GRID_W
    nkeys = NA_KH * GRID_W
    f32 = jnp.float32
    lane = lax.broadcasted_iota(jnp.int32, (1, LANES), 1)
    lo = lane < HEAD_DIM
    zero = jnp.zeros((), jnp.bfloat16)
    ones = jnp.ones((nkeys, LANES), jnp.bfloat16)

    for rr in range(rows_tile):
        r = i * rows_tile + rr
        r0 = jnp.clip(r - NA_KH // 2, 0, rows_total - NA_KH)
        d0 = r0 - r + (NA_KH - 1)
        ks = pl.multiple_of(r0 * GRID_W, GRID_W)
        qrow = slice(rr * GRID_W, (rr + 1) * GRID_W)
        outs = []
        for g in range(2):
            cols = slice(g * LANES, (g + 1) * LANES)
            qg = q_ref[0, qrow, cols]
            lhs = jnp.concatenate([jnp.where(lo, qg, zero), jnp.where(lo, zero, qg)], axis=0)
            kw = k_ref[0, pl.ds(ks, nkeys), cols]
            vw = jnp.concatenate([v_ref[0, pl.ds(ks, nkeys), cols], ones], axis=1)
            sc = jnp.einsum("qd,kd->qk", lhs, kw, preferred_element_type=f32)
            bias = jnp.concatenate(
                [jnp.concatenate([bias_ref[0, 2 * g + hh, d0 + 2 * m]
                                  for m in range(NA_KH // 2)], axis=-1)
                 for hh in range(2)], axis=0)
            sc = sc + bias
            m_ = jnp.max(sc, axis=-1, keepdims=True)
            e = jnp.exp2(sc - m_)
            pv = jnp.dot(e.astype(jnp.bfloat16), vw, preferred_element_type=f32)
            pv = jnp.where(jnp.concatenate([lo, lo], axis=1), pv[0:GRID_W], pv[GRID_W:2 * GRID_W])
            outs.append(pv[:, :LANES] / pv[:, LANES:])
        y = jnp.concatenate(outs, axis=-1) * z_ref[0, qrow, :].astype(f32)
        y_ref[qrow, D_GROUP:2 * D_GROUP] = y.astype(y_ref.dtype)


def _attn_kernel(sink_ref, ct_ref, ckv_ref, dt_ref, dkv_ref, bias_ref, o_ref, *, layer):
    i = pl.program_id(1)
    y_ref = o_ref.at[0]
    cols = lambda ref, c: ref.at[:, :, pl.ds(c, D_GROUP)]
    _swa_tile(sink_ref, cols(ct_ref, P_CQ), cols(ct_ref, P_CQS), cols(ct_ref, P_ZC), ckv_ref,
              y_ref, i, layer)
    _nbr_tile(cols(dt_ref, P_DQ), cols(dt_ref, P_ZD), cols(dkv_ref, P_DK), cols(dkv_ref, P_DV),
              bias_ref, y_ref, i)


def _attn(c_tile, c_kv, d_tile, d_kv, sink, bias, layer):
    b, s, _ = c_tile.shape
    t = T_ATTN
    tile = lambda a: pl.BlockSpec((1, t, a.shape[2]), lambda bi, i, sk: (bi, i, 0))
    seq = lambda a: pl.BlockSpec((1, s, a.shape[2]), lambda bi, i, sk: (bi, 0, 0))
    return pl.pallas_call(
        functools.partial(_attn_kernel, layer=layer),
        grid_spec=pltpu.PrefetchScalarGridSpec(
            num_scalar_prefetch=1,
            grid=(b, s // t),
            in_specs=[
                tile(c_tile), seq(c_kv), tile(d_tile), seq(d_kv),
                pl.BlockSpec((1,) + bias.shape[1:], lambda bi, i, sk: (layer, 0, 0, 0, 0)),
            ],
            out_specs=pl.BlockSpec((1, t, 2 * D_GROUP), lambda bi, i, sk: (bi, i, 0)),
        ),
        out_shape=jax.ShapeDtypeStruct((b, s, 2 * D_GROUP), jnp.bfloat16),
        compiler_params=_params(),
        name="attn",
    )(sink, c_tile, c_kv, d_tile, d_kv, bias)


def _rope_tables(s):
    inv_freq = ROPE_THETA ** (-jnp.arange(0, HEAD_DIM, 2, dtype=jnp.float32) / HEAD_DIM)
    inv_freq = jnp.tile(inv_freq, LANES // HALF)
    hi = jnp.arange(0, s, GRID_W, dtype=jnp.float32)[:, None, None] * inv_freq
    lo = jnp.arange(GRID_W, dtype=jnp.float32)[None, :, None] * inv_freq
    ch, sh, cl, sl = jnp.cos(hi), jnp.sin(hi), jnp.cos(lo), jnp.sin(lo)
    cos = ch * cl - sh * sl
    sin = sh * cl + ch * sl
    return cos.reshape(s, LANES), sin.reshape(s, LANES)


def _nbr_bias_table(rpb):
    nrel = 2 * NA_KW - 1
    ndr = 2 * NA_KH - 1
    c = np.arange(GRID_W)
    c0 = np.clip(c - NA_KW // 2, 0, GRID_W - NA_KW)
    col_ok = (c[None, :] >= c0[:, None]) & (c[None, :] < c0[:, None] + NA_KW)
    dc = np.clip(c[None, :] - c[:, None], -(NA_KW - 1), NA_KW - 1) + (NA_KW - 1)
    onehot = (dc[None] == np.arange(nrel)[:, None, None]).astype(np.float32)
    pick = np.zeros((2, nrel, GRID_W, 2, GRID_W), np.float32)
    pick[0, :, :, 0, :] = onehot
    pick[1, :, :, 1, :] = onehot
    pick = pick.reshape(2 * nrel, GRID_W, LANES)
    dr = np.arange(ndr + 1)[:, None] + np.arange(2)[None, :]
    valid = (dr < ndr)[:, None, :, None] & col_ok[None, :, None, :]
    valid = valid.reshape(ndr + 1, GRID_W, LANES)
    ext = jnp.pad(rpb, ((0, 0), (0, 0), (0, 2), (0, 0)))
    pair = jnp.concatenate([ext[:, :, :-1], ext[:, :, 1:]], axis=-1)
    full = jnp.einsum("lhdc,cqj->lhdqj", pair, jnp.asarray(pick),
                      precision=lax.Precision.HIGHEST)
    return jnp.where(jnp.asarray(valid), full * LOG2E, NEG_INF)


def kernel(x, norm_g, w_in, w_out, conv_a_w, conv_a_b, ln_a_g, ln_a_b, conv_b_w, swa_sink,
           na_rpb, final_norm_g):
    depth = norm_g.shape[0]
    s = x.shape[1]
    assert s % max(TM_PROJ, T_MIX, T_ATTN) == 0 and s % GRID_W == 0, x.shape
    assert s // GRID_W >= NA_KH and s >= 3 * SWA_BLOCK, x.shape
    assert w_in.shape[1:] == (x.shape[2], D_IN) and w_out.shape[1:] == (4 * D_GROUP, x.shape[2])
    cos_t, sin_t = _rope_tables(s)
    w_out_b = w_out.astype(jnp.bfloat16)
    bias = _nbr_bias_table(na_rpb)
    sink = swa_sink * LOG2E
    fg = final_norm_g.reshape(1, -1)
    for l in range(depth):
        proj_ab, c_tile, c_kv, d_tile, d_kv = _in_proj(x, norm_g, w_in, cos_t, sin_t, l)
        ycd = _attn(c_tile, c_kv, d_tile, d_kv, sink, bias, l)
        x = _conv_out(proj_ab, ycd, x, conv_a_w, conv_a_b, ln_a_g, ln_a_b, conv_b_w,
                      w_out_b, fg, l, final_norm=(l == depth - 1))
    return x
```

```python
import functools
import math

import numpy as np
import jax
import jax.numpy as jnp
from jax import lax
from jax.experimental import pallas as pl
from jax.experimental.pallas import tpu as pltpu

D_GROUP = 256
HEAD_DIM = 64
HALF = HEAD_DIM // 2
GRID_W = 64
CONV_A_WIDTH = 31
CONV_A_PAD = (CONV_A_WIDTH - 1) // 2
SWA_WINDOW = 128
SWA_BLOCK = 128
SWA_HEADS = 4
NA_KH = 8
NA_KW = 16
ROPE_THETA = 10000.0
EPS = 1e-6
NEG_INF = -1e30
LOG2E = math.log2(math.e)
Q_SCALE = HEAD_DIM ** -0.5 * LOG2E

COL_A_U, COL_A_V, COL_A_Z = 0, 256, 512
COL_B_B, COL_B_C, COL_B_X, COL_B_Z = 768, 1024, 1280, 1536
COL_C_Q, COL_C_K, COL_C_V, COL_C_Z = 1792, 2048, 2176, 2304
COL_D_Q, COL_D_K, COL_D_V, COL_D_Z = 2560, 2816, 3072, 3328
D_IN = 3584

P_H, P_ZA = 0, 256
P_BZ, P_G = 512, 768
W_AB = 1024
P_CQ, P_CQS, P_ZC = 0, 256, 512
W_CT = 768
P_CK, P_CV = 0, 128
W_CKV = 256
P_DQ, P_ZD = 0, 256
W_DT = 512
P_DK, P_DV = 0, 256
W_DKV = 512

LANES = 128
SUBLANES = 8
HALO = 16
VMEM_LIMIT = 56 * 1024 * 1024

TM_PROJ = 1024
T_MIX = 1024
T_ATTN = 2048
PROJ_CHUNK = 512
PROJ_ROWS = 256
CONV_ROWS = 256


def _silu(x):
    return x * jax.nn.sigmoid(x)


def _params():
    return pltpu.CompilerParams(
        dimension_semantics=("parallel", "parallel"), vmem_limit_bytes=VMEM_LIMIT)


def _in_proj_kernel(x_ref, g_ref, w32_ref, cos_ref, sin_ref, oab_ref, oct_ref, ockv_ref, odt_ref,
                    odkv_ref, w_ref, *, layer):
    @pl.when((pl.program_id(0) == 0) & (pl.program_id(1) == 0))
    def _():
        for c0 in range(0, D_IN, PROJ_CHUNK):
            w_ref[:, c0:c0 + PROJ_CHUNK] = w32_ref[0, :, c0:c0 + PROJ_CHUNK].astype(w_ref.dtype)

    lane = lax.broadcasted_iota(jnp.int32, (1, LANES), 1)
    first_half = (lane % HEAD_DIM) < HALF
    for r0 in range(0, x_ref.shape[1], PROJ_ROWS):
        _in_proj_rows(x_ref, g_ref, cos_ref, sin_ref, oab_ref, oct_ref, ockv_ref, odt_ref, odkv_ref,
                      w_ref, first_half, r0, layer)


def _in_proj_rows(x_ref, g_ref, cos_ref, sin_ref, oab_ref, oct_ref, ockv_ref, odt_ref, odkv_ref,
                  w_ref, first_half, r0, layer):
    rows = pl.ds(r0, PROJ_ROWS)
    x = x_ref[0, rows, :]
    ms = jnp.mean(x * x, axis=-1, keepdims=True)
    h = (x * lax.rsqrt(ms + EPS) * g_ref[layer:layer + 1, :]).astype(jnp.bfloat16)

    def rope(t):
        partner = jnp.where(first_half, -pltpu.roll(t, LANES - HALF, 1), pltpu.roll(t, HALF, 1))
        return t * cos_ref[rows, :] + partner * sin_ref[rows, :]

    def store(ref, col, t):
        ref[0, rows, col:col + t.shape[1]] = t.astype(ref.dtype)

    def chunk(c0):
        return jnp.dot(h, w_ref[:, c0:c0 + PROJ_CHUNK], preferred_element_type=jnp.float32)

    g = D_GROUP
    acc = chunk(COL_A_U)
    store(oab_ref, P_H, acc[:, :g] * jax.nn.sigmoid(acc[:, g:]))
    acc = chunk(COL_A_Z)
    store(oab_ref, P_ZA, _silu(acc[:, :g]))
    b_gate = acc[:, g:]
    acc = chunk(COL_B_C)
    store(oab_ref, P_G, acc[:, :g] * acc[:, g:])
    acc = chunk(COL_B_Z)
    store(oab_ref, P_BZ, b_gate * _silu(acc[:, :g]))
    for s0 in range(0, g, LANES):
        t = rope(acc[:, g + s0:g + s0 + LANES]) * Q_SCALE
        store(oct_ref, P_CQ + s0, t)
        store(oct_ref, P_CQS + s0, pltpu.roll(t, HEAD_DIM, 1))
    acc = chunk(COL_C_K)
    store(ockv_ref, P_CK, rope(acc[:, :LANES]))
    store(ockv_ref, P_CV, acc[:, LANES:g])
    store(oct_ref, P_ZC, _silu(acc[:, g:]))
    acc = chunk(COL_D_Q)
    store(odt_ref, P_DQ, acc[:, :g] * Q_SCALE)
    store(odkv_ref, P_DK, acc[:, g:])
    acc = chunk(COL_D_V)
    store(odkv_ref, P_DV, acc[:, :g])
    store(odt_ref, P_ZD, _silu(acc[:, g:]))


def _in_proj(x, g, w, cos, sin, layer):
    b, s, d = x.shape
    tm = TM_PROJ
    widths = (W_AB, W_CT, W_CKV, W_DT, W_DKV)
    return pl.pallas_call(
        functools.partial(_in_proj_kernel, layer=layer),
        grid=(b, s // tm),
        in_specs=[
            pl.BlockSpec((1, tm, d), lambda bi, i: (bi, i, 0)),
            pl.BlockSpec(g.shape, lambda bi, i: (0, 0)),
            pl.BlockSpec((1, d, D_IN), lambda bi, i: (layer, 0, 0), pipeline_mode=pl.Buffered(1)),
            pl.BlockSpec((tm, LANES), lambda bi, i: (i, 0)),
            pl.BlockSpec((tm, LANES), lambda bi, i: (i, 0)),
        ],
        out_specs=[pl.BlockSpec((1, tm, wd), lambda bi, i: (bi, i, 0)) for wd in widths],
        out_shape=[jax.ShapeDtypeStruct((b, s, wd), jnp.bfloat16) for wd in widths],
        scratch_shapes=[pltpu.VMEM((d, D_IN), jnp.bfloat16)],
        compiler_params=pltpu.CompilerParams(
            dimension_semantics=("arbitrary", "arbitrary"), vmem_limit_bytes=VMEM_LIMIT),
        name="in_proj",
    )(x, g, w, cos, sin)


def _conv_out_kernel(cur_ref, prev_ref, next_ref, aw_ref, ab_ref, lg_ref, lb_ref, bw_ref,
                     ycd_ref, x_ref, w_ref, g_ref, o_ref, hbuf, gbuf, hph, gph,
                     *, layer, final_norm):
    i = pl.program_id(1)
    n = pl.num_programs(1)
    t = T_MIX
    f32 = jnp.float32
    has_prev = (i > 0).astype(f32)
    has_next = (i < n - 1).astype(f32)
    row = lambda ref, j: ref[layer, j:j + 1, :]
    vec = lambda ref: ref[layer:layer + 1, :]
    col = lambda ref, rows, c: ref[0, rows, c:c + D_GROUP].astype(f32)

    full = slice(None)
    hwin = jnp.concatenate([col(prev_ref, full, P_H) * has_prev, col(cur_ref, full, P_H),
                            col(next_ref, full, P_H) * has_next], axis=0)
    gwin = jnp.concatenate([col(prev_ref, full, P_G) * has_prev, col(cur_ref, full, P_G),
                            col(next_ref, full, P_G) * has_next], axis=0)
    hbuf[...] = hwin
    gbuf[...] = gwin

    nwin = t + 2 * HALO
    nph = nwin - SUBLANES
    for p in range(1, SUBLANES):
        hph[p - 1] = pltpu.roll(hwin, nwin - p, 0)[0:nph]
    gph[0] = pltpu.roll(gwin, nwin - 1, 0)[0:nph]
    gph[1] = pltpu.roll(gwin, nwin - (SUBLANES - 1), 0)[0:nph]

    def tap(off, rows):
        p = off % SUBLANES
        if p == 0:
            return hbuf[off:off + rows, :]
        return hph[p - 1, off - p:off - p + rows, :]

    rc = CONV_ROWS
    for r0 in range(0, t, rc):
        rows = slice(r0, r0 + rc)
        acc = jnp.broadcast_to(vec(ab_ref), (rc, D_GROUP))
        for j in range(CONV_A_WIDTH):
            acc = acc + aw_ref[layer, j:j + 1, :] * tap(HALO - CONV_A_PAD + j + r0, rc)
        mu = jnp.mean(acc, axis=-1, keepdims=True)
        xc = acc - mu
        var = jnp.mean(xc * xc, axis=-1, keepdims=True)
        hn = xc * lax.rsqrt(var + EPS) * vec(lg_ref) + vec(lb_ref)
        ya = _silu(hn) * col(cur_ref, rows, P_ZA)
        gm = HALO + r0 - SUBLANES
        conv = (row(bw_ref, 0) * gph[1, gm:gm + rc, :]
                + row(bw_ref, 1) * gbuf[HALO + r0:HALO + r0 + rc, :]
                + row(bw_ref, 2) * gph[0, HALO + r0:HALO + r0 + rc, :])
        yb = col(cur_ref, rows, P_BZ) * conv
        yab = jnp.concatenate([ya.astype(jnp.bfloat16), yb.astype(jnp.bfloat16)], axis=1)
        xo = x_ref[0, rows, :] + jnp.dot(ycd_ref[0, rows, :], w_ref[0, 2 * D_GROUP:4 * D_GROUP, :],
                                         preferred_element_type=f32)
        xo = xo + jnp.dot(yab, w_ref[0, 0:2 * D_GROUP, :], preferred_element_type=f32)
        if final_norm:
            ms = jnp.mean(xo * xo, axis=-1, keepdims=True)
            xo = xo * lax.rsqrt(ms + EPS) * g_ref[...]
        o_ref[0, rows, :] = xo


def _conv_out(proj, ycd, x, aw, ab, lg, lb, bw, w, g, layer, final_norm):
    b, s, d = x.shape
    t = T_MIX
    hb = t // HALO
    nh = s // HALO
    f32 = jnp.float32
    whole = lambda a: pl.BlockSpec(a.shape, lambda bi, i: (0,) * a.ndim)
    return pl.pallas_call(
        functools.partial(_conv_out_kernel, layer=layer, final_norm=final_norm),
        grid=(b, s // t),
        in_specs=[
            pl.BlockSpec((1, t, W_AB), lambda bi, i: (bi, i, 0)),
            pl.BlockSpec((1, HALO, W_AB), lambda bi, i: (bi, jnp.maximum(i * hb - 1, 0), 0)),
            pl.BlockSpec((1, HALO, W_AB), lambda bi, i: (bi, jnp.minimum((i + 1) * hb, nh - 1), 0)),
            whole(aw), whole(ab), whole(lg), whole(lb), whole(bw),
            pl.BlockSpec((1, t, 2 * D_GROUP), lambda bi, i: (bi, i, 0)),
            pl.BlockSpec((1, t, d), lambda bi, i: (bi, i, 0)),
            pl.BlockSpec((1,) + w.shape[1:], lambda bi, i: (layer, 0, 0)),
            pl.BlockSpec((1, d), lambda bi, i: (0, 0)),
        ],
        out_specs=pl.BlockSpec((1, t, d), lambda bi, i: (bi, i, 0)),
        out_shape=jax.ShapeDtypeStruct((b, s, d), f32),
        scratch_shapes=[pltpu.VMEM((t + 2 * HALO, D_GROUP), f32),
                        pltpu.VMEM((t + 2 * HALO, D_GROUP), f32),
                        pltpu.VMEM((SUBLANES - 1, t + 2 * HALO - SUBLANES, D_GROUP), f32),
                        pltpu.VMEM((2, t + 2 * HALO - SUBLANES, D_GROUP), f32)],
        compiler_params=_params(),
        name="conv_out",
    )(proj, proj, proj, aw, ab, lg, lb, bw, ycd, x, w, g)


def _swa_tile(sink_ref, q_ref, qs_ref, z_ref, kv_ref, y_ref, i, layer):
    s_len = kv_ref.shape[1]
    nkeys = 3 * SWA_BLOCK
    f32 = jnp.float32
    lane = lax.broadcasted_iota(jnp.int32, (1, LANES), 1)
    lo = lane < HEAD_DIM
    zero = jnp.zeros((), jnp.bfloat16)
    ones = jnp.ones((nkeys, LANES), jnp.bfloat16)
    for nb in range(T_ATTN // SWA_BLOCK):
        q0 = i * T_ATTN + nb * SWA_BLOCK
        ws = pl.multiple_of(jnp.clip(q0 - SWA_BLOCK, 0, s_len - nkeys), SWA_BLOCK)
        rows = slice(nb * SWA_BLOCK, (nb + 1) * SWA_BLOCK)
        win = pl.ds(ws, nkeys)
        lhs = [
            jnp.where(lo, q_ref[0, rows, 0:LANES], zero),
            jnp.where(lo, qs_ref[0, rows, 0:LANES], zero),
            jnp.where(lo, zero, qs_ref[0, rows, LANES:2 * LANES]),
            jnp.where(lo, zero, q_ref[0, rows, LANES:2 * LANES]),
        ]
        kw = kv_ref[0, win, 0:LANES]
        vw = jnp.concatenate([kv_ref[0, win, LANES:2 * LANES], ones], axis=1)
        qpos = q0 + lax.broadcasted_iota(jnp.int32, (SWA_BLOCK, nkeys), 0)
        kpos = ws + lax.broadcasted_iota(jnp.int32, (SWA_BLOCK, nkeys), 1)
        valid = jnp.abs(kpos - qpos) <= SWA_WINDOW
        num, den = [], []
        for h in range(SWA_HEADS):
            sc = jnp.einsum("qd,kd->qk", lhs[h], kw, preferred_element_type=f32)
            sh = jnp.where(valid, sc, NEG_INF)
            sink = sink_ref[layer, h] * LOG2E
            m = jnp.maximum(jnp.max(sh, axis=-1, keepdims=True), sink)
            e = jnp.exp2(sh - m)
            pv = jnp.dot(e.astype(jnp.bfloat16), vw, preferred_element_type=f32)
            num.append(pv[:, :LANES])
            den.append(pv[:, LANES:] + jnp.exp2(sink - m))
        og0 = jnp.where(lo, num[0], pltpu.roll(num[1], HEAD_DIM, 1)) / jnp.where(lo, den[0], den[1])
        og1 = jnp.where(lo, pltpu.roll(num[2], HEAD_DIM, 1), num[3]) / jnp.where(lo, den[2], den[3])
        y = jnp.concatenate([og0, og1], axis=-1) * z_ref[0, rows, :].astype(f32)
        y_ref[rows, 0:D_GROUP] = y.astype(y_ref.dtype)


def _nbr_tile(q_ref, z_ref, k_ref, v_ref, bias_ref, y_ref, i):
    rows_total = k_ref.shape[1] // GRID_W
    rows_tile = T_ATTN // GRID_W
    nkeys = NA_KH * GRID_W
    f32 = jnp.float32
    lane = lax.broadcasted_iota(jnp.int32, (1, LANES), 1)
    lo = lane < HEAD_DIM
    zero = jnp.zeros((), jnp.bfloat16)
    ones = jnp.ones((nkeys, LANES), jnp.bfloat16)

    for rr in range(rows_tile):
        r = i * rows_tile + rr
        r0 = jnp.clip(r - NA_KH // 2, 0, rows_total - NA_KH)
        d0 = r0 - r + (NA_KH - 1)
        ks = pl.multiple_of(r0 * GRID_W, GRID_W)
        qrow = slice(rr * GRID_W, (rr + 1) * GRID_W)
        outs = []
        for g in range(2):
            cols = slice(g * LANES, (g + 1) * LANES)
            qg = q_ref[0, qrow, cols]
            lhs = jnp.concatenate([jnp.where(lo, qg, zero), jnp.where(lo, zero, qg)], axis=0)
            kw = k_ref[0, pl.ds(ks, nkeys), cols]
            vw = jnp.concatenate([v_ref[0, pl.ds(ks, nkeys), cols], ones], axis=1)
            sc = jnp.einsum("qd,kd->qk", lhs, kw, preferred_element_type=f32)
            bias = jnp.concatenate(
                [jnp.concatenate([bias_ref[0, 2 * g + hh, d0 + 2 * m]
                                  for m in range(NA_KH // 2)], axis=-1)
                 for hh in range(2)], axis=0)
            sc = sc + bias
            m_ = jnp.max(sc, axis=-1, keepdims=True)
            e = jnp.exp2(sc - m_)
            pv = jnp.dot(e.astype(jnp.bfloat16), vw, preferred_element_type=f32)
            pv = jnp.where(jnp.concatenate([lo, lo], axis=1), pv[0:GRID_W], pv[GRID_W:2 * GRID_W])
            outs.append(pv[:, :LANES] / pv[:, LANES:])
        y = jnp.concatenate(outs, axis=-1) * z_ref[0, qrow, :].astype(f32)
        y_ref[qrow, D_GROUP:2 * D_GROUP] = y.astype(y_ref.dtype)


def _attn_kernel(sink_ref, ct_ref, ckv_ref, dt_ref, dkv_ref, bias_ref, o_ref, *, layer):
    i = pl.program_id(1)
    y_ref = o_ref.at[0]
    cols = lambda ref, c: ref.at[:, :, pl.ds(c, D_GROUP)]
    _swa_tile(sink_ref, cols(ct_ref, P_CQ), cols(ct_ref, P_CQS), cols(ct_ref, P_ZC), ckv_ref,
              y_ref, i, layer)
    _nbr_tile(cols(dt_ref, P_DQ), cols(dt_ref, P_ZD), cols(dkv_ref, P_DK), cols(dkv_ref, P_DV),
              bias_ref, y_ref, i)


def _attn(c_tile, c_kv, d_tile, d_kv, sink, bias, layer):
    b, s, _ = c_tile.shape
    t = T_ATTN
    tile = lambda a: pl.BlockSpec((1, t, a.shape[2]), lambda bi, i, sk: (bi, i, 0))
    seq = lambda a: pl.BlockSpec((1, s, a.shape[2]), lambda bi, i, sk: (bi, 0, 0))
    return pl.pallas_call(
        functools.partial(_attn_kernel, layer=layer),
        grid_spec=pltpu.PrefetchScalarGridSpec(
            num_scalar_prefetch=1,
            grid=(b, s // t),
            in_specs=[
                tile(c_tile), seq(c_kv), tile(d_tile), seq(d_kv),
                pl.BlockSpec((1,) + bias.shape[1:], lambda bi, i, sk: (layer, 0, 0, 0, 0)),
            ],
            out_specs=pl.BlockSpec((1, t, 2 * D_GROUP), lambda bi, i, sk: (bi, i, 0)),
        ),
        out_shape=jax.ShapeDtypeStruct((b, s, 2 * D_GROUP), jnp.bfloat16),
        compiler_params=_params(),
        name="attn",
    )(sink, c_tile, c_kv, d_tile, d_kv, bias)


def _rope_tables(s):
    inv_freq = ROPE_THETA ** (-jnp.arange(0, HEAD_DIM, 2, dtype=jnp.float32) / HEAD_DIM)
    inv_freq = jnp.tile(inv_freq, LANES // HALF)
    hi = jnp.arange(0, s, GRID_W, dtype=jnp.float32)[:, None, None] * inv_freq
    lo = jnp.arange(GRID_W, dtype=jnp.float32)[None, :, None] * inv_freq
    ch, sh, cl, sl = jnp.cos(hi), jnp.sin(hi), jnp.cos(lo), jnp.sin(lo)
    cos = ch * cl - sh * sl
    sin = sh * cl + ch * sl
    return cos.reshape(s, LANES), sin.reshape(s, LANES)


def _nbr_bias_table(rpb):
    nrel = 2 * NA_KW - 1
    ndr = 2 * NA_KH - 1
    c = np.arange(GRID_W)
    c0 = np.clip(c - NA_KW // 2, 0, GRID_W - NA_KW)
    col_ok = (c[None, :] >= c0[:, None]) & (c[None, :] < c0[:, None] + NA_KW)
    dc = np.clip(c[None, :] - c[:, None], -(NA_KW - 1), NA_KW - 1) + (NA_KW - 1)
    onehot = (dc[None] == np.arange(nrel)[:, None, None]).astype(np.float32)
    pick = np.zeros((2, nrel, GRID_W, 2, GRID_W), np.float32)
    pick[0, :, :, 0, :] = onehot
    pick[1, :, :, 1, :] = onehot
    pick = pick.reshape(2 * nrel, GRID_W, LANES)
    dr = np.arange(ndr + 1)[:, None] + np.arange(2)[None, :]
    valid = (dr < ndr)[:, None, :, None] & col_ok[None, :, None, :]
    valid = valid.reshape(ndr + 1, GRID_W, LANES)
    ext = jnp.pad(rpb, ((0, 0), (0, 0), (0, 2), (0, 0)))
    pair = jnp.concatenate([ext[:, :, :-1], ext[:, :, 1:]], axis=-1)
    full = jnp.einsum("lhdc,cqj->lhdqj", pair, jnp.asarray(pick),
                      precision=lax.Precision.HIGHEST)
    return jnp.where(jnp.asarray(valid), full * LOG2E, NEG_INF)


def kernel(x, norm_g, w_in, w_out, conv_a_w, conv_a_b, ln_a_g, ln_a_b, conv_b_w, swa_sink,
           na_rpb, final_norm_g):
    depth = norm_g.shape[0]
    s = x.shape[1]
    assert s % max(TM_PROJ, T_MIX, T_ATTN) == 0 and s % GRID_W == 0, x.shape
    assert s // GRID_W >= NA_KH and s >= 3 * SWA_BLOCK, x.shape
    assert w_in.shape[1:] == (x.shape[2], D_IN) and w_out.shape[1:] == (4 * D_GROUP, x.shape[2])
    cos_t, sin_t = _rope_tables(s)
    w_out_b = w_out.astype(jnp.bfloat16)
    bias = _nbr_bias_table(na_rpb)
    fg = final_norm_g.reshape(1, -1)
    for l in range(depth):
        proj_ab, c_tile, c_kv, d_tile, d_kv = _in_proj(x, norm_g, w_in, cos_t, sin_t, l)
        ycd = _attn(c_tile, c_kv, d_tile, d_kv, swa_sink, bias, l)
        x = _conv_out(proj_ab, ycd, x, conv_a_w, conv_a_b, ln_a_g, ln_a_b, conv_b_w,
                      w_out_b, fg, l, final_norm=(l == depth - 1))
    return x
```

```python
import functools
import math

import numpy as np
import jax
import jax.numpy as jnp
from jax import lax
from jax.experimental import pallas as pl
from jax.experimental.pallas import tpu as pltpu

D_GROUP = 256
HEAD_DIM = 64
HALF = HEAD_DIM // 2
GRID_W = 64
CONV_A_WIDTH = 31
CONV_A_PAD = (CONV_A_WIDTH - 1) // 2
SWA_WINDOW = 128
SWA_BLOCK = 128
SWA_HEADS = 4
NA_KH = 8
NA_KW = 16
ROPE_THETA = 10000.0
EPS = 1e-6
NEG_INF = -1e30
LOG2E = math.log2(math.e)
Q_SCALE = HEAD_DIM ** -0.5 * LOG2E

COL_A_U, COL_A_V, COL_A_Z = 0, 256, 512
COL_B_B, COL_B_C, COL_B_X, COL_B_Z = 768, 1024, 1280, 1536
COL_C_Q, COL_C_K, COL_C_V, COL_C_Z = 1792, 2048, 2176, 2304
COL_D_Q, COL_D_K, COL_D_V, COL_D_Z = 2560, 2816, 3072, 3328
D_IN = 3584

P_H, P_ZA = 0, 256
P_BZ, P_G = 512, 768
W_AB = 1024
P_CQ, P_CQS, P_ZC = 0, 256, 512
W_CT = 768
P_CK, P_CV = 0, 128
W_CKV = 256
P_DQ, P_ZD = 0, 256
W_DT = 512
P_DK, P_DV = 0, 256
W_DKV = 512

LANES = 128
SUBLANES = 8
HALO = 16
VMEM_LIMIT = 56 * 1024 * 1024

TM_PROJ = 1024
T_MIX = 1024
T_ATTN = 2048
PROJ_CHUNK = 512
PROJ_ROWS = 256
CONV_ROWS = 256


def _silu(x):
    return x * jax.nn.sigmoid(x)


def _params():
    return pltpu.CompilerParams(
        dimension_semantics=("parallel", "parallel"), vmem_limit_bytes=VMEM_LIMIT)


def _in_proj_kernel(x_ref, g_ref, w32_ref, cos_ref, sin_ref, oab_ref, oct_ref, ockv_ref, odt_ref,
                    odkv_ref, w_ref, *, layer):
    @pl.when((pl.program_id(0) == 0) & (pl.program_id(1) == 0))
    def _():
        for c0 in range(0, D_IN, PROJ_CHUNK):
            w_ref[:, c0:c0 + PROJ_CHUNK] = w32_ref[0, :, c0:c0 + PROJ_CHUNK].astype(w_ref.dtype)

    lane = lax.broadcasted_iota(jnp.int32, (1, LANES), 1)
    first_half = (lane % HEAD_DIM) < HALF
    for r0 in range(0, x_ref.shape[1], PROJ_ROWS):
        _in_proj_rows(x_ref, g_ref, cos_ref, sin_ref, oab_ref, oct_ref, ockv_ref, odt_ref, odkv_ref,
                      w_ref, first_half, r0, layer)


def _in_proj_rows(x_ref, g_ref, cos_ref, sin_ref, oab_ref, oct_ref, ockv_ref, odt_ref, odkv_ref,
                  w_ref, first_half, r0, layer):
    rows = pl.ds(r0, PROJ_ROWS)
    x = x_ref[0, rows, :]
    ms = jnp.mean(x * x, axis=-1, keepdims=True)
    h = (x * lax.rsqrt(ms + EPS) * g_ref[layer:layer + 1, :]).astype(jnp.bfloat16)

    def rope(t):
        partner = jnp.where(first_half, -pltpu.roll(t, LANES - HALF, 1), pltpu.roll(t, HALF, 1))
        return t * cos_ref[rows, :] + partner * sin_ref[rows, :]

    def store(ref, col, t):
        ref[0, rows, col:col + t.shape[1]] = t.astype(ref.dtype)

    def chunk(c0):
        return jnp.dot(h, w_ref[:, c0:c0 + PROJ_CHUNK], preferred_element_type=jnp.float32)

    g = D_GROUP
    acc = chunk(COL_A_U)
    store(oab_ref, P_H, acc[:, :g] * jax.nn.sigmoid(acc[:, g:]))
    acc = chunk(COL_A_Z)
    store(oab_ref, P_ZA, _silu(acc[:, :g]))
    b_gate = acc[:, g:]
    acc = chunk(COL_B_C)
    store(oab_ref, P_G, acc[:, :g] * acc[:, g:])
    acc = chunk(COL_B_Z)
    store(oab_ref, P_BZ, b_gate * _silu(acc[:, :g]))
    for s0 in range(0, g, LANES):
        t = rope(acc[:, g + s0:g + s0 + LANES]) * Q_SCALE
        store(oct_ref, P_CQ + s0, t)
        store(oct_ref, P_CQS + s0, pltpu.roll(t, HEAD_DIM, 1))
    acc = chunk(COL_C_K)
    store(ockv_ref, P_CK, rope(acc[:, :LANES]))
    store(ockv_ref, P_CV, acc[:, LANES:g])
    store(oct_ref, P_ZC, _silu(acc[:, g:]))
    acc = chunk(COL_D_Q)
    store(odt_ref, P_DQ, acc[:, :g] * Q_SCALE)
    store(odkv_ref, P_DK, acc[:, g:])
    acc = chunk(COL_D_V)
    store(odkv_ref, P_DV, acc[:, :g])
    store(odt_ref, P_ZD, _silu(acc[:, g:]))


def _in_proj(x, g, w, cos, sin, layer):
    b, s, d = x.shape
    tm = TM_PROJ
    widths = (W_AB, W_CT, W_CKV, W_DT, W_DKV)
    return pl.pallas_call(
        functools.partial(_in_proj_kernel, layer=layer),
        grid=(b, s // tm),
        in_specs=[
            pl.BlockSpec((1, tm, d), lambda bi, i: (bi, i, 0)),
            pl.BlockSpec(g.shape, lambda bi, i: (0, 0)),
            pl.BlockSpec((1, d, D_IN), lambda bi, i: (layer, 0, 0), pipeline_mode=pl.Buffered(1)),
            pl.BlockSpec((tm, LANES), lambda bi, i: (i, 0)),
            pl.BlockSpec((tm, LANES), lambda bi, i: (i, 0)),
        ],
        out_specs=[pl.BlockSpec((1, tm, wd), lambda bi, i: (bi, i, 0)) for wd in widths],
        out_shape=[jax.ShapeDtypeStruct((b, s, wd), jnp.bfloat16) for wd in widths],
        scratch_shapes=[pltpu.VMEM((d, D_IN), jnp.bfloat16)],
        compiler_params=pltpu.CompilerParams(
            dimension_semantics=("arbitrary", "arbitrary"), vmem_limit_bytes=VMEM_LIMIT),
        name="in_proj",
    )(x, g, w, cos, sin)


def _conv_out_kernel(cur_ref, prev_ref, next_ref, aw_ref, ab_ref, lg_ref, lb_ref, bw_ref,
                     ycd_ref, x_ref, w_ref, g_ref, o_ref, hbuf, gbuf, hph, gph,
                     *, layer, final_norm):
    i = pl.program_id(1)
    n = pl.num_programs(1)
    t = T_MIX
    f32 = jnp.float32
    has_prev = (i > 0).astype(f32)
    has_next = (i < n - 1).astype(f32)
    row = lambda ref, j: ref[layer, j:j + 1, :]
    vec = lambda ref: ref[layer:layer + 1, :]
    col = lambda ref, rows, c: ref[0, rows, c:c + D_GROUP].astype(f32)

    full = slice(None)
    hwin = jnp.concatenate([col(prev_ref, full, P_H) * has_prev, col(cur_ref, full, P_H),
                            col(next_ref, full, P_H) * has_next], axis=0)
    gwin = jnp.concatenate([col(prev_ref, full, P_G) * has_prev, col(cur_ref, full, P_G),
                            col(next_ref, full, P_G) * has_next], axis=0)
    hbuf[...] = hwin
    gbuf[...] = gwin

    nwin = t + 2 * HALO
    nph = nwin - SUBLANES
    for p in range(1, SUBLANES):
        hph[p - 1] = pltpu.roll(hwin, nwin - p, 0)[0:nph]
    gph[0] = pltpu.roll(gwin, nwin - 1, 0)[0:nph]
    gph[1] = pltpu.roll(gwin, nwin - (SUBLANES - 1), 0)[0:nph]

    def tap(off, rows):
        p = off % SUBLANES
        if p == 0:
            return hbuf[off:off + rows, :]
        return hph[p - 1, off - p:off - p + rows, :]

    rc = CONV_ROWS
    for r0 in range(0, t, rc):
        rows = slice(r0, r0 + rc)
        acc = jnp.broadcast_to(vec(ab_ref), (rc, D_GROUP))
        for j in range(CONV_A_WIDTH):
            acc = acc + aw_ref[layer, j:j + 1, :] * tap(HALO - CONV_A_PAD + j + r0, rc)
        mu = jnp.mean(acc, axis=-1, keepdims=True)
        xc = acc - mu
        var = jnp.mean(xc * xc, axis=-1, keepdims=True)
        hn = xc * lax.rsqrt(var + EPS) * vec(lg_ref) + vec(lb_ref)
        ya = _silu(hn) * col(cur_ref, rows, P_ZA)
        gm = HALO + r0 - SUBLANES
        conv = (row(bw_ref, 0) * gph[1, gm:gm + rc, :]
                + row(bw_ref, 1) * gbuf[HALO + r0:HALO + r0 + rc, :]
                + row(bw_ref, 2) * gph[0, HALO + r0:HALO + r0 + rc, :])
        yb = col(cur_ref, rows, P_BZ) * conv
        yab = jnp.concatenate([ya.astype(jnp.bfloat16), yb.astype(jnp.bfloat16)], axis=1)
        xo = x_ref[0, rows, :] + jnp.dot(ycd_ref[0, rows, :], w_ref[0, 2 * D_GROUP:4 * D_GROUP, :],
                                         preferred_element_type=f32)
        xo = xo + jnp.dot(yab, w_ref[0, 0:2 * D_GROUP, :], preferred_element_type=f32)
        if final_norm:
            ms = jnp.mean(xo * xo, axis=-1, keepdims=True)
            xo = xo * lax.rsqrt(ms + EPS) * g_ref[...]
        o_ref[0, rows, :] = xo


def _conv_out(proj, ycd, x, aw, ab, lg, lb, bw, w, g, layer, final_norm):
    b, s, d = x.shape
    t = T_MIX
    hb = t // HALO
    nh = s // HALO
    f32 = jnp.float32
    whole = lambda a: pl.BlockSpec(a.shape, lambda bi, i: (0,) * a.ndim)
    return pl.pallas_call(
        functools.partial(_conv_out_kernel, layer=layer, final_norm=final_norm),
        grid=(b, s // t),
        in_specs=[
            pl.BlockSpec((1, t, W_AB), lambda bi, i: (bi, i, 0)),
            pl.BlockSpec((1, HALO, W_AB), lambda bi, i: (bi, jnp.maximum(i * hb - 1, 0), 0)),
            pl.BlockSpec((1, HALO, W_AB), lambda bi, i: (bi, jnp.minimum((i + 1) * hb, nh - 1), 0)),
            whole(aw), whole(ab), whole(lg), whole(lb), whole(bw),
            pl.BlockSpec((1, t, 2 * D_GROUP), lambda bi, i: (bi, i, 0)),
            pl.BlockSpec((1, t, d), lambda bi, i: (bi, i, 0)),
            pl.BlockSpec((1,) + w.shape[1:], lambda bi, i: (layer, 0, 0)),
            pl.BlockSpec((1, d), lambda bi, i: (0, 0)),
        ],
        out_specs=pl.BlockSpec((1, t, d), lambda bi, i: (bi, i, 0)),
        out_shape=jax.ShapeDtypeStruct((b, s, d), f32),
        scratch_shapes=[pltpu.VMEM((t + 2 * HALO, D_GROUP), f32),
                        pltpu.VMEM((t + 2 * HALO, D_GROUP), f32),
                        pltpu.VMEM((SUBLANES - 1, t + 2 * HALO - SUBLANES, D_GROUP), f32),
                        pltpu.VMEM((2, t + 2 * HALO - SUBLANES, D_GROUP), f32)],
        compiler_params=_params(),
        name="conv_out",
    )(proj, proj, proj, aw, ab, lg, lb, bw, ycd, x, w, g)


def _swa_tile(sink_ref, q_ref, qs_ref, z_ref, kv_ref, y_ref, i, layer):
    s_len = kv_ref.shape[1]
    nkeys = 3 * SWA_BLOCK
    f32 = jnp.float32
    lane = lax.broadcasted_iota(jnp.int32, (1, LANES), 1)
    lo = lane < HEAD_DIM
    zero = jnp.zeros((), jnp.bfloat16)
    ones = jnp.ones((nkeys, LANES), jnp.bfloat16)
    for nb in range(T_ATTN // SWA_BLOCK):
        q0 = i * T_ATTN + nb * SWA_BLOCK
        ws = pl.multiple_of(jnp.clip(q0 - SWA_BLOCK, 0, s_len - nkeys), SWA_BLOCK)
        rows = slice(nb * SWA_BLOCK, (nb + 1) * SWA_BLOCK)
        win = pl.ds(ws, nkeys)
        lhs = [
            jnp.where(lo, q_ref[0, rows, 0:LANES], zero),
            jnp.where(lo, qs_ref[0, rows, 0:LANES], zero),
            jnp.where(lo, zero, qs_ref[0, rows, LANES:2 * LANES]),
            jnp.where(lo, zero, q_ref[0, rows, LANES:2 * LANES]),
        ]
        kw = kv_ref[0, win, 0:LANES]
        vw = jnp.concatenate([kv_ref[0, win, LANES:2 * LANES], ones], axis=1)
        qpos = q0 + lax.broadcasted_iota(jnp.int32, (SWA_BLOCK, nkeys), 0)
        kpos = ws + lax.broadcasted_iota(jnp.int32, (SWA_BLOCK, nkeys), 1)
        valid = jnp.abs(kpos - qpos) <= SWA_WINDOW
        num, den = [], []
        for h in range(SWA_HEADS):
            sc = jnp.einsum("qd,kd->qk", lhs[h], kw, preferred_element_type=f32)
            sh = jnp.where(valid, sc, NEG_INF)
            sink = sink_ref[layer, h] * LOG2E
            m = jnp.maximum(jnp.max(sh, axis=-1, keepdims=True), sink)
            e = jnp.exp2(sh - m)
            pv = jnp.dot(e.astype(jnp.bfloat16), vw, preferred_element_type=f32)
            num.append(pv[:, :LANES])
            den.append(pv[:, LANES:] + jnp.exp2(sink - m))
        og0 = jnp.where(lo, num[0], pltpu.roll(num[1], HEAD_DIM, 1)) / jnp.where(lo, den[0], den[1])
        og1 = jnp.where(lo, pltpu.roll(num[2], HEAD_DIM, 1), num[3]) / jnp.where(lo, den[2], den[3])
        y = jnp.concatenate([og0, og1], axis=-1) * z_ref[0, rows, :].astype(f32)
        y_ref[rows, 0:D_GROUP] = y.astype(y_ref.dtype)


def _nbr_tile(q_ref, z_ref, k_ref, v_ref, bias_ref, y_ref, i):
    rows_total = k_ref.shape[1] // GRID_W
    rows_tile = T_ATTN // GRID_W
    nkeys = NA_KH * GRID_W
    f32 = jnp.float32
    lane = lax.broadcasted_iota(jnp.int32, (1, LANES), 1)
    lo = lane < HEAD_DIM
    zero = jnp.zeros((), jnp.bfloat16)
    ones = jnp.ones((nkeys, LANES), jnp.bfloat16)

    for rr in range(rows_tile):
        r = i * rows_tile + rr
        r0 = jnp.clip(r - NA_KH // 2, 0, rows_total - NA_KH)
        d0 = r0 - r + (NA_KH - 1)
        ks = pl.multiple_of(r0 * GRID_W, GRID_W)
        qrow = slice(rr * GRID_W, (rr + 1) * GRID_W)
        outs = []
        for g in range(2):
            cols = slice(g * LANES, (g + 1) * LANES)
            qg = q_ref[0, qrow, cols]
            lhs = jnp.concatenate([jnp.where(lo, qg, zero), jnp.where(lo, zero, qg)], axis=0)
            kw = k_ref[0, pl.ds(ks, nkeys), cols]
            vw = jnp.concatenate([v_ref[0, pl.ds(ks, nkeys), cols], ones], axis=1)
            sc = jnp.einsum("qd,kd->qk", lhs, kw, preferred_element_type=f32)
            bias = jnp.concatenate(
                [jnp.concatenate([bias_ref[0, 2 * g + hh, d0 + 2 * m]
                                  for m in range(NA_KH // 2)], axis=-1)
                 for hh in range(2)], axis=0)
            sc = sc + bias
            m_ = jnp.max(sc, axis=-1, keepdims=True)
            e = jnp.exp2(sc - m_)
            pv = jnp.dot(e.astype(jnp.bfloat16), vw, preferred_element_type=f32)
            pv = jnp.where(jnp.concatenate([lo, lo], axis=1), pv[0:GRID_W], pv[GRID_W:2 * GRID_W])
            outs.append(pv[:, :LANES] / pv[:, LANES:])
        y = jnp.concatenate(outs, axis=-1) * z_ref[0, qrow, :].astype(f32)
        y_ref[qrow, D_GROUP:2 * D_GROUP] = y.astype(y_ref.dtype)


def _attn_kernel(sink_ref, ct_ref, ckv_ref, dt_ref, dkv_ref, bias_ref, o_ref, *, layer):
    i = pl.program_id(1)
    y_ref = o_ref.at[0]
    cols = lambda ref, c: ref.at[:, :, pl.ds(c, D_GROUP)]
    _swa_tile(sink_ref, cols(ct_ref, P_CQ), cols(ct_ref, P_CQS), cols(ct_ref, P_ZC), ckv_ref,
              y_ref, i, layer)
    _nbr_tile(cols(dt_ref, P_DQ), cols(dt_ref, P_ZD), cols(dkv_ref, P_DK), cols(dkv_ref, P_DV),
              bias_ref, y_ref, i)


def _attn(c_tile, c_kv, d_tile, d_kv, sink, bias, layer):
    b, s, _ = c_tile.shape
    t = T_ATTN
    tile = lambda a: pl.BlockSpec((1, t, a.shape[2]), lambda bi, i, sk: (bi, i, 0))
    seq = lambda a: pl.BlockSpec((1, s, a.shape[2]), lambda bi, i, sk: (bi, 0, 0))
    return pl.pallas_call(
        functools.partial(_attn_kernel, layer=layer),
        grid_spec=pltpu.PrefetchScalarGridSpec(
            num_scalar_prefetch=1,
            grid=(b, s // t),
            in_specs=[
                tile(c_tile), seq(c_kv), tile(d_tile), seq(d_kv),
                pl.BlockSpec((1,) + bias.shape[1:], lambda bi, i, sk: (layer, 0, 0, 0, 0)),
            ],
            out_specs=pl.BlockSpec((1, t, 2 * D_GROUP), lambda bi, i, sk: (bi, i, 0)),
        ),
        out_shape=jax.ShapeDtypeStruct((b, s, 2 * D_GROUP), jnp.bfloat16),
        compiler_params=_params(),
        name="attn",
    )(sink, c_tile, c_kv, d_tile, d_kv, bias)


def _rope_tables(s):
    inv_freq = ROPE_THETA ** (-jnp.arange(0, HEAD_DIM, 2, dtype=jnp.float32) / HEAD_DIM)
    inv_freq = jnp.tile(inv_freq, LANES // HALF)
    hi = jnp.arange(0, s, GRID_W, dtype=jnp.float32)[:, None, None] * inv_freq
    lo = jnp.arange(GRID_W, dtype=jnp.float32)[None, :, None] * inv_freq
    ch, sh, cl, sl = jnp.cos(hi), jnp.sin(hi), jnp.cos(lo), jnp.sin(lo)
    cos = ch * cl - sh * sl
    sin = sh * cl + ch * sl
    return cos.reshape(s, LANES), sin.reshape(s, LANES)


def _nbr_bias_table(rpb):
    nrel = 2 * NA_KW - 1
    ndr = 2 * NA_KH - 1
    c = np.arange(GRID_W)
    c0 = np.clip(c - NA_KW // 2, 0, GRID_W - NA_KW)
    col_ok = (c[None, :] >= c0[:, None]) & (c[None, :] < c0[:, None] + NA_KW)
    dc = np.clip(c[None, :] - c[:, None], -(NA_KW - 1), NA_KW - 1) + (NA_KW - 1)
    onehot = (dc[None] == np.arange(nrel)[:, None, None]).astype(np.float32)
    pick = np.zeros((2, nrel, GRID_W, 2, GRID_W), np.float32)
    pick[0, :, :, 0, :] = onehot
    pick[1, :, :, 1, :] = onehot
    pick = pick.reshape(2 * nrel, GRID_W, LANES)
    col_mask = np.where(np.tile(col_ok, (1, 2)), 0.0, NEG_INF)[None]
    half_mask = np.zeros((2, GRID_W, 2, GRID_W), np.float32)
    half_mask[0, :, 0, :] = NEG_INF
    half_mask[1, :, 1, :] = NEG_INF
    rows = np.concatenate([pick, col_mask, half_mask.reshape(2, GRID_W, LANES)]).astype(np.float32)
    dr = np.arange(ndr + 1)[:, None] + np.arange(2)[None, :]
    coef = np.concatenate([np.ones((ndr + 1, 1)), dr >= ndr], axis=1).astype(np.float32)
    ext = jnp.pad(rpb * LOG2E, ((0, 0), (0, 0), (0, 2), (0, 0)))
    lhs = jnp.concatenate([ext[:, :, :-1], ext[:, :, 1:],
                           jnp.broadcast_to(coef, ext.shape[:2] + coef.shape)], axis=-1)
    return jnp.einsum("lhdc,cqj->lhdqj", lhs, jnp.asarray(rows),
                      precision=lax.Precision.HIGHEST)


def kernel(x, norm_g, w_in, w_out, conv_a_w, conv_a_b, ln_a_g, ln_a_b, conv_b_w, swa_sink,
           na_rpb, final_norm_g):
    depth = norm_g.shape[0]
    s = x.shape[1]
    assert s % max(TM_PROJ, T_MIX, T_ATTN) == 0 and s % GRID_W == 0, x.shape
    assert s // GRID_W >= NA_KH and s >= 3 * SWA_BLOCK, x.shape
    assert w_in.shape[1:] == (x.shape[2], D_IN) and w_out.shape[1:] == (4 * D_GROUP, x.shape[2])
    cos_t, sin_t = _rope_tables(s)
    w_out_b = w_out.astype(jnp.bfloat16)
    bias = _nbr_bias_table(na_rpb)
    fg = final_norm_g.reshape(1, -1)
    for l in range(depth):
        proj_ab, c_tile, c_kv, d_tile, d_kv = _in_proj(x, norm_g, w_in, cos_t, sin_t, l)
        ycd = _attn(c_tile, c_kv, d_tile, d_kv, swa_sink, bias, l)
        x = _conv_out(proj_ab, ycd, x, conv_a_w, conv_a_b, ln_a_g, ln_a_b, conv_b_w,
                      w_out_b, fg, l, final_norm=(l == depth - 1))
    return x
```

```python
import functools
import math

import numpy as np
import jax
import jax.numpy as jnp
from jax import lax
from jax.experimental import pallas as pl
from jax.experimental.pallas import tpu as pltpu

D_GROUP = 256
HEAD_DIM = 64
HALF = HEAD_DIM // 2
GRID_W = 64
CONV_A_WIDTH = 31
CONV_A_PAD = (CONV_A_WIDTH - 1) // 2
SWA_WINDOW = 128
SWA_BLOCK = 128
SWA_HEADS = 4
NA_KH = 8
NA_KW = 16
ROPE_THETA = 10000.0
EPS = 1e-6
NEG_INF = -1e30
LOG2E = math.log2(math.e)
Q_SCALE = HEAD_DIM ** -0.5 * LOG2E

COL_A_U, COL_A_V, COL_A_Z = 0, 256, 512
COL_B_B, COL_B_C, COL_B_X, COL_B_Z = 768, 1024, 1280, 1536
COL_C_Q, COL_C_K, COL_C_V, COL_C_Z = 1792, 2048, 2176, 2304
COL_D_Q, COL_D_K, COL_D_V, COL_D_Z = 2560, 2816, 3072, 3328
D_IN = 3584

P_H, P_ZA = 0, 256
P_BZ, P_G = 512, 768
W_AB = 1024
P_CQ, P_CQS, P_ZC = 0, 256, 512
W_CT = 768
P_CK, P_CV = 0, 128
W_CKV = 256
P_DQ, P_ZD = 0, 256
W_DT = 512
P_DK, P_DV = 0, 256
W_DKV = 512

LANES = 128
SUBLANES = 8
HALO = 16
VMEM_LIMIT = 56 * 1024 * 1024

TM_PROJ = 1024
T_MIX = 1024
T_ATTN = 2048
PROJ_CHUNK = 512
PROJ_ROWS = 256
CONV_ROWS = 256


def _silu(x):
    return x * jax.nn.sigmoid(x)


def _params():
    return pltpu.CompilerParams(
        dimension_semantics=("parallel", "parallel"), vmem_limit_bytes=VMEM_LIMIT)


def _in_proj_kernel(x_ref, g_ref, w_hbm, cos_ref, sin_ref, oab_ref, oct_ref, ockv_ref, odt_ref,
                    odkv_ref, w32_ref, w_ref, sem, *, layer):
    lane = lax.broadcasted_iota(jnp.int32, (1, LANES), 1)
    first_half = (lane % HEAD_DIM) < HALF
    refs = (x_ref, g_ref, cos_ref, sin_ref, oab_ref, oct_ref, ockv_ref, odt_ref, odkv_ref, w_ref,
            first_half)
    first = (pl.program_id(0) == 0) & (pl.program_id(1) == 0)

    def w_copy(c0):
        cols = pl.ds(c0, PROJ_CHUNK)
        return pltpu.make_async_copy(w_hbm.at[layer, :, cols], w32_ref.at[:, cols],
                                     sem.at[c0 // PROJ_CHUNK])

    def land(c0):
        w_copy(c0).wait()
        w_ref[:, c0:c0 + PROJ_CHUNK] = w32_ref[:, c0:c0 + PROJ_CHUNK].astype(w_ref.dtype)

    @pl.when(first)
    def _():
        for c0 in range(0, D_IN, PROJ_CHUNK):
            w_copy(c0).start()
        _in_proj_rows(*refs, 0, x_ref.shape[1], layer, land)

    @pl.when(jnp.logical_not(first))
    def _():
        for r0 in range(0, x_ref.shape[1], PROJ_ROWS):
            _in_proj_rows(*refs, r0, PROJ_ROWS, layer, None)


def _in_proj_rows(x_ref, g_ref, cos_ref, sin_ref, oab_ref, oct_ref, ockv_ref, odt_ref, odkv_ref,
                  w_ref, first_half, r0, nrows, layer, before_chunk):
    rows = pl.ds(r0, nrows)
    x = x_ref[0, rows, :]
    ms = jnp.mean(x * x, axis=-1, keepdims=True)
    h = (x * lax.rsqrt(ms + EPS) * g_ref[layer:layer + 1, :]).astype(jnp.bfloat16)

    def rope(t):
        partner = jnp.where(first_half, -pltpu.roll(t, LANES - HALF, 1), pltpu.roll(t, HALF, 1))
        return t * cos_ref[rows, :] + partner * sin_ref[rows, :]

    def store(ref, col, t):
        ref[0, rows, col:col + t.shape[1]] = t.astype(ref.dtype)

    def chunk(c0):
        if before_chunk is not None:
            before_chunk(c0)
        return jnp.dot(h, w_ref[:, c0:c0 + PROJ_CHUNK], preferred_element_type=jnp.float32)

    g = D_GROUP
    acc = chunk(COL_A_U)
    store(oab_ref, P_H, acc[:, :g] * jax.nn.sigmoid(acc[:, g:]))
    acc = chunk(COL_A_Z)
    store(oab_ref, P_ZA, _silu(acc[:, :g]))
    b_gate = acc[:, g:]
    acc = chunk(COL_B_C)
    store(oab_ref, P_G, acc[:, :g] * acc[:, g:])
    acc = chunk(COL_B_Z)
    store(oab_ref, P_BZ, b_gate * _silu(acc[:, :g]))
    for s0 in range(0, g, LANES):
        t = rope(acc[:, g + s0:g + s0 + LANES]) * Q_SCALE
        store(oct_ref, P_CQ + s0, t)
        store(oct_ref, P_CQS + s0, pltpu.roll(t, HEAD_DIM, 1))
    acc = chunk(COL_C_K)
    store(ockv_ref, P_CK, rope(acc[:, :LANES]))
    store(ockv_ref, P_CV, acc[:, LANES:g])
    store(oct_ref, P_ZC, _silu(acc[:, g:]))
    acc = chunk(COL_D_Q)
    store(odt_ref, P_DQ, acc[:, :g] * Q_SCALE)
    store(odkv_ref, P_DK, acc[:, g:])
    acc = chunk(COL_D_V)
    store(odkv_ref, P_DV, acc[:, :g])
    store(odt_ref, P_ZD, _silu(acc[:, g:]))


def _in_proj(x, g, w, cos, sin, layer):
    b, s, d = x.shape
    tm = TM_PROJ
    widths = (W_AB, W_CT, W_CKV, W_DT, W_DKV)
    return pl.pallas_call(
        functools.partial(_in_proj_kernel, layer=layer),
        grid=(b, s // tm),
        in_specs=[
            pl.BlockSpec((1, tm, d), lambda bi, i: (bi, i, 0)),
            pl.BlockSpec(g.shape, lambda bi, i: (0, 0)),
            pl.BlockSpec(memory_space=pl.ANY),
            pl.BlockSpec((tm, LANES), lambda bi, i: (i, 0)),
            pl.BlockSpec((tm, LANES), lambda bi, i: (i, 0)),
        ],
        out_specs=[pl.BlockSpec((1, tm, wd), lambda bi, i: (bi, i, 0)) for wd in widths],
        out_shape=[jax.ShapeDtypeStruct((b, s, wd), jnp.bfloat16) for wd in widths],
        scratch_shapes=[pltpu.VMEM((d, D_IN), jnp.float32), pltpu.VMEM((d, D_IN), jnp.bfloat16),
                        pltpu.SemaphoreType.DMA((D_IN // PROJ_CHUNK,))],
        compiler_params=pltpu.CompilerParams(
            dimension_semantics=("arbitrary", "arbitrary"), vmem_limit_bytes=VMEM_LIMIT),
        name="in_proj",
    )(x, g, w, cos, sin)


def _conv_out_kernel(cur_ref, prev_ref, next_ref, aw_ref, ab_ref, lg_ref, lb_ref, bw_ref,
                     ycd_ref, x_ref, w_ref, g_ref, o_ref, hbuf, gbuf, hph, gph,
                     *, layer, final_norm):
    i = pl.program_id(1)
    n = pl.num_programs(1)
    t = T_MIX
    f32 = jnp.float32
    has_prev = (i > 0).astype(f32)
    has_next = (i < n - 1).astype(f32)
    row = lambda ref, j: ref[layer, j:j + 1, :]
    vec = lambda ref: ref[layer:layer + 1, :]
    col = lambda ref, rows, c: ref[0, rows, c:c + D_GROUP].astype(f32)

    full = slice(None)
    hwin = jnp.concatenate([col(prev_ref, full, P_H) * has_prev, col(cur_ref, full, P_H),
                            col(next_ref, full, P_H) * has_next], axis=0)
    gwin = jnp.concatenate([col(prev_ref, full, P_G) * has_prev, col(cur_ref, full, P_G),
                            col(next_ref, full, P_G) * has_next], axis=0)
    hbuf[...] = hwin
    gbuf[...] = gwin

    nwin = t + 2 * HALO
    nph = nwin - SUBLANES
    for p in range(1, SUBLANES):
        hph[p - 1] = pltpu.roll(hwin, nwin - p, 0)[0:nph]
    gph[0] = pltpu.roll(gwin, nwin - 1, 0)[0:nph]
    gph[1] = pltpu.roll(gwin, nwin - (SUBLANES - 1), 0)[0:nph]

    def tap(off, rows):
        p = off % SUBLANES
        if p == 0:
            return hbuf[off:off + rows, :]
        return hph[p - 1, off - p:off - p + rows, :]

    rc = CONV_ROWS
    for r0 in range(0, t, rc):
        rows = slice(r0, r0 + rc)
        acc = jnp.broadcast_to(vec(ab_ref), (rc, D_GROUP))
        for j in range(CONV_A_WIDTH):
            acc = acc + aw_ref[layer, j:j + 1, :] * tap(HALO - CONV_A_PAD + j + r0, rc)
        mu = jnp.mean(acc, axis=-1, keepdims=True)
        xc = acc - mu
        var = jnp.mean(xc * xc, axis=-1, keepdims=True)
        hn = xc * lax.rsqrt(var + EPS) * vec(lg_ref) + vec(lb_ref)
        ya = _silu(hn) * col(cur_ref, rows, P_ZA)
        gm = HALO + r0 - SUBLANES
        conv = (row(bw_ref, 0) * gph[1, gm:gm + rc, :]
                + row(bw_ref, 1) * gbuf[HALO + r0:HALO + r0 + rc, :]
                + row(bw_ref, 2) * gph[0, HALO + r0:HALO + r0 + rc, :])
        yb = col(cur_ref, rows, P_BZ) * conv
        yab = jnp.concatenate([ya.astype(jnp.bfloat16), yb.astype(jnp.bfloat16)], axis=1)
        xo = x_ref[0, rows, :] + jnp.dot(ycd_ref[0, rows, :], w_ref[0, 2 * D_GROUP:4 * D_GROUP, :],
                                         preferred_element_type=f32)
        xo = xo + jnp.dot(yab, w_ref[0, 0:2 * D_GROUP, :], preferred_element_type=f32)
        if final_norm:
            ms = jnp.mean(xo * xo, axis=-1, keepdims=True)
            xo = xo * lax.rsqrt(ms + EPS) * g_ref[...]
        o_ref[0, rows, :] = xo


def _conv_out(proj, ycd, x, aw, ab, lg, lb, bw, w, g, layer, final_norm):
    b, s, d = x.shape
    t = T_MIX
    hb = t // HALO
    nh = s // HALO
    f32 = jnp.float32
    whole = lambda a: pl.BlockSpec(a.shape, lambda bi, i: (0,) * a.ndim)
    return pl.pallas_call(
        functools.partial(_conv_out_kernel, layer=layer, final_norm=final_norm),
        grid=(b, s // t),
        in_specs=[
            pl.BlockSpec((1, t, W_AB), lambda bi, i: (bi, i, 0)),
            pl.BlockSpec((1, HALO, W_AB), lambda bi, i: (bi, jnp.maximum(i * hb - 1, 0), 0)),
            pl.BlockSpec((1, HALO, W_AB), lambda bi, i: (bi, jnp.minimum((i + 1) * hb, nh - 1), 0)),
            whole(aw), whole(ab), whole(lg), whole(lb), whole(bw),
            pl.BlockSpec((1, t, 2 * D_GROUP), lambda bi, i: (bi, i, 0)),
            pl.BlockSpec((1, t, d), lambda bi, i: (bi, i, 0)),
            pl.BlockSpec((1,) + w.shape[1:], lambda bi, i: (layer, 0, 0)),
            pl.BlockSpec((1, d), lambda bi, i: (0, 0)),
        ],
        out_specs=pl.BlockSpec((1, t, d), lambda bi, i: (bi, i, 0)),
        out_shape=jax.ShapeDtypeStruct((b, s, d), f32),
        scratch_shapes=[pltpu.VMEM((t + 2 * HALO, D_GROUP), f32),
                        pltpu.VMEM((t + 2 * HALO, D_GROUP), f32),
                        pltpu.VMEM((SUBLANES - 1, t + 2 * HALO - SUBLANES, D_GROUP), f32),
                        pltpu.VMEM((2, t + 2 * HALO - SUBLANES, D_GROUP), f32)],
        compiler_params=_params(),
        name="conv_out",
    )(proj, proj, proj, aw, ab, lg, lb, bw, ycd, x, w, g)


def _swa_tile(sink_ref, q_ref, qs_ref, z_ref, kv_ref, y_ref, i, layer):
    s_len = kv_ref.shape[1]
    nkeys = 3 * SWA_BLOCK
    f32 = jnp.float32
    lane = lax.broadcasted_iota(jnp.int32, (1, LANES), 1)
    lo = lane < HEAD_DIM
    zero = jnp.zeros((), jnp.bfloat16)
    ones = jnp.ones((nkeys, LANES), jnp.bfloat16)
    for nb in range(T_ATTN // SWA_BLOCK):
        q0 = i * T_ATTN + nb * SWA_BLOCK
        ws = pl.multiple_of(jnp.clip(q0 - SWA_BLOCK, 0, s_len - nkeys), SWA_BLOCK)
        rows = slice(nb * SWA_BLOCK, (nb + 1) * SWA_BLOCK)
        win = pl.ds(ws, nkeys)
        lhs = [
            jnp.where(lo, q_ref[0, rows, 0:LANES], zero),
            jnp.where(lo, qs_ref[0, rows, 0:LANES], zero),
            jnp.where(lo, zero, qs_ref[0, rows, LANES:2 * LANES]),
            jnp.where(lo, zero, q_ref[0, rows, LANES:2 * LANES]),
        ]
        kw = kv_ref[0, win, 0:LANES]
        vw = jnp.concatenate([kv_ref[0, win, LANES:2 * LANES], ones], axis=1)
        qpos = q0 + lax.broadcasted_iota(jnp.int32, (SWA_BLOCK, nkeys), 0)
        kpos = ws + lax.broadcasted_iota(jnp.int32, (SWA_BLOCK, nkeys), 1)
        valid = jnp.abs(kpos - qpos) <= SWA_WINDOW
        num, den = [], []
        for h in range(SWA_HEADS):
            sc = jnp.einsum("qd,kd->qk", lhs[h], kw, preferred_element_type=f32)
            sh = jnp.where(valid, sc, NEG_INF)
            sink = sink_ref[layer, h] * LOG2E
            m = jnp.maximum(jnp.max(sh, axis=-1, keepdims=True), sink)
            e = jnp.exp2(sh - m)
            pv = jnp.dot(e.astype(jnp.bfloat16), vw, preferred_element_type=f32)
            num.append(pv[:, :LANES])
            den.append(pv[:, LANES:] + jnp.exp2(sink - m))
        og0 = jnp.where(lo, num[0], pltpu.roll(num[1], HEAD_DIM, 1)) / jnp.where(lo, den[0], den[1])
        og1 = jnp.where(lo, pltpu.roll(num[2], HEAD_DIM, 1), num[3]) / jnp.where(lo, den[2], den[3])
        y = jnp.concatenate([og0, og1], axis=-1) * z_ref[0, rows, :].astype(f32)
        y_ref[rows, 0:D_GROUP] = y.astype(y_ref.dtype)


def _nbr_tile(q_ref, z_ref, k_ref, v_ref, bias_ref, y_ref, i):
    rows_total = k_ref.shape[1] // GRID_W
    rows_tile = T_ATTN // GRID_W
    nkeys = NA_KH * GRID_W
    f32 = jnp.float32
    lane = lax.broadcasted_iota(jnp.int32, (1, LANES), 1)
    lo = lane < HEAD_DIM
    zero = jnp.zeros((), jnp.bfloat16)
    ones = jnp.ones((nkeys, LANES), jnp.bfloat16)

    for rr in range(rows_tile):
        r = i * rows_tile + rr
        r0 = jnp.clip(r - NA_KH // 2, 0, rows_total - NA_KH)
        d0 = r0 - r + (NA_KH - 1)
        ks = pl.multiple_of(r0 * GRID_W, GRID_W)
        qrow = slice(rr * GRID_W, (rr + 1) * GRID_W)
        outs = []
        for g in range(2):
            cols = slice(g * LANES, (g + 1) * LANES)
            qg = q_ref[0, qrow, cols]
            lhs = jnp.concatenate([jnp.where(lo, qg, zero), jnp.where(lo, zero, qg)], axis=0)
            kw = k_ref[0, pl.ds(ks, nkeys), cols]
            vw = jnp.concatenate([v_ref[0, pl.ds(ks, nkeys), cols], ones], axis=1)
            sc = jnp.einsum("qd,kd->qk", lhs, kw, preferred_element_type=f32)
            bias = jnp.concatenate(
                [jnp.concatenate([bias_ref[0, 2 * g + hh, d0 + 2 * m]
                                  for m in range(NA_KH // 2)], axis=-1)
                 for hh in range(2)], axis=0)
            sc = sc + bias
            m_ = jnp.max(sc, axis=-1, keepdims=True)
            e = jnp.exp2(sc - m_)
            pv = jnp.dot(e.astype(jnp.bfloat16), vw, preferred_element_type=f32)
            pv = jnp.where(jnp.concatenate([lo, lo], axis=1), pv[0:GRID_W], pv[GRID_W:2 * GRID_W])
            outs.append(pv[:, :LANES] / pv[:, LANES:])
        y = jnp.concatenate(outs, axis=-1) * z_ref[0, qrow, :].astype(f32)
        y_ref[qrow, D_GROUP:2 * D_GROUP] = y.astype(y_ref.dtype)


def _attn_kernel(sink_ref, ct_ref, ckv_ref, dt_ref, dkv_ref, bias_ref, o_ref, *, layer):
    i = pl.program_id(1)
    y_ref = o_ref.at[0]
    cols = lambda ref, c: ref.at[:, :, pl.ds(c, D_GROUP)]
    _swa_tile(sink_ref, cols(ct_ref, P_CQ), cols(ct_ref, P_CQS), cols(ct_ref, P_ZC), ckv_ref,
              y_ref, i, layer)
    _nbr_tile(cols(dt_ref, P_DQ), cols(dt_ref, P_ZD), cols(dkv_ref, P_DK), cols(dkv_ref, P_DV),
              bias_ref, y_ref, i)


def _attn(c_tile, c_kv, d_tile, d_kv, sink, bias, layer):
    b, s, _ = c_tile.shape
    t = T_ATTN
    tile = lambda a: pl.BlockSpec((1, t, a.shape[2]), lambda bi, i, sk: (bi, i, 0))
    seq = lambda a: pl.BlockSpec((1, s, a.shape[2]), lambda bi, i, sk: (bi, 0, 0))
    return pl.pallas_call(
        functools.partial(_attn_kernel, layer=layer),
        grid_spec=pltpu.PrefetchScalarGridSpec(
            num_scalar_prefetch=1,
            grid=(b, s // t),
            in_specs=[
                tile(c_tile), seq(c_kv), tile(d_tile), seq(d_kv),
                pl.BlockSpec((1,) + bias.shape[1:], lambda bi, i, sk: (layer, 0, 0, 0, 0)),
            ],
            out_specs=pl.BlockSpec((1, t, 2 * D_GROUP), lambda bi, i, sk: (bi, i, 0)),
        ),
        out_shape=jax.ShapeDtypeStruct((b, s, 2 * D_GROUP), jnp.bfloat16),
        compiler_params=_params(),
        name="attn",
    )(sink, c_tile, c_kv, d_tile, d_kv, bias)


def _rope_tables(s):
    inv_freq = ROPE_THETA ** (-jnp.arange(0, HEAD_DIM, 2, dtype=jnp.float32) / HEAD_DIM)
    inv_freq = jnp.tile(inv_freq, LANES // HALF)
    hi = jnp.arange(0, s, GRID_W, dtype=jnp.float32)[:, None, None] * inv_freq
    lo = jnp.arange(GRID_W, dtype=jnp.float32)[None, :, None] * inv_freq
    ch, sh, cl, sl = jnp.cos(hi), jnp.sin(hi), jnp.cos(lo), jnp.sin(lo)
    cos = ch * cl - sh * sl
    sin = sh * cl + ch * sl
    return cos.reshape(s, LANES), sin.reshape(s, LANES)


def _nbr_bias_table(rpb):
    nrel = 2 * NA_KW - 1
    ndr = 2 * NA_KH - 1
    c = np.arange(GRID_W)
    c0 = np.clip(c - NA_KW // 2, 0, GRID_W - NA_KW)
    col_ok = (c[None, :] >= c0[:, None]) & (c[None, :] < c0[:, None] + NA_KW)
    dc = np.clip(c[None, :] - c[:, None], -(NA_KW - 1), NA_KW - 1) + (NA_KW - 1)
    onehot = (dc[None] == np.arange(nrel)[:, None, None]).astype(np.float32)
    pick = np.zeros((2, nrel, GRID_W, 2, GRID_W), np.float32)
    pick[0, :, :, 0, :] = onehot
    pick[1, :, :, 1, :] = onehot
    pick = pick.reshape(2 * nrel, GRID_W, LANES)
    dr = np.arange(ndr + 1)[:, None] + np.arange(2)[None, :]
    valid = (dr < ndr)[:, None, :, None] & col_ok[None, :, None, :]
    valid = valid.reshape(ndr + 1, GRID_W, LANES)
    ext = jnp.pad(rpb, ((0, 0), (0, 0), (0, 2), (0, 0)))
    pair = jnp.concatenate([ext[:, :, :-1], ext[:, :, 1:]], axis=-1)
    full = jnp.einsum("lhdc,cqj->lhdqj", pair, jnp.asarray(pick),
                      precision=lax.Precision.HIGHEST)
    return jnp.where(jnp.asarray(valid), full * LOG2E, NEG_INF)


def kernel(x, norm_g, w_in, w_out, conv_a_w, conv_a_b, ln_a_g, ln_a_b, conv_b_w, swa_sink,
           na_rpb, final_norm_g):
    depth = norm_g.shape[0]
    s = x.shape[1]
    assert s % max(TM_PROJ, T_MIX, T_ATTN) == 0 and s % GRID_W == 0, x.shape
    assert s // GRID_W >= NA_KH and s >= 3 * SWA_BLOCK, x.shape
    assert w_in.shape[1:] == (x.shape[2], D_IN) and w_out.shape[1:] == (4 * D_GROUP, x.shape[2])
    cos_t, sin_t = _rope_tables(s)
    w_out_b = w_out.astype(jnp.bfloat16)
    bias = _nbr_bias_table(na_rpb)
    fg = final_norm_g.reshape(1, -1)
    for l in range(depth):
        proj_ab, c_tile, c_kv, d_tile, d_kv = _in_proj(x, norm_g, w_in, cos_t, sin_t, l)
        ycd = _attn(c_tile, c_kv, d_tile, d_kv, swa_sink, bias, l)
        x = _conv_out(proj_ab, ycd, x, conv_a_w, conv_a_b, ln_a_g, ln_a_b, conv_b_w,
                      w_out_b, fg, l, final_norm=(l == depth - 1))
    return x
```

```python
import functools
import math

import numpy as np
import jax
import jax.numpy as jnp
from jax import lax
from jax.experimental import pallas as pl
from jax.experimental.pallas import tpu as pltpu

D_GROUP = 256
HEAD_DIM = 64
HALF = HEAD_DIM // 2
GRID_W = 64
CONV_A_WIDTH = 31
CONV_A_PAD = (CONV_A_WIDTH - 1) // 2
SWA_WINDOW = 128
SWA_BLOCK = 128
SWA_HEADS = 4
NA_KH = 8
NA_KW = 16
ROPE_THETA = 10000.0
EPS = 1e-6
NEG_INF = -1e30
LOG2E = math.log2(math.e)
Q_SCALE = HEAD_DIM ** -0.5 * LOG2E

COL_A_U, COL_A_V, COL_A_Z = 0, 256, 512
COL_B_B, COL_B_C, COL_B_X, COL_B_Z = 768, 1024, 1280, 1536
COL_C_Q, COL_C_K, COL_C_V, COL_C_Z = 1792, 2048, 2176, 2304
COL_D_Q, COL_D_K, COL_D_V, COL_D_Z = 2560, 2816, 3072, 3328
D_IN = 3584

P_H, P_ZA = 0, 256
P_BZ, P_G = 512, 768
W_AB = 1024
P_CQ, P_CQS, P_ZC = 0, 256, 512
W_CT = 768
P_CK, P_CV = 0, 128
W_CKV = 256
P_DQ, P_ZD = 0, 256
W_DT = 512
P_DK, P_DV = 0, 256
W_DKV = 512

LANES = 128
SUBLANES = 8
HALO = 16
VMEM_LIMIT = 56 * 1024 * 1024

TM_PROJ = 1024
T_MIX = 1024
T_ATTN = 2048
PROJ_CHUNK = 512
PROJ_ROWS = 512
CONV_ROWS = 256


def _silu(x):
    return x * jax.nn.sigmoid(x)


def _params():
    return pltpu.CompilerParams(
        dimension_semantics=("parallel", "parallel"), vmem_limit_bytes=VMEM_LIMIT)


def _in_proj_kernel(x_ref, g_ref, w32_ref, cos_ref, sin_ref, oab_ref, oct_ref, ockv_ref, odt_ref,
                    odkv_ref, w_ref, *, layer):
    @pl.when((pl.program_id(0) == 0) & (pl.program_id(1) == 0))
    def _():
        for c0 in range(0, D_IN, PROJ_CHUNK):
            w_ref[:, c0:c0 + PROJ_CHUNK] = w32_ref[0, :, c0:c0 + PROJ_CHUNK].astype(w_ref.dtype)

    lane = lax.broadcasted_iota(jnp.int32, (1, LANES), 1)
    first_half = (lane % HEAD_DIM) < HALF
    for r0 in range(0, x_ref.shape[1], PROJ_ROWS):
        _in_proj_rows(x_ref, g_ref, cos_ref, sin_ref, oab_ref, oct_ref, ockv_ref, odt_ref, odkv_ref,
                      w_ref, first_half, r0, layer)


def _in_proj_rows(x_ref, g_ref, cos_ref, sin_ref, oab_ref, oct_ref, ockv_ref, odt_ref, odkv_ref,
                  w_ref, first_half, r0, layer):
    rows = pl.ds(r0, PROJ_ROWS)
    x = x_ref[0, rows, :]
    ms = jnp.mean(x * x, axis=-1, keepdims=True)
    h = (x * lax.rsqrt(ms + EPS) * g_ref[layer:layer + 1, :]).astype(jnp.bfloat16)

    def rope(t):
        partner = jnp.where(first_half, -pltpu.roll(t, LANES - HALF, 1), pltpu.roll(t, HALF, 1))
        return t * cos_ref[rows, :] + partner * sin_ref[rows, :]

    def store(ref, col, t):
        ref[0, rows, col:col + t.shape[1]] = t.astype(ref.dtype)

    def chunk(c0):
        return jnp.dot(h, w_ref[:, c0:c0 + PROJ_CHUNK], preferred_element_type=jnp.float32)

    g = D_GROUP
    acc = chunk(COL_A_U)
    store(oab_ref, P_H, acc[:, :g] * jax.nn.sigmoid(acc[:, g:]))
    acc = chunk(COL_A_Z)
    store(oab_ref, P_ZA, _silu(acc[:, :g]))
    b_gate = acc[:, g:]
    acc = chunk(COL_B_C)
    store(oab_ref, P_G, acc[:, :g] * acc[:, g:])
    acc = chunk(COL_B_Z)
    store(oab_ref, P_BZ, b_gate * _silu(acc[:, :g]))
    for s0 in range(0, g, LANES):
        t = rope(acc[:, g + s0:g + s0 + LANES]) * Q_SCALE
        store(oct_ref, P_CQ + s0, t)
        store(oct_ref, P_CQS + s0, pltpu.roll(t, HEAD_DIM, 1))
    acc = chunk(COL_C_K)
    store(ockv_ref, P_CK, rope(acc[:, :LANES]))
    store(ockv_ref, P_CV, acc[:, LANES:g])
    store(oct_ref, P_ZC, _silu(acc[:, g:]))
    acc = chunk(COL_D_Q)
    store(odt_ref, P_DQ, acc[:, :g] * Q_SCALE)
    store(odkv_ref, P_DK, acc[:, g:])
    acc = chunk(COL_D_V)
    store(odkv_ref, P_DV, acc[:, :g])
    store(odt_ref, P_ZD, _silu(acc[:, g:]))


def _in_proj(x, g, w, cos, sin, layer):
    b, s, d = x.shape
    tm = TM_PROJ
    widths = (W_AB, W_CT, W_CKV, W_DT, W_DKV)
    return pl.pallas_call(
        functools.partial(_in_proj_kernel, layer=layer),
        grid=(b, s // tm),
        in_specs=[
            pl.BlockSpec((1, tm, d), lambda bi, i: (bi, i, 0)),
            pl.BlockSpec(g.shape, lambda bi, i: (0, 0)),
            pl.BlockSpec((1, d, D_IN), lambda bi, i: (layer, 0, 0), pipeline_mode=pl.Buffered(1)),
            pl.BlockSpec((tm, LANES), lambda bi, i: (i, 0)),
            pl.BlockSpec((tm, LANES), lambda bi, i: (i, 0)),
        ],
        out_specs=[pl.BlockSpec((1, tm, wd), lambda bi, i: (bi, i, 0)) for wd in widths],
        out_shape=[jax.ShapeDtypeStruct((b, s, wd), jnp.bfloat16) for wd in widths],
        scratch_shapes=[pltpu.VMEM((d, D_IN), jnp.bfloat16)],
        compiler_params=pltpu.CompilerParams(
            dimension_semantics=("arbitrary", "arbitrary"), vmem_limit_bytes=VMEM_LIMIT),
        name="in_proj",
    )(x, g, w, cos, sin)


def _conv_out_kernel(cur_ref, prev_ref, next_ref, aw_ref, ab_ref, lg_ref, lb_ref, bw_ref,
                     ycd_ref, x_ref, w_ref, g_ref, o_ref, hbuf, gbuf, hph, gph,
                     *, layer, final_norm):
    i = pl.program_id(1)
    n = pl.num_programs(1)
    t = T_MIX
    f32 = jnp.float32
    has_prev = (i > 0).astype(f32)
    has_next = (i < n - 1).astype(f32)
    row = lambda ref, j: ref[layer, j:j + 1, :]
    vec = lambda ref: ref[layer:layer + 1, :]
    col = lambda ref, rows, c: ref[0, rows, c:c + D_GROUP].astype(f32)

    full = slice(None)
    hwin = jnp.concatenate([col(prev_ref, full, P_H) * has_prev, col(cur_ref, full, P_H),
                            col(next_ref, full, P_H) * has_next], axis=0)
    gwin = jnp.concatenate([col(prev_ref, full, P_G) * has_prev, col(cur_ref, full, P_G),
                            col(next_ref, full, P_G) * has_next], axis=0)
    hbuf[...] = hwin
    gbuf[...] = gwin

    nwin = t + 2 * HALO
    nph = nwin - SUBLANES
    for p in range(1, SUBLANES):
        hph[p - 1] = pltpu.roll(hwin, nwin - p, 0)[0:nph]
    gph[0] = pltpu.roll(gwin, nwin - 1, 0)[0:nph]
    gph[1] = pltpu.roll(gwin, nwin - (SUBLANES - 1), 0)[0:nph]

    def tap(off, rows):
        p = off % SUBLANES
        if p == 0:
            return hbuf[off:off + rows, :]
        return hph[p - 1, off - p:off - p + rows, :]

    rc = CONV_ROWS
    for r0 in range(0, t, rc):
        rows = slice(r0, r0 + rc)
        acc = jnp.broadcast_to(vec(ab_ref), (rc, D_GROUP))
        for j in range(CONV_A_WIDTH):
            acc = acc + aw_ref[layer, j:j + 1, :] * tap(HALO - CONV_A_PAD + j + r0, rc)
        mu = jnp.mean(acc, axis=-1, keepdims=True)
        xc = acc - mu
        var = jnp.mean(xc * xc, axis=-1, keepdims=True)
        hn = xc * lax.rsqrt(var + EPS) * vec(lg_ref) + vec(lb_ref)
        ya = _silu(hn) * col(cur_ref, rows, P_ZA)
        gm = HALO + r0 - SUBLANES
        conv = (row(bw_ref, 0) * gph[1, gm:gm + rc, :]
                + row(bw_ref, 1) * gbuf[HALO + r0:HALO + r0 + rc, :]
                + row(bw_ref, 2) * gph[0, HALO + r0:HALO + r0 + rc, :])
        yb = col(cur_ref, rows, P_BZ) * conv
        yab = jnp.concatenate([ya.astype(jnp.bfloat16), yb.astype(jnp.bfloat16)], axis=1)
        xo = x_ref[0, rows, :] + jnp.dot(ycd_ref[0, rows, :], w_ref[0, 2 * D_GROUP:4 * D_GROUP, :],
                                         preferred_element_type=f32)
        xo = xo + jnp.dot(yab, w_ref[0, 0:2 * D_GROUP, :], preferred_element_type=f32)
        if final_norm:
            ms = jnp.mean(xo * xo, axis=-1, keepdims=True)
            xo = xo * lax.rsqrt(ms + EPS) * g_ref[...]
        o_ref[0, rows, :] = xo


def _conv_out(proj, ycd, x, aw, ab, lg, lb, bw, w, g, layer, final_norm):
    b, s, d = x.shape
    t = T_MIX
    hb = t // HALO
    nh = s // HALO
    f32 = jnp.float32
    whole = lambda a: pl.BlockSpec(a.shape, lambda bi, i: (0,) * a.ndim)
    return pl.pallas_call(
        functools.partial(_conv_out_kernel, layer=layer, final_norm=final_norm),
        grid=(b, s // t),
        in_specs=[
            pl.BlockSpec((1, t, W_AB), lambda bi, i: (bi, i, 0)),
            pl.BlockSpec((1, HALO, W_AB), lambda bi, i: (bi, jnp.maximum(i * hb - 1, 0), 0)),
            pl.BlockSpec((1, HALO, W_AB), lambda bi, i: (bi, jnp.minimum((i + 1) * hb, nh - 1), 0)),
            whole(aw), whole(ab), whole(lg), whole(lb), whole(bw),
            pl.BlockSpec((1, t, 2 * D_GROUP), lambda bi, i: (bi, i, 0)),
            pl.BlockSpec((1, t, d), lambda bi, i: (bi, i, 0)),
            pl.BlockSpec((1,) + w.shape[1:], lambda bi, i: (layer, 0, 0)),
            pl.BlockSpec((1, d), lambda bi, i: (0, 0)),
        ],
        out_specs=pl.BlockSpec((1, t, d), lambda bi, i: (bi, i, 0)),
        out_shape=jax.ShapeDtypeStruct((b, s, d), f32),
        scratch_shapes=[pltpu.VMEM((t + 2 * HALO, D_GROUP), f32),
                        pltpu.VMEM((t + 2 * HALO, D_GROUP), f32),
                        pltpu.VMEM((SUBLANES - 1, t + 2 * HALO - SUBLANES, D_GROUP), f32),
                        pltpu.VMEM((2, t + 2 * HALO - SUBLANES, D_GROUP), f32)],
        compiler_params=_params(),
        name="conv_out",
    )(proj, proj, proj, aw, ab, lg, lb, bw, ycd, x, w, g)


def _swa_tile(sink_ref, q_ref, qs_ref, z_ref, kv_ref, y_ref, i, layer):
    s_len = kv_ref.shape[1]
    nkeys = 3 * SWA_BLOCK
    f32 = jnp.float32
    lane = lax.broadcasted_iota(jnp.int32, (1, LANES), 1)
    lo = lane < HEAD_DIM
    zero = jnp.zeros((), jnp.bfloat16)
    ones = jnp.ones((nkeys, LANES), jnp.bfloat16)
    for nb in range(T_ATTN // SWA_BLOCK):
        q0 = i * T_ATTN + nb * SWA_BLOCK
        ws = pl.multiple_of(jnp.clip(q0 - SWA_BLOCK, 0, s_len - nkeys), SWA_BLOCK)
        rows = slice(nb * SWA_BLOCK, (nb + 1) * SWA_BLOCK)
        win = pl.ds(ws, nkeys)
        lhs = [
            jnp.where(lo, q_ref[0, rows, 0:LANES], zero),
            jnp.where(lo, qs_ref[0, rows, 0:LANES], zero),
            jnp.where(lo, zero, qs_ref[0, rows, LANES:2 * LANES]),
            jnp.where(lo, zero, q_ref[0, rows, LANES:2 * LANES]),
        ]
        kw = kv_ref[0, win, 0:LANES]
        vw = jnp.concatenate([kv_ref[0, win, LANES:2 * LANES], ones], axis=1)
        qpos = q0 + lax.broadcasted_iota(jnp.int32, (SWA_BLOCK, nkeys), 0)
        kpos = ws + lax.broadcasted_iota(jnp.int32, (SWA_BLOCK, nkeys), 1)
        valid = jnp.abs(kpos - qpos) <= SWA_WINDOW
        num, den = [], []
        for h in range(SWA_HEADS):
            sc = jnp.einsum("qd,kd->qk", lhs[h], kw, preferred_element_type=f32)
            sh = jnp.where(valid, sc, NEG_INF)
            sink = sink_ref[layer, h] * LOG2E
            m = jnp.maximum(jnp.max(sh, axis=-1, keepdims=True), sink)
            e = jnp.exp2(sh - m)
            pv = jnp.dot(e.astype(jnp.bfloat16), vw, preferred_element_type=f32)
            num.append(pv[:, :LANES])
            den.append(pv[:, LANES:] + jnp.exp2(sink - m))
        og0 = jnp.where(lo, num[0], pltpu.roll(num[1], HEAD_DIM, 1)) / jnp.where(lo, den[0], den[1])
        og1 = jnp.where(lo, pltpu.roll(num[2], HEAD_DIM, 1), num[3]) / jnp.where(lo, den[2], den[3])
        y = jnp.concatenate([og0, og1], axis=-1) * z_ref[0, rows, :].astype(f32)
        y_ref[rows, 0:D_GROUP] = y.astype(y_ref.dtype)


def _nbr_tile(q_ref, z_ref, k_ref, v_ref, bias_ref, y_ref, i):
    rows_total = k_ref.shape[1] // GRID_W
    rows_tile = T_ATTN // GRID_W
    nkeys = NA_KH * GRID_W
    f32 = jnp.float32
    lane = lax.broadcasted_iota(jnp.int32, (1, LANES), 1)
    lo = lane < HEAD_DIM
    zero = jnp.zeros((), jnp.bfloat16)
    ones = jnp.ones((nkeys, LANES), jnp.bfloat16)

    for rr in range(rows_tile):
        r = i * rows_tile + rr
        r0 = jnp.clip(r - NA_KH // 2, 0, rows_total - NA_KH)
        d0 = r0 - r + (NA_KH - 1)
        ks = pl.multiple_of(r0 * GRID_W, GRID_W)
        qrow = slice(rr * GRID_W, (rr + 1) * GRID_W)
        outs = []
        for g in range(2):
            cols = slice(g * LANES, (g + 1) * LANES)
            qg = q_ref[0, qrow, cols]
            lhs = jnp.concatenate([jnp.where(lo, qg, zero), jnp.where(lo, zero, qg)], axis=0)
            kw = k_ref[0, pl.ds(ks, nkeys), cols]
            vw = jnp.concatenate([v_ref[0, pl.ds(ks, nkeys), cols], ones], axis=1)
            sc = jnp.einsum("qd,kd->qk", lhs, kw, preferred_element_type=f32)
            bias = jnp.concatenate(
                [jnp.concatenate([bias_ref[0, 2 * g + hh, d0 + 2 * m]
                                  for m in range(NA_KH // 2)], axis=-1)
                 for hh in range(2)], axis=0)
            sc = sc + bias
            m_ = jnp.max(sc, axis=-1, keepdims=True)
            e = jnp.exp2(sc - m_)
            pv = jnp.dot(e.astype(jnp.bfloat16), vw, preferred_element_type=f32)
            pv = jnp.where(jnp.concatenate([lo, lo], axis=1), pv[0:GRID_W], pv[GRID_W:2 * GRID_W])
            outs.append(pv[:, :LANES] / pv[:, LANES:])
        y = jnp.concatenate(outs, axis=-1) * z_ref[0, qrow, :].astype(f32)
        y_ref[qrow, D_GROUP:2 * D_GROUP] = y.astype(y_ref.dtype)


def _attn_kernel(sink_ref, ct_ref, ckv_ref, dt_ref, dkv_ref, bias_ref, o_ref, *, layer):
    i = pl.program_id(1)
    y_ref = o_ref.at[0]
    cols = lambda ref, c: ref.at[:, :, pl.ds(c, D_GROUP)]
    _swa_tile(sink_ref, cols(ct_ref, P_CQ), cols(ct_ref, P_CQS), cols(ct_ref, P_ZC), ckv_ref,
              y_ref, i, layer)
    _nbr_tile(cols(dt_ref, P_DQ), cols(dt_ref, P_ZD), cols(dkv_ref, P_DK), cols(dkv_ref, P_DV),
              bias_ref, y_ref, i)


def _attn(c_tile, c_kv, d_tile, d_kv, sink, bias, layer):
    b, s, _ = c_tile.shape
    t = T_ATTN
    tile = lambda a: pl.BlockSpec((1, t, a.shape[2]), lambda bi, i, sk: (bi, i, 0))
    seq = lambda a: pl.BlockSpec((1, s, a.shape[2]), lambda bi, i, sk: (bi, 0, 0))
    return pl.pallas_call(
        functools.partial(_attn_kernel, layer=layer),
        grid_spec=pltpu.PrefetchScalarGridSpec(
            num_scalar_prefetch=1,
            grid=(b, s // t),
            in_specs=[
                tile(c_tile), seq(c_kv), tile(d_tile), seq(d_kv),
                pl.BlockSpec((1,) + bias.shape[1:], lambda bi, i, sk: (layer, 0, 0, 0, 0)),
            ],
            out_specs=pl.BlockSpec((1, t, 2 * D_GROUP), lambda bi, i, sk: (bi, i, 0)),
        ),
        out_shape=jax.ShapeDtypeStruct((b, s, 2 * D_GROUP), jnp.bfloat16),
        compiler_params=_params(),
        name="attn",
    )(sink, c_tile, c_kv, d_tile, d_kv, bias)


def _rope_tables(s):
    inv_freq = ROPE_THETA ** (-jnp.arange(0, HEAD_DIM, 2, dtype=jnp.float32) / HEAD_DIM)
    inv_freq = jnp.tile(inv_freq, LANES // HALF)
    hi = jnp.arange(0, s, GRID_W, dtype=jnp.float32)[:, None, None] * inv_freq
    lo = jnp.arange(GRID_W, dtype=jnp.float32)[None, :, None] * inv_freq
    ch, sh, cl, sl = jnp.cos(hi), jnp.sin(hi), jnp.cos(lo), jnp.sin(lo)
    cos = ch * cl - sh * sl
    sin = sh * cl + ch * sl
    return cos.reshape(s, LANES), sin.reshape(s, LANES)


def _nbr_bias_table(rpb):
    nrel = 2 * NA_KW - 1
    ndr = 2 * NA_KH - 1
    c = np.arange(GRID_W)
    c0 = np.clip(c - NA_KW // 2, 0, GRID_W - NA_KW)
    col_ok = (c[None, :] >= c0[:, None]) & (c[None, :] < c0[:, None] + NA_KW)
    dc = np.clip(c[None, :] - c[:, None], -(NA_KW - 1), NA_KW - 1) + (NA_KW - 1)
    onehot = (dc[None] == np.arange(nrel)[:, None, None]).astype(np.float32)
    pick = np.zeros((2, nrel, GRID_W, 2, GRID_W), np.float32)
    pick[0, :, :, 0, :] = onehot
    pick[1, :, :, 1, :] = onehot
    pick = pick.reshape(2 * nrel, GRID_W, LANES)
    dr = np.arange(ndr + 1)[:, None] + np.arange(2)[None, :]
    valid = (dr < ndr)[:, None, :, None] & col_ok[None, :, None, :]
    valid = valid.reshape(ndr + 1, GRID_W, LANES)
    ext = jnp.pad(rpb, ((0, 0), (0, 0), (0, 2), (0, 0)))
    pair = jnp.concatenate([ext[:, :, :-1], ext[:, :, 1:]], axis=-1)
    full = jnp.einsum("lhdc,cqj->lhdqj", pair, jnp.asarray(pick),
                      precision=lax.Precision.HIGHEST)
    return jnp.where(jnp.asarray(valid), full * LOG2E, NEG_INF)


def kernel(x, norm_g, w_in, w_out, conv_a_w, conv_a_b, ln_a_g, ln_a_b, conv_b_w, swa_sink,
           na_rpb, final_norm_g):
    depth = norm_g.shape[0]
    s = x.shape[1]
    assert s % max(TM_PROJ, T_MIX, T_ATTN) == 0 and s % GRID_W == 0, x.shape
    assert s // GRID_W >= NA_KH and s >= 3 * SWA_BLOCK, x.shape
    assert w_in.shape[1:] == (x.shape[2], D_IN) and w_out.shape[1:] == (4 * D_GROUP, x.shape[2])
    cos_t, sin_t = _rope_tables(s)
    w_out_b = w_out.astype(jnp.bfloat16)
    bias = _nbr_bias_table(na_rpb)
    fg = final_norm_g.reshape(1, -1)
    for l in range(depth):
        proj_ab, c_tile, c_kv, d_tile, d_kv = _in_proj(x, norm_g, w_in, cos_t, sin_t, l)
        ycd = _attn(c_tile, c_kv, d_tile, d_kv, swa_sink, bias, l)
        x = _conv_out(proj_ab, ycd, x, conv_a_w, conv_a_b, ln_a_g, ln_a_b, conv_b_w,
                      w_out_b, fg, l, final_norm=(l == depth - 1))
    return x
```
